```python
import math, functools
import jax, jax.numpy as jnp
from jax import lax
import numpy as np

D_MODEL = 1024
BATCH = 8
SEQ = 2048
DEPTH = 2
DEC_BATCH = 32
DEC_SEQ = 1
PAST_LEN = 16384
PAGE_SIZE = 128

N_GROUP_HEADS = 4
HEAD_DIM = 64
GROUP_W = N_GROUP_HEADS * HEAD_DIM
MIX_W = 4 * GROUP_W
CONV_K = 31
GLA_LR = 16
GLA_TAU = 16.0
GLA_CHUNK = 16
RET_CHUNK = 128
FOX_BLOCK = 128
D_FF = 2816
N_EXPERTS = 8
TOP_K = 2
D_EXPERT = 3584
EPS = 1e-6
IN_SPLITS = (GROUP_W, GROUP_W,
             GROUP_W, GROUP_W, GROUP_W, N_GROUP_HEADS,
             GROUP_W, GROUP_W, GROUP_W, GLA_LR, GROUP_W,
             GROUP_W, GROUP_W, GROUP_W, GROUP_W)
IN_COLS = 13 * GROUP_W + N_GROUP_HEADS + GLA_LR

kernel_name = 'hybrid_conv_fox_gla_retnet_adaln_decoder_step'


def rms_norm(x, g):
    xf = x.astype(jnp.float32)
    y = xf * lax.rsqrt(jnp.mean(xf * xf, axis=-1, keepdims=True) + EPS)
    return (y * g.astype(jnp.float32)).astype(x.dtype)


def layer_norm(x, g, b):
    xf = x.astype(jnp.float32)
    mu = jnp.mean(xf, axis=-1, keepdims=True)
    xc = xf - mu
    var = jnp.mean(xc * xc, axis=-1, keepdims=True)
    return (xc * lax.rsqrt(var + EPS) * g.astype(jnp.float32) + b.astype(jnp.float32)).astype(x.dtype)


def rope(x, pos):
    half = HEAD_DIM // 2
    inv = 10000.0 ** (-jnp.arange(half, dtype=jnp.float32) / half)
    ang = pos[:, None] * inv[None, :]
    cos = jnp.cos(ang)[None, :, None, :]
    sin = jnp.sin(ang)[None, :, None, :]
    xf = x.astype(jnp.float32)
    x1, x2 = xf[..., :half], xf[..., half:]
    return jnp.concatenate([x1 * cos - x2 * sin, x2 * cos + x1 * sin], axis=-1)


def causal_dwconv(u, buf, w, b):
    full = jnp.concatenate([buf.astype(u.dtype), u], axis=1)
    y = lax.conv_general_dilated(full, w[:, None, :].astype(u.dtype), (1,), 'VALID',
                                 dimension_numbers=('NWC', 'WIO', 'NWC'),
                                 feature_group_count=u.shape[-1])
    return y + b, full[:, full.shape[1] - (CONV_K - 1):]


def to_chunks(x, c):
    b, l, h, d = x.shape
    return x.reshape(b, l // c, c, h, d).transpose(1, 0, 3, 2, 4)


def from_chunks(x):
    n, b, h, c, d = x.shape
    return x.transpose(1, 0, 3, 2, 4).reshape(b, n * c, h, d)


def gla_chunked(q, k, v, glog, s0):
    c = math.gcd(q.shape[1], GLA_CHUNK)
    causal = jnp.tril(jnp.ones((c, c), bool))[:, :, None]

    def step(S, inp):
        qc, kc, vc, gc = inp
        bcum = jnp.cumsum(gc, axis=2)
        diff = jnp.where(causal, bcum[:, :, :, None, :] - bcum[:, :, None, :, :], -jnp.inf)
        att = jnp.einsum('bhid,bhjd,bhijd->bhij', qc, kc, jnp.exp(diff))
        o = (jnp.einsum('bhij,bhjv->bhiv', att, vc)
             + jnp.einsum('bhid,bhdv->bhiv', qc * jnp.exp(bcum), S))
        b_last = bcum[:, :, -1:, :]
        S = (jnp.exp(b_last[:, :, 0, :, None]) * S
             + jnp.einsum('bhjd,bhjv->bhdv', kc * jnp.exp(b_last - bcum), vc))
        return S, o

    xs = tuple(to_chunks(t.astype(jnp.float32), c) for t in (q, k, v, glog))
    S, o = lax.scan(step, s0.astype(jnp.float32), xs)
    return from_chunks(o), S


def retention_chunked(q, k, v, s0):
    c = math.gcd(q.shape[1], RET_CHUNK)
    h = q.shape[2]
    lg = jnp.log(1.0 - jnp.exp2(-5.0 - jnp.arange(h, dtype=jnp.float32)))
    i = jnp.arange(c, dtype=jnp.float32)
    rel = (i[:, None] - i[None, :])[None]
    dmat = jnp.exp(jnp.where(rel >= 0, rel * lg[:, None, None], -jnp.inf))
    dq = jnp.exp((i + 1.0)[None, :] * lg[:, None])[..., None]
    dk = jnp.exp((c - 1.0 - i)[None, :] * lg[:, None])[..., None]
    ds = jnp.exp(c * lg)[:, None, None]

    def step(S, inp):
        qc, kc, vc = inp
        att = jnp.einsum('bhid,bhjd->bhij', qc, kc) * dmat
        o = (jnp.einsum('bhij,bhjv->bhiv', att, vc)
             + jnp.einsum('bhid,bhdv->bhiv', qc * dq, S))
        S = ds * S + jnp.einsum('bhjd,bhjv->bhdv', kc * dk, vc)
        return S, o

    xs = tuple(to_chunks(t.astype(jnp.float32), c) for t in (q, k, v))
    S, o = lax.scan(step, s0.astype(jnp.float32), xs)
    return from_chunks(o), S


def fox_prompt(q, k, v, logf):
    b, l, h, d = q.shape
    blk = math.gcd(l, FOX_BLOCK)
    nb = l // blk
    F = jnp.cumsum(logf, axis=1).transpose(0, 2, 1)
    kpos = jnp.arange(l)
    qb = q.reshape(b, nb, blk, h, d).transpose(1, 0, 2, 3, 4)
    Fb = F.reshape(b, h, nb, blk).transpose(2, 0, 1, 3)
    scale = d ** -0.5

    def one_block(args):
        qi, Fi, idx = args
        s = (jnp.einsum('bqhd,bkhd->bhqk', qi, k).astype(jnp.float32) * scale
             + (Fi[..., :, None] - F[:, :, None, :]))
        qpos = idx * blk + jnp.arange(blk)
        s = jnp.where(kpos[None, :] <= qpos[:, None], s, -jnp.inf)
        p = jax.nn.softmax(s, axis=-1).astype(v.dtype)
        return jnp.einsum('bhqk,bkhd->bqhd', p, v)

    o = lax.map(one_block, (qb, Fb, jnp.arange(nb)))
    return o.transpose(1, 0, 2, 3, 4).reshape(b, l, h, d)


def fox_sample(q, k, v, logf, k_past, v_past, logf_past):
    l = q.shape[1]
    p_len = k_past.shape[1]
    scale = q.shape[-1] ** -0.5
    gp = jnp.cumsum(logf_past.astype(jnp.float32), axis=1)
    gn = gp[:, -1:] + jnp.cumsum(logf, axis=1)
    gp = gp.transpose(0, 2, 1)
    gn = gn.transpose(0, 2, 1)
    s_past = (jnp.einsum('bqhd,bkhd->bhqk', q, k_past).astype(jnp.float32) * scale
              + (gn[..., :, None] - gp[..., None, :]))
    s_new = (jnp.einsum('bqhd,bkhd->bhqk', q, k).astype(jnp.float32) * scale
             + (gn[..., :, None] - gn[..., None, :]))
    s_new = jnp.where(jnp.tril(jnp.ones((l, l), bool)), s_new, -jnp.inf)
    p = jax.nn.softmax(jnp.concatenate([s_past, s_new], axis=-1), axis=-1)
    return (jnp.einsum('bhqk,bkhd->bqhd', p[..., :p_len].astype(v_past.dtype), v_past)
            + jnp.einsum('bhqk,bkhd->bqhd', p[..., p_len:].astype(v.dtype), v))


def mix_layer(h, pos, lw, conv_buf, gla_s, ret_s, past):
    b, l = h.shape[:2]
    nh, d = N_GROUP_HEADS, HEAD_DIM

    def heads(t):
        return t.reshape(b, l, nh, d)

    z = h @ lw['w_in']
    cuts = np.cumsum(IN_SPLITS)[:-1].tolist()
    (c_a, c_g, f_q, f_k, f_v, f_f, g_q, g_k, g_v, g_lr, g_o,
     r_q, r_k, r_v, r_g) = jnp.split(z, cuts, axis=-1)
    u = c_a * jax.nn.sigmoid(c_g)
    y, conv_new = causal_dwconv(u, conv_buf, lw['conv_w'], lw['conv_b'])
    out_a = jax.nn.silu(layer_norm(y, lw['conv_ln_g'], lw['conv_ln_b']))
    q = rms_norm(heads(f_q), lw['fox_qn_g'])
    k = rms_norm(heads(f_k), lw['fox_kn_g'])
    v = heads(f_v)
    logf = jax.nn.log_sigmoid(f_f.astype(jnp.float32) + lw['fox_fb'].astype(jnp.float32))
    if past is None:
        o_b = fox_prompt(q, k, v, logf)
    else:
        o_b = fox_sample(q, k, v, logf, *past)
    out_b = o_b.reshape(b, l, GROUP_W)
    glog = jax.nn.log_sigmoid((g_lr @ lw['gla_w_gate2'] + lw['gla_b_gate']).astype(jnp.float32)) / GLA_TAU
    o_c, gla_new = gla_chunked(heads(g_q) * HEAD_DIM ** -0.5, heads(g_k), heads(g_v), heads(glog), gla_s)
    out_c = rms_norm(o_c, lw['gla_norm_g'].reshape(nh, d)).reshape(b, l, GROUP_W) * jax.nn.silu(g_o)
    rq = rope(heads(r_q), pos)
    rk = rope(heads(r_k), pos) * HEAD_DIM ** -0.5
    o_d, ret_new = retention_chunked(rq, rk, heads(r_v), ret_s)
    out_d = rms_norm(o_d, lw['ret_norm_g'].reshape(nh, d)).reshape(b, l, GROUP_W) * jax.nn.silu(r_g)
    cat = jnp.concatenate([out_a, out_b, out_c, out_d], axis=-1).astype(h.dtype)
    return cat @ lw['w_out'], conv_new, k, v, logf, gla_new, ret_new


def swiglu(h, w1, w3, w2):
    return (jax.nn.silu(h @ w1) * (h @ w3)) @ w2


def moe(h, router, w1, w3, w2):
    logits = (h @ router).astype(jnp.float32)
    vals, idx = lax.top_k(logits, TOP_K)
    gates = jax.nn.softmax(vals, axis=-1)
    y = jnp.zeros_like(h)
    for e in range(N_EXPERTS):
        g_e = jnp.sum(jnp.where(idx == e, gates, 0.0), axis=-1, keepdims=True).astype(h.dtype)
        y = y + g_e * swiglu(h, w1[e], w3[e], w2[e])
    return y


def block(x, c, pos, lw, ffn, conv_buf, gla_s, ret_s, past):
    mod = jax.nn.silu(c) @ lw['w_ada'] + lw['b_ada']
    sh1, sc1, g1, sh2, sc2, g2 = jnp.split(mod[:, None, :], 6, axis=-1)
    h = rms_norm(x, lw['norm1_g']) * (1 + sc1) + sh1
    mix, conv_new, k, v, logf, gla_new, ret_new = mix_layer(h, pos, lw, conv_buf, gla_s, ret_s, past)
    x = x + g1 * mix
    h = rms_norm(x, lw['norm2_g']) * (1 + sc2) + sh2
    x = x + g2 * ffn(h)
    return x, (conv_new, k, v, logf, gla_new, ret_new)


def setup_inputs(seed: int = 0) -> dict:
    key = jax.random.key(seed)
    ks = iter(jax.random.split(key, 48))
    f32 = jnp.float32

    def nrm(shape, scale):
        return scale * jax.random.normal(next(ks), shape, f32)

    nh, d = N_GROUP_HEADS, HEAD_DIM
    n_pages = PAST_LEN // PAGE_SIZE
    n_used = DEC_BATCH * n_pages
    n_phys = n_used + max(1, n_used // 4)
    n_dense = (DEPTH + 1) // 2
    n_moe = DEPTH // 2
    page_table = jax.random.permutation(next(ks), n_phys)[:n_used].reshape(DEC_BATCH, n_pages).astype(jnp.int32)
    return {
        'x_prompt': nrm((BATCH, SEQ, D_MODEL), 1.0),
        'x_sample': nrm((DEC_BATCH, DEC_SEQ, D_MODEL), 1.0),
        'c_prompt': nrm((BATCH, D_MODEL), 1.0),
        'c_sample': nrm((DEC_BATCH, D_MODEL), 1.0),
        'state_conv': nrm((DEPTH, DEC_BATCH, CONV_K - 1, GROUP_W), 0.5),
        'cache_k': nrm((n_phys, DEPTH, PAGE_SIZE, nh, d), 1.0),
        'cache_v': nrm((n_phys, DEPTH, PAGE_SIZE, nh, d), 1.0),
        'cache_logf': jax.nn.log_sigmoid(3.0 + nrm((n_phys, DEPTH, PAGE_SIZE, nh), 0.5)),
        'state_gla': nrm((DEPTH, DEC_BATCH, nh, d, d), 0.3),
        'state_ret': nrm((DEPTH, DEC_BATCH, nh, d, d), 1.0),
        'page_table': page_table,
        'w_in': nrm((DEPTH, D_MODEL, IN_COLS), D_MODEL ** -0.5),
        'w_out': nrm((DEPTH, MIX_W, D_MODEL), MIX_W ** -0.5),
        'conv_w': nrm((DEPTH, CONV_K, GROUP_W), CONV_K ** -0.5),
        'conv_b': nrm((DEPTH, GROUP_W), 0.02),
        'conv_ln_g': 1.0 + nrm((DEPTH, GROUP_W), 0.02),
        'conv_ln_b': nrm((DEPTH, GROUP_W), 0.02),
        'fox_qn_g': 1.0 + nrm((DEPTH, HEAD_DIM), 0.02),
        'fox_kn_g': 1.0 + nrm((DEPTH, HEAD_DIM), 0.02),
        'fox_fb': 3.0 + nrm((DEPTH, N_GROUP_HEADS), 0.5),
        'gla_w_gate2': nrm((DEPTH, GLA_LR, GROUP_W), GLA_LR ** -0.5),
        'gla_b_gate': nrm((DEPTH, GROUP_W), 0.02),
        'gla_norm_g': 1.0 + nrm((DEPTH, GROUP_W), 0.02),
        'ret_norm_g': 1.0 + nrm((DEPTH, GROUP_W), 0.02),
        'norm1_g': 1.0 + nrm((DEPTH, D_MODEL), 0.02),
        'norm2_g': 1.0 + nrm((DEPTH, D_MODEL), 0.02),
        'w_ada': nrm((DEPTH, D_MODEL, 6 * D_MODEL), 0.5 * D_MODEL ** -0.5),
        'b_ada': nrm((DEPTH, 6 * D_MODEL), 0.02),
        'ffn_w1': nrm((n_dense, D_MODEL, D_FF), D_MODEL ** -0.5),
        'ffn_w3': nrm((n_dense, D_MODEL, D_FF), D_MODEL ** -0.5),
        'ffn_w2': nrm((n_dense, D_FF, D_MODEL), D_FF ** -0.5),
        'moe_router': nrm((n_moe, D_MODEL, N_EXPERTS), D_MODEL ** -0.5),
        'moe_w1': nrm((n_moe, N_EXPERTS, D_MODEL, D_EXPERT), D_MODEL ** -0.5),
        'moe_w3': nrm((n_moe, N_EXPERTS, D_MODEL, D_EXPERT), D_MODEL ** -0.5),
        'moe_w2': nrm((n_moe, N_EXPERTS, D_EXPERT, D_MODEL), D_EXPERT ** -0.5),
    }


def reference(x_prompt, x_sample, c_prompt, c_sample, state_conv, cache_k, cache_v, cache_logf,
              state_gla, state_ret, page_table, w_in, w_out, conv_w, conv_b, conv_ln_g, conv_ln_b,
              fox_qn_g, fox_kn_g, fox_fb, gla_w_gate2, gla_b_gate, gla_norm_g, ret_norm_g,
              norm1_g, norm2_g, w_ada, b_ada, ffn_w1, ffn_w3, ffn_w2,
              moe_router, moe_w1, moe_w3, moe_w2):
    nh, d = N_GROUP_HEADS, HEAD_DIM
    b, l = x_prompt.shape[:2]
    bd, ls = x_sample.shape[:2]
    p_len = page_table.shape[1] * cache_k.shape[2]
    pos_p = jnp.arange(l, dtype=jnp.float32)
    pos_s = p_len + jnp.arange(ls, dtype=jnp.float32)
    conv0 = jnp.zeros((b, CONV_K - 1, GROUP_W), x_prompt.dtype)
    rec0 = jnp.zeros((b, nh, d, d), jnp.float32)
    xp, xs = x_prompt, x_sample
    sp_list, ss_list = [], []
    for li in range(DEPTH):
        lw = {'w_in': w_in[li], 'w_out': w_out[li], 'conv_w': conv_w[li], 'conv_b': conv_b[li],
              'conv_ln_g': conv_ln_g[li], 'conv_ln_b': conv_ln_b[li], 'fox_qn_g': fox_qn_g[li],
              'fox_kn_g': fox_kn_g[li], 'fox_fb': fox_fb[li], 'gla_w_gate2': gla_w_gate2[li],
              'gla_b_gate': gla_b_gate[li], 'gla_norm_g': gla_norm_g[li], 'ret_norm_g': ret_norm_g[li],
              'norm1_g': norm1_g[li], 'norm2_g': norm2_g[li], 'w_ada': w_ada[li], 'b_ada': b_ada[li]}
        if li % 2 == 0:
            ffn = functools.partial(swiglu, w1=ffn_w1[li // 2], w3=ffn_w3[li // 2], w2=ffn_w2[li // 2])
        else:
            ffn = functools.partial(moe, router=moe_router[li // 2], w1=moe_w1[li // 2],
                                    w3=moe_w3[li // 2], w2=moe_w2[li // 2])
        past = (cache_k[page_table, li].reshape(bd, p_len, nh, d),
                cache_v[page_table, li].reshape(bd, p_len, nh, d),
                cache_logf[page_table, li].reshape(bd, p_len, nh))
        xp, sp = block(xp, c_prompt, pos_p, lw, ffn, conv0, rec0, rec0, None)
        xs, ss = block(xs, c_sample, pos_s, lw, ffn, state_conv[li], state_gla[li], state_ret[li], past)
        sp_list.append(sp)
        ss_list.append(ss)
    conv_p = jnp.stack([s[0] for s in sp_list], axis=0)
    conv_s = jnp.stack([s[0] for s in ss_list], axis=0)
    k_p = jnp.stack([s[1] for s in sp_list], axis=1)
    k_s = jnp.stack([s[1] for s in ss_list], axis=1)
    v_p = jnp.stack([s[2] for s in sp_list], axis=1)
    v_s = jnp.stack([s[2] for s in ss_list], axis=1)
    lf_p = jnp.stack([s[3] for s in sp_list], axis=1)
    lf_s = jnp.stack([s[3] for s in ss_list], axis=1)
    gla_p = jnp.stack([s[4] for s in sp_list], axis=0)
    gla_s = jnp.stack([s[4] for s in ss_list], axis=0)
    ret_p = jnp.stack([s[5] for s in sp_list], axis=0)
    ret_s = jnp.stack([s[5] for s in ss_list], axis=0)
    return (xp, xs, conv_p, conv_s, k_p, k_s, v_p, v_s, lf_p, lf_s, gla_p, gla_s, ret_p, ret_s)
```

```python
import functools
import math

import numpy as np
import jax
import jax.numpy as jnp
from jax import lax
from jax.experimental import pallas as pl
from jax.experimental.pallas import tpu as pltpu

F32 = jnp.float32
BF16 = jnp.bfloat16
HI = lax.Precision.HIGHEST

D_MODEL = 1024
NH = 4
HD = 64
GW = NH * HD
CONV_K = 31
GLA_LR = 16
GLA_TAU = 16.0
GLA_CHUNK = 16
RET_CHUNK = 128
N_EXPERTS = 8
EPS = 1e-6
N_MAIN = 13 * GW
TAIL_W = 128
IN_PAD = N_MAIN + TAIL_W
TAIL_BLK = N_MAIN // TAIL_W
(G_CA, G_CG, G_FQ, G_FK, G_FV, G_GQ, G_GK, G_GV, G_GO, G_RQ, G_RK, G_RV, G_RG) = range(13)
VMEM_LIMIT = 56 * 1024 * 1024
MOE_ROWS = 128
NEG = -1e30


def _cp(sem, vmem=VMEM_LIMIT):
    return pltpu.CompilerParams(dimension_semantics=sem, vmem_limit_bytes=vmem)


def _silu(x):
    return x * jax.nn.sigmoid(x)


def _log_sigmoid(x):
    return jnp.minimum(x, 0.0) - jnp.log1p(jnp.exp(-jnp.abs(x)))


def _dot(a, b):
    return jnp.dot(a, b, preferred_element_type=F32)


def _split3(x):
    hi = x.astype(BF16)
    r1 = x - hi.astype(F32)
    mid = r1.astype(BF16)
    lo = (r1 - mid.astype(F32)).astype(BF16)
    return hi, mid, lo


def _dot_sel(a, b, data):
    if data == "a":
        sel = b.astype(BF16)
        return sum(_dot(p, sel) for p in _split3(a))
    sel = a.astype(BF16)
    return sum(_dot(sel, p) for p in _split3(b))


def _head_ids(shape, dim):
    return lax.shift_right_logical(lax.broadcasted_iota(jnp.int32, shape, dim), int(math.log2(HD)))


def _dot_nt(a, b, **kw):
    return lax.dot_general(a, b, (((1,), (1,)), ((), ())), preferred_element_type=F32, **kw)


def _dot_tn(a, b, **kw):
    return lax.dot_general(a, b, (((0,), (0,)), ((), ())), preferred_element_type=F32, **kw)


def _modnorm(x, g, sc, sh):
    y = x * lax.rsqrt(jnp.mean(x * x, axis=-1, keepdims=True) + EPS)
    return (y * g) * (1.0 + sc) + sh


def _head_rms(x, hsum, g):
    ms = _dot_sel(x * x, hsum, "a") * (1.0 / HD)
    return x * lax.rsqrt(ms + EPS) * g


def _np_hsum():
    i = np.arange(GW) // HD
    return (i[:, None] == i[None, :]).astype(np.float32)


def _ada_kernel(c_ref, w_ref, b_ref, o_ref):
    c = c_ref[...]
    o_ref[...] = _dot(_silu(c).astype(BF16), w_ref[...].astype(BF16)) + b_ref[...]


def ada_mod(c_all, w_ada, b_ada):
    depth, d, n = w_ada.shape
    r = c_all.shape[0]
    tn = 1536
    return pl.pallas_call(
        _ada_kernel,
        grid=(depth, n // tn),
        in_specs=[pl.BlockSpec((r, d), lambda l, j: (0, 0)),
                  pl.BlockSpec((None, d, tn), lambda l, j: (l, 0, j)),
                  pl.BlockSpec((None, 1, tn), lambda l, j: (l, 0, j))],
        out_specs=pl.BlockSpec((None, r, tn), lambda l, j: (l, 0, j)),
        out_shape=jax.ShapeDtypeStruct((depth, r, n), F32),
        compiler_params=_cp(("arbitrary", "arbitrary")),
        name="ada_mod",
    )(c_all, w_ada, b_ada.reshape(depth, 1, n))


def _in_proj_kernel(x_ref, sc_ref, sh_ref, g_ref, w_ref, z_ref):
    h = _modnorm(x_ref[...], g_ref[...], sc_ref[...], sh_ref[...])
    z_ref[...] = _dot(h.astype(BF16), w_ref[...])


def _mod_spec(mod, tm, rows_per_group):
    r = mod.shape[1]
    tiles = max(rows_per_group // tm, 1)
    return pl.BlockSpec((None, r, mod.shape[2]), lambda i, *_: (i // tiles, 0, 0))


def in_proj(x, sc, sh, g, w, tm, rows_per_group):
    m, d = x.shape
    n = w.shape[1]
    return pl.pallas_call(
        _in_proj_kernel,
        grid=(m // tm,),
        in_specs=[pl.BlockSpec((tm, d), lambda i: (i, 0)),
                  _mod_spec(sc, tm, rows_per_group), _mod_spec(sh, tm, rows_per_group),
                  pl.BlockSpec((1, d), lambda i: (0, 0)),
                  pl.BlockSpec((d, n), lambda i: (0, 0))],
        out_specs=pl.BlockSpec((tm, n), lambda i: (i, 0)),
        out_shape=jax.ShapeDtypeStruct((m, n), F32),
        compiler_params=_cp(("arbitrary",)),
        name="in_proj",
    )(x, sc, sh, g, w)


def _out_proj_kernel(a_ref, b_ref, c_ref, d_ref, w_ref, x_ref, g1_ref, o_ref):
    cat = jnp.concatenate([a_ref[...], b_ref[...], c_ref[...], d_ref[...]], axis=1)
    o_ref[...] = x_ref[...] + g1_ref[...] * _dot(cat, w_ref[...])


def out_proj(parts, w, x, g1, tm, rows_per_group):
    m, d = x.shape
    return pl.pallas_call(
        _out_proj_kernel,
        grid=(m // tm,),
        in_specs=[pl.BlockSpec((tm, GW), lambda i: (i, 0))] * 4 + [
            pl.BlockSpec((4 * GW, d), lambda i: (0, 0)),
            pl.BlockSpec((tm, d), lambda i: (i, 0)),
            _mod_spec(g1, tm, rows_per_group)],
        out_specs=pl.BlockSpec((tm, d), lambda i: (i, 0)),
        out_shape=jax.ShapeDtypeStruct((m, d), F32),
        compiler_params=_cp(("arbitrary",)),
        name="out_proj",
    )(*parts, w, x, g1)


def _ffn_kernel(x_ref, sc_ref, sh_ref, g_ref, gate_ref, w1_ref, w3_ref, w2_ref, o_ref, h_scr, acc_scr):
    j = pl.program_id(1)

    @pl.when(j == 0)
    def _():
        h_scr[...] = _modnorm(x_ref[...], g_ref[...], sc_ref[...], sh_ref[...]).astype(BF16)
        acc_scr[...] = jnp.zeros_like(acc_scr)

    h = h_scr[...]
    act = (_silu(_dot(h, w1_ref[...])) * _dot(h, w3_ref[...])).astype(BF16)
    acc_scr[...] += _dot(act, w2_ref[...])

    @pl.when(j == pl.num_programs(1) - 1)
    def _():
        o_ref[...] = x_ref[...] + gate_ref[...] * acc_scr[...]


def ffn_dense(x, sc, sh, g, gate, w1, w3, w2, tm, tf, rows_per_group):
    m, d = x.shape
    ff = w1.shape[1]
    return pl.pallas_call(
        _ffn_kernel,
        grid=(m // tm, ff // tf),
        in_specs=[pl.BlockSpec((tm, d), lambda i, j: (i, 0)),
                  _mod_spec(sc, tm, rows_per_group), _mod_spec(sh, tm, rows_per_group),
                  pl.BlockSpec((1, d), lambda i, j: (0, 0)),
                  _mod_spec(gate, tm, rows_per_group),
                  pl.BlockSpec((d, tf), lambda i, j: (0, j)),
                  pl.BlockSpec((d, tf), lambda i, j: (0, j)),
                  pl.BlockSpec((tf, d), lambda i, j: (j, 0))],
        out_specs=pl.BlockSpec((tm, d), lambda i, j: (i, 0)),
        out_shape=jax.ShapeDtypeStruct((m, d), F32),
        scratch_shapes=[pltpu.VMEM((tm, d), BF16), pltpu.VMEM((tm, d), F32)],
        compiler_params=_cp(("arbitrary", "arbitrary")),
        name="ffn_dense",
    )(x, sc, sh, g, gate, w1, w3, w2)


def _route_kernel(n_valid, x_ref, sc_ref, sh_ref, g_ref, rt_ref, tri_ref, h_ref, rank_ref, gate_ref, cnt_ref):
    t = x_ref.shape[0]
    h = _modnorm(x_ref[...], g_ref[...], sc_ref[...], sh_ref[...])
    h_ref[...] = h.astype(BF16)
    logits = jnp.dot(h, rt_ref[...], preferred_element_type=F32, precision=HI).T[:N_EXPERTS]
    e_iota = lax.broadcasted_iota(jnp.int32, logits.shape, 0).astype(F32)
    m1 = jnp.max(logits, axis=0, keepdims=True)
    i1 = jnp.min(jnp.where(logits == m1, e_iota, float(N_EXPERTS)), axis=0, keepdims=True)
    sel1 = e_iota == i1
    rest = jnp.where(sel1, -jnp.inf, logits)
    m2 = jnp.max(rest, axis=0, keepdims=True)
    i2 = jnp.min(jnp.where(rest == m2, e_iota, float(N_EXPERTS)), axis=0, keepdims=True)
    sel2 = e_iota == i2
    e2 = jnp.exp(m2 - m1)
    den = 1.0 + e2
    gate = jnp.where(sel1, 1.0 / den, 0.0) + jnp.where(sel2, e2 / den, 0.0)
    tok = pl.program_id(0) * t + lax.broadcasted_iota(jnp.int32, logits.shape, 1)
    sel = jnp.logical_and(jnp.logical_or(sel1, sel2), tok < n_valid)
    incl = _dot(jnp.where(sel, 1.0, 0.0).astype(BF16), tri_ref[...])
    rank_ref[...] = jnp.where(sel, incl - 1.0, -1.0)
    gate_ref[...] = gate
    cnt_ref[...] = jnp.broadcast_to(incl[:, t - 1:t], cnt_ref.shape)


def _moe_kernel(nblk_ref, h_ref, rank_ref, gate_ref, x_ref, g2_ref, w1_ref, w3_ref, w2_ref, o_ref, xs_scr, y_scr):
    i, e, j = pl.program_id(0), pl.program_id(1), pl.program_id(2)
    nff = pl.num_programs(2)
    t = h_ref.shape[0]
    r = MOE_ROWS
    nb = nblk_ref[i * N_EXPERTS + e]
    row = lax.broadcasted_iota(jnp.int32, (r, t), 0).astype(F32)

    @pl.when(jnp.logical_and(e == 0, j == 0))
    def _():
        o_ref[...] = jnp.zeros_like(o_ref)

    @pl.when(j == 0)
    def _():
        def gather(s, c):
            r0 = pl.multiple_of(s * r, r)
            onehot = jnp.where(rank_ref[...] == row + (s * r).astype(F32), 1.0, 0.0).astype(BF16)
            xs_scr[pl.ds(r0, r), :] = _dot(onehot, h_ref[...]).astype(BF16)
            y_scr[pl.ds(r0, r), :] = jnp.zeros((r, y_scr.shape[1]), F32)
            return c
        lax.fori_loop(0, nb, gather, 0)

    def ffn(s, c):
        r0 = pl.multiple_of(s * r, r)
        xs = xs_scr[pl.ds(r0, r), :]
        act = (_silu(_dot(xs, w1_ref[...])) * _dot(xs, w3_ref[...])).astype(BF16)
        y_scr[pl.ds(r0, r), :] += _dot(act, w2_ref[...])
        return c
    lax.fori_loop(0, nb, ffn, 0)

    @pl.when(j == nff - 1)
    def _():
        def scatter(s, c):
            r0 = pl.multiple_of(s * r, r)
            hit = rank_ref[...] == row + (s * r).astype(F32)
            wgt = jnp.where(hit, gate_ref[...], 0.0).astype(BF16)
            o_ref[...] += _dot_tn(wgt, y_scr[pl.ds(r0, r), :].astype(BF16))
            return c
        lax.fori_loop(0, nb, scatter, 0)

    @pl.when(jnp.logical_and(e == N_EXPERTS - 1, j == nff - 1))
    def _():
        o_ref[...] = x_ref[...] + g2_ref[...] * o_ref[...]


def moe_ffn(x, sc, sh, g, gate2, router_t, w1, w3, w2, t, tf, rows_per_group, n_valid):
    m, d = x.shape
    nt = m // t
    ne, _, ff = w1.shape
    assert ne == N_EXPERTS
    tri = jnp.asarray(np.triu(np.ones((t, t), np.float32)), BF16)
    h, rank, gate, cnt = pl.pallas_call(
        functools.partial(_route_kernel, n_valid),
        grid=(nt,),
        in_specs=[pl.BlockSpec((t, d), lambda i: (i, 0)),
                  _mod_spec(sc, t, rows_per_group), _mod_spec(sh, t, rows_per_group),
                  pl.BlockSpec((1, d), lambda i: (0, 0)),
                  pl.BlockSpec((d, 128), lambda i: (0, 0)),
                  pl.BlockSpec((t, t), lambda i: (0, 0))],
        out_specs=[pl.BlockSpec((t, d), lambda i: (i, 0)),
                   pl.BlockSpec((None, ne, t), lambda i: (i, 0, 0)),
                   pl.BlockSpec((None, ne, t), lambda i: (i, 0, 0)),
                   pl.BlockSpec((None, ne, 128), lambda i: (i, 0, 0))],
        out_shape=[jax.ShapeDtypeStruct((m, d), BF16),
                   jax.ShapeDtypeStruct((nt, ne, t), F32),
                   jax.ShapeDtypeStruct((nt, ne, t), F32),
                   jax.ShapeDtypeStruct((nt, ne, 128), F32)],
        compiler_params=_cp(("arbitrary",)),
        name="moe_route",
    )(x, sc, sh, g, router_t, tri)
    nblk = ((cnt[:, :, 0].astype(jnp.int32) + (MOE_ROWS - 1)) // MOE_ROWS).reshape(nt * ne)
    rank = rank.reshape(nt, ne, 1, t)
    gate = gate.reshape(nt, ne, 1, t)
    return pl.pallas_call(
        _moe_kernel,
        grid_spec=pltpu.PrefetchScalarGridSpec(
            num_scalar_prefetch=1,
            grid=(nt, ne, ff // tf),
            in_specs=[pl.BlockSpec((t, d), lambda i, e, j, nb: (i, 0)),
                      pl.BlockSpec((None, None, 1, t), lambda i, e, j, nb: (i, e, 0, 0)),
                      pl.BlockSpec((None, None, 1, t), lambda i, e, j, nb: (i, e, 0, 0)),
                      pl.BlockSpec((t, d), lambda i, e, j, nb: (i, 0)),
                      _mod_spec(gate2, t, rows_per_group),
                      pl.BlockSpec((None, d, tf), lambda i, e, j, nb: (e, 0, j)),
                      pl.BlockSpec((None, d, tf), lambda i, e, j, nb: (e, 0, j)),
                      pl.BlockSpec((None, tf, d), lambda i, e, j, nb: (e, j, 0))],
            out_specs=pl.BlockSpec((t, d), lambda i, e, j, nb: (i, 0)),
            scratch_shapes=[pltpu.VMEM((t, d), BF16), pltpu.VMEM((t, d), F32)]),
        out_shape=jax.ShapeDtypeStruct((m, d), F32),
        compiler_params=_cp(("arbitrary", "arbitrary", "arbitrary")),
        name="moe_ffn",
    )(nblk, h, rank, gate, x, gate2, w1, w3, w2)


def _conv_kernel(a_ref, g_ref, w_ref, b_ref, lg_ref, lb_ref, o_ref, cn_ref, ext_scr):
    tc = a_ref.shape[0]
    halo = 32
    off = halo - (CONV_K - 1)

    @pl.when(pl.program_id(1) == 0)
    def _():
        ext_scr[0:halo, :] = jnp.zeros((halo, GW), F32)

    ext_scr[halo:halo + tc, :] = a_ref[...] * jax.nn.sigmoid(g_ref[...])
    rc = 64
    for c in range(tc // rc):
        acc = jnp.zeros((rc, GW), F32)
        for k in range(CONV_K):
            acc = acc + w_ref[k:k + 1, :] * ext_scr[c * rc + k + off:c * rc + k + off + rc, :]
        y = acc + b_ref[...]
        mu = jnp.mean(y, axis=-1, keepdims=True)
        yc = y - mu
        var = jnp.mean(yc * yc, axis=-1, keepdims=True)
        o_ref[c * rc:(c + 1) * rc, :] = _silu(yc * lax.rsqrt(var + EPS) * lg_ref[...] + lb_ref[...]).astype(o_ref.dtype)
    cn_ref[...] = ext_scr[tc + off:tc + halo, :]
    ext_scr[0:halo, :] = ext_scr[tc:tc + halo, :]


def conv_prompt(z, nb, l, conv_w, conv_b, ln_g, ln_b, tc):
    m = z.shape[0]
    nl = l // tc
    row = lambda v: v.reshape(1, GW)
    return pl.pallas_call(
        _conv_kernel,
        grid=(nb, nl),
        in_specs=[pl.BlockSpec((tc, GW), lambda b, i: (b * nl + i, G_CA)),
                  pl.BlockSpec((tc, GW), lambda b, i: (b * nl + i, G_CG)),
                  pl.BlockSpec((CONV_K, GW), lambda b, i: (0, 0)),
                  pl.BlockSpec((1, GW), lambda b, i: (0, 0)),
                  pl.BlockSpec((1, GW), lambda b, i: (0, 0)),
                  pl.BlockSpec((1, GW), lambda b, i: (0, 0))],
        out_specs=[pl.BlockSpec((tc, GW), lambda b, i: (b * nl + i, 0)),
                   pl.BlockSpec((None, CONV_K - 1, GW), lambda b, i: (b, 0, 0))],
        out_shape=[jax.ShapeDtypeStruct((m, GW), BF16),
                   jax.ShapeDtypeStruct((nb, CONV_K - 1, GW), F32)],
        scratch_shapes=[pltpu.VMEM((tc + 32, GW), F32)],
        compiler_params=_cp(("arbitrary", "arbitrary")),
        name="conv_prompt",
    )(z, z, conv_w, row(conv_b), row(ln_g), row(ln_b))


def _fox_prep_kernel(q_ref, k_ref, v_ref, t_ref, gq_ref, gk_ref, fb_ref, hs_ref, tri_ref,
                     qb_ref, kb_ref, vb_ref, kf_ref, vf_ref, lf_ref, fc_ref, carry_scr):
    @pl.when(pl.program_id(1) == 0)
    def _():
        carry_scr[...] = jnp.zeros_like(carry_scr)

    hs = hs_ref[...]
    qn = _head_rms(q_ref[...], hs, gq_ref[...])
    kn = _head_rms(k_ref[...], hs, gk_ref[...])
    v = v_ref[...]
    qb_ref[...] = (qn * (HD ** -0.5)).astype(BF16)
    kb_ref[...] = kn.astype(BF16)
    vb_ref[...] = v.astype(BF16)
    kf_ref[...] = kn
    vf_ref[...] = v
    lf = _log_sigmoid(t_ref[...] + fb_ref[...])
    lf_ref[...] = lf
    cum = _dot_sel(tri_ref[...], lf, "b") + carry_scr[...]
    fc_ref[...] = cum
    carry_scr[...] = cum[cum.shape[0] - 1:, :]


def _fox_flash_kernel(q_ref, k_ref, v_ref, fc_ref, ft_ref, o_ref, m_scr, l_scr, acc_scr):
    tq = q_ref.shape[0]
    tk = tq
    qi = pl.program_id(1)
    q = q_ref[...]
    head = _head_ids((1, GW), 1)
    rpos = lax.broadcasted_iota(jnp.int32, (tq, tk), 0)
    cpos = lax.broadcasted_iota(jnp.int32, (tq, tk), 1)
    m_scr[...] = jnp.full(m_scr.shape, NEG, F32)
    l_scr[...] = jnp.zeros_like(l_scr)
    acc_scr[...] = jnp.zeros_like(acc_scr)
    fc = fc_ref[...]

    def body(jb, c):
        k0 = pl.multiple_of(jb * tk, tk)
        kb = k_ref[pl.ds(k0, tk), :]
        vb = v_ref[pl.ds(k0, tk), :]
        ft = ft_ref[jb]
        keep = (cpos + (jb - qi) * tk) <= rpos
        for h in range(NH):
            qh = jnp.where(head == h, q, jnp.zeros_like(q))
            s = _dot_nt(qh, kb) + (fc[:, h:h + 1] - ft[h:h + 1, :])
            s = jnp.where(keep, s, NEG)
            m_old = m_scr[h]
            m_new = jnp.maximum(m_old, jnp.max(s, axis=1, keepdims=True))
            alpha = jnp.exp(m_old - m_new)
            p = jnp.exp(s - m_new)
            l_scr[h] = alpha * l_scr[h] + jnp.sum(p, axis=1, keepdims=True)
            m_scr[h] = m_new
            pv = _dot(p.astype(BF16), vb)
            acc = acc_scr[...]
            acc_scr[...] = jnp.where(head == h, alpha * acc + pv, acc)
        return c

    lax.fori_loop(0, qi + 1, body, 0)
    inv = jnp.zeros((tq, GW), F32)
    for h in range(NH):
        inv = jnp.where(head == h, 1.0 / l_scr[h], inv)
    o_ref[...] = (acc_scr[...] * inv).astype(o_ref.dtype)


def fox_prompt(z, nb, l, gq, gk, fb, tp, tq):
    m = z.shape[0]
    nl = l // tp
    hs = jnp.asarray(_np_hsum())
    tri = jnp.asarray(np.tril(np.ones((tp, tp), np.float32)))
    row = lambda v: jnp.tile(v, NH).reshape(1, GW)
    fbp = jnp.zeros((1, TAIL_W), F32).at[0, :NH].set(fb)
    blk = lambda g: pl.BlockSpec((tp, GW), lambda b, i: (b * nl + i, g))
    oblk = lambda w: pl.BlockSpec((tp, w), lambda b, i: (b * nl + i, 0))
    qb, kb, vb, kf, vf, lf, fc = pl.pallas_call(
        _fox_prep_kernel,
        grid=(nb, nl),
        in_specs=[blk(G_FQ), blk(G_FK), blk(G_FV),
                  pl.BlockSpec((tp, TAIL_W), lambda b, i: (b * nl + i, TAIL_BLK)),
                  pl.BlockSpec((1, GW), lambda b, i: (0, 0)),
                  pl.BlockSpec((1, GW), lambda b, i: (0, 0)),
                  pl.BlockSpec((1, TAIL_W), lambda b, i: (0, 0)),
                  pl.BlockSpec((GW, GW), lambda b, i: (0, 0)),
                  pl.BlockSpec((tp, tp), lambda b, i: (0, 0))],
        out_specs=[oblk(GW)] * 5 + [oblk(TAIL_W)] * 2,
        out_shape=[jax.ShapeDtypeStruct((m, GW), BF16)] * 3 + [jax.ShapeDtypeStruct((m, GW), F32)] * 2
                  + [jax.ShapeDtypeStruct((m, TAIL_W), F32)] * 2,
        scratch_shapes=[pltpu.VMEM((1, TAIL_W), F32)],
        compiler_params=_cp(("arbitrary", "arbitrary")),
        name="fox_prep",
    )(z, z, z, z, row(gq), row(gk), fbp, hs, tri)
    nq = l // tq
    ft = jnp.pad(fc.reshape(nb, nq, tq, TAIL_W)[..., :NH].transpose(0, 1, 3, 2), ((0, 0), (0, 0), (0, 8 - NH), (0, 0)))
    o = pl.pallas_call(
        _fox_flash_kernel,
        grid=(nb, nq),
        in_specs=[pl.BlockSpec((tq, GW), lambda b, i: (b * nq + i, 0)),
                  pl.BlockSpec((l, GW), lambda b, i: (b, 0)),
                  pl.BlockSpec((l, GW), lambda b, i: (b, 0)),
                  pl.BlockSpec((tq, TAIL_W), lambda b, i: (b * nq + i, 0)),
                  pl.BlockSpec((None, nq, 8, tq), lambda b, i: (b, 0, 0, 0))],
        out_specs=pl.BlockSpec((tq, GW), lambda b, i: (b * nq + i, 0)),
        out_shape=jax.ShapeDtypeStruct((m, GW), BF16),
        scratch_shapes=[pltpu.VMEM((NH, tq, 1), F32), pltpu.VMEM((NH, tq, 1), F32), pltpu.VMEM((tq, GW), F32)],
        compiler_params=_cp(("arbitrary", "arbitrary")),
        name="fox_flash",
    )(qb, kb, vb, fc, ft)
    return o, kf, vf, lf[:, :NH]


def _gla_kernel(q_ref, k_ref, v_ref, t_ref, go_ref, w2_ref, bg_ref, gn_ref, hs_ref, cs_ref, bd_ref,
                o_ref, st_ref, st_scr, x_scr):
    tg = q_ref.shape[0]
    c = GLA_CHUNK

    @pl.when(pl.program_id(1) == 0)
    def _():
        st_scr[...] = jnp.zeros_like(st_scr)

    hs = hs_ref[...]
    hs_b = hs.astype(BF16)
    bd = bd_ref[...]
    glog = _log_sigmoid(_dot(t_ref[...].astype(BF16), w2_ref[...]) + bg_ref[...]) * (1.0 / GLA_TAU)
    cs = cs_ref[...]
    tri = jnp.where(lax.broadcasted_iota(jnp.int32, (tg, tg), 0) >= lax.broadcasted_iota(jnp.int32, (tg, tg), 1), cs, 0.0)
    bcum = _dot_sel(tri, glog, "b")
    blast = _dot_sel(cs, glog, "b")
    q = q_ref[...] * (HD ** -0.5)
    k = k_ref[...]
    v = v_ref[...]
    qe = (q * jnp.exp(bcum)).astype(BF16)
    ke = (k * jnp.exp(blast - bcum)).astype(BF16)
    vb = v.astype(BF16)
    ii = lax.broadcasted_iota(jnp.int32, (c, GW), 0)
    outs = []
    for n in range(tg // c):
        r0 = n * c
        bc = bcum[r0:r0 + c]
        qc = q[r0:r0 + c]
        kc = k[r0:r0 + c]
        vc = v[r0:r0 + c]
        for j in range(c):
            ex = jnp.exp(jnp.where(ii >= j, bc - bc[j:j + 1], -jnp.inf))
            x_scr[j * c:(j + 1) * c, :] = (qc * ex * kc[j:j + 1]).astype(BF16)
        att = _dot(x_scr[...], hs_b)
        o = _dot_nt(qe[r0:r0 + c], st_scr[...].astype(BF16))
        for j in range(c):
            o = o + att[j * c:(j + 1) * c] * vc[j:j + 1]
        outs.append(o)
        kv = _dot_tn(vb[r0:r0 + c], ke[r0:r0 + c])
        st_scr[...] = st_scr[...] * jnp.exp(blast[r0:r0 + 1]) + kv * bd
    o = jnp.concatenate(outs, axis=0)
    o_ref[...] = (_head_rms(o, hs, gn_ref[...]) * _silu(go_ref[...])).astype(o_ref.dtype)
    st_ref[...] = st_scr[...]


def gla_prompt(z, nb, l, w_gate2, b_gate, norm_g, tg):
    m = z.shape[0]
    nl = l // tg
    hs = jnp.asarray(_np_hsum())
    ch = np.arange(tg) // GLA_CHUNK
    cs = jnp.asarray((ch[:, None] == ch[None, :]).astype(np.float32))
    w2p = jnp.zeros((TAIL_W, GW), F32).at[NH:NH + GLA_LR].set(w_gate2).astype(BF16)
    blk = lambda g: pl.BlockSpec((tg, GW), lambda b, i: (b * nl + i, g))
    full = lambda r, c: pl.BlockSpec((r, c), lambda b, i: (0, 0))
    return pl.pallas_call(
        _gla_kernel,
        grid=(nb, nl),
        in_specs=[blk(G_GQ), blk(G_GK), blk(G_GV),
                  pl.BlockSpec((tg, TAIL_W), lambda b, i: (b * nl + i, TAIL_BLK)),
                  blk(G_GO), full(TAIL_W, GW), full(1, GW), full(1, GW), full(GW, GW), full(tg, tg), full(GW, GW)],
        out_specs=[pl.BlockSpec((tg, GW), lambda b, i: (b * nl + i, 0)),
                   pl.BlockSpec((None, GW, GW), lambda b, i: (b, 0, 0))],
        out_shape=[jax.ShapeDtypeStruct((m, GW), BF16), jax.ShapeDtypeStruct((nb, GW, GW), F32)],
        scratch_shapes=[pltpu.VMEM((GW, GW), F32), pltpu.VMEM((GLA_CHUNK * GLA_CHUNK, GW), BF16)],
        compiler_params=_cp(("arbitrary", "arbitrary")),
        name="gla_prompt",
    )(z, z, z, z, z, w2p, b_gate.reshape(1, GW), norm_g.reshape(1, GW), hs, cs, hs)


def _ret_lg_row():
    lg = np.log(1.0 - np.exp2(-5.0 - np.arange(NH, dtype=np.float32))).astype(np.float32)
    return np.repeat(lg, HD).reshape(1, GW)


def _rope(x, cos, sin_signed):
    first = (lax.broadcasted_iota(jnp.int32, (1, GW), 1) & (HD - 1)) < (HD // 2)
    swapped = jnp.where(first, pltpu.roll(x, GW - HD // 2, axis=1), pltpu.roll(x, HD // 2, axis=1))
    return x * cos + swapped * sin_signed


def _ret_kernel(q_ref, k_ref, v_ref, g_ref, cos_ref, sin_ref, lg_ref, gn_ref, hs_ref, o_ref, st_ref, st_scr):
    tr = q_ref.shape[0]

    @pl.when(pl.program_id(1) == 0)
    def _():
        st_scr[...] = jnp.zeros_like(st_scr)

    hs = hs_ref[...]
    lg = lg_ref[...]
    cos = cos_ref[...]
    sin = sin_ref[...]
    q = _rope(q_ref[...], cos, sin)
    k = _rope(k_ref[...], cos, sin) * (HD ** -0.5)
    v = v_ref[...]
    head = _head_ids((1, GW), 1)
    ri = lax.broadcasted_iota(jnp.int32, (tr, 1), 0).astype(F32)
    qd = (q * jnp.exp((ri + 1.0) * lg)).astype(BF16)
    kd = (k * jnp.exp((tr - 1.0 - ri) * lg)).astype(BF16)
    qb = q.astype(BF16)
    kb = k.astype(BF16)
    vb = v.astype(BF16)
    rel = (lax.broadcasted_iota(jnp.int32, (tr, tr), 0) - lax.broadcasted_iota(jnp.int32, (tr, tr), 1)).astype(F32)
    o = _dot_nt(qd, st_scr[...].astype(BF16))
    for h in range(NH):
        lgh = lg[:, h * HD:h * HD + 1]
        dmat = jnp.exp(jnp.where(rel >= 0, rel * lgh, -jnp.inf))
        att = _dot_nt(jnp.where(head == h, qb, jnp.zeros_like(qb)), kb) * dmat
        o = o + jnp.where(head == h, _dot(att.astype(BF16), vb), 0.0)
    kv = _dot_tn(vb, kd)
    st_scr[...] = st_scr[...] * jnp.exp(tr * lg) + kv * hs
    o_ref[...] = (_head_rms(o, hs, gn_ref[...]) * _silu(g_ref[...])).astype(o_ref.dtype)
    st_ref[...] = st_scr[...]


def _rope_tables(pos):
    half = HD // 2
    inv = 10000.0 ** (-jnp.arange(half, dtype=F32) / half)
    ang = pos[:, None] * inv[None, :]
    cos = jnp.cos(ang)
    sin = jnp.sin(ang)
    cos_t = jnp.tile(jnp.concatenate([cos, cos], axis=1), (1, NH))
    sin_t = jnp.tile(jnp.concatenate([-sin, sin], axis=1), (1, NH))
    return cos_t, sin_t


def ret_prompt(z, nb, l, norm_g, tr):
    m = z.shape[0]
    nl = l // tr
    hs = jnp.asarray(_np_hsum())
    cos_t, sin_t = _rope_tables(jnp.arange(l, dtype=F32))
    blk = lambda g: pl.BlockSpec((tr, GW), lambda b, i: (b * nl + i, g))
    full = lambda r, c: pl.BlockSpec((r, c), lambda b, i: (0, 0))
    tab = pl.BlockSpec((tr, GW), lambda b, i: (i, 0))
    return pl.pallas_call(
        _ret_kernel,
        grid=(nb, nl),
        in_specs=[blk(G_RQ), blk(G_RK), blk(G_RV), blk(G_RG), tab, tab, full(1, GW), full(1, GW), full(GW, GW)],
        out_specs=[pl.BlockSpec((tr, GW), lambda b, i: (b * nl + i, 0)),
                   pl.BlockSpec((None, GW, GW), lambda b, i: (b, 0, 0))],
        out_shape=[jax.ShapeDtypeStruct((m, GW), BF16), jax.ShapeDtypeStruct((nb, GW, GW), F32)],
        scratch_shapes=[pltpu.VMEM((GW, GW), F32)],
        compiler_params=_cp(("arbitrary", "arbitrary")),
        name="ret_prompt",
    )(z, z, z, z, cos_t, sin_t, jnp.asarray(_ret_lg_row()), norm_g.reshape(1, GW), hs)


def _state_from_blockdiag(st):
    nb = st.shape[0]
    s5 = st.reshape(nb, NH, HD, NH, HD)
    diag = jnp.stack([s5[:, h, :, h, :] for h in range(NH)], axis=1)
    return diag.transpose(0, 1, 3, 2)


def _sample_rows_kernel(z_ref, buf_ref, cw_ref, cb_ref, lg_ref, lb_ref, gq_ref, gk_ref, fb_ref, w2_ref, bg_ref,
                        cos_ref, sin_ref, rlg_ref, hs_ref,
                        oa_ref, cn_ref, fq_ref, fk_ref, lf_ref, gq_o, ga_o, rq_o, rk_o, ra_o):
    grp = lambda g: z_ref[:, g * GW:(g + 1) * GW]
    tail = z_ref[:, N_MAIN:N_MAIN + TAIL_W]
    hs = hs_ref[...]
    u = grp(G_CA) * jax.nn.sigmoid(grp(G_CG))
    y = cw_ref[CONV_K - 1:CONV_K, :] * u + cb_ref[...]
    for k in range(CONV_K - 1):
        y = y + cw_ref[k:k + 1, :] * buf_ref[k]
    mu = jnp.mean(y, axis=-1, keepdims=True)
    yc = y - mu
    var = jnp.mean(yc * yc, axis=-1, keepdims=True)
    oa_ref[...] = _silu(yc * lax.rsqrt(var + EPS) * lg_ref[...] + lb_ref[...]).astype(oa_ref.dtype)
    for k in range(CONV_K - 2):
        cn_ref[k] = buf_ref[k + 1]
    cn_ref[CONV_K - 2] = u
    fq_ref[...] = _head_rms(grp(G_FQ), hs, gq_ref[...]) * (HD ** -0.5)
    fk_ref[...] = _head_rms(grp(G_FK), hs, gk_ref[...])
    lf_ref[...] = _log_sigmoid(tail + fb_ref[...])
    glog = _log_sigmoid(_dot(tail.astype(BF16), w2_ref[...]) + bg_ref[...]) * (1.0 / GLA_TAU)
    gq_o[...] = grp(G_GQ) * (HD ** -0.5)
    ga_o[...] = jnp.exp(glog)
    rq_o[...] = _rope(grp(G_RQ), cos_ref[...], sin_ref[...])
    rk_o[...] = _rope(grp(G_RK), cos_ref[...], sin_ref[...]) * (HD ** -0.5)
    ra_o[...] = jnp.broadcast_to(jnp.exp(rlg_ref[...]), ra_o.shape)


def _rec_step_kernel(q_ref, k_ref, a_ref, v_ref, s_ref, o_ref, sn_ref):
    for h in range(NH):
        q = q_ref[h]
        k = k_ref[h]
        a = a_ref[h]
        v = v_ref[h]
        s = s_ref[h]
        qk = jnp.sum(q * k, axis=0, keepdims=True)
        o_ref[h] = qk * v + jnp.sum((q * a) * s, axis=0, keepdims=True)
        sn_ref[h] = a * s + k * v


def rec_step(q, k, a, v, state, li):
    nb = q.shape[0]
    col = lambda x: x.reshape(nb, NH, HD, 1)
    cspec = pl.BlockSpec((None, NH, HD, 1), lambda b: (b, 0, 0, 0))
    rspec = pl.BlockSpec((None, NH, 1, HD), lambda b: (b, 0, 0, 0))
    o, sn = pl.pallas_call(
        _rec_step_kernel,
        grid=(nb,),
        in_specs=[cspec, cspec, cspec, rspec,
                  pl.BlockSpec((None, None, NH, HD, HD), lambda b: (li, b, 0, 0, 0))],
        out_specs=[rspec, pl.BlockSpec((None, NH, HD, HD), lambda b: (b, 0, 0, 0))],
        out_shape=[jax.ShapeDtypeStruct((nb, NH, 1, HD), F32), jax.ShapeDtypeStruct((nb, NH, HD, HD), F32)],
        compiler_params=_cp(("arbitrary",)),
        name="rec_step",
    )(col(q), col(k), col(a), v.reshape(nb, NH, 1, HD), state)
    return o.reshape(nb, GW), sn


FOX_PAGES = 16


def _fox_sample_kernel(pt_ref, q_ref, kn_ref, vn_ref, lfn_ref, ts_ref, pre_ref, *rest):
    g_n = FOX_PAGES
    k_refs = rest[:g_n]
    v_refs = rest[g_n:2 * g_n]
    f_refs = rest[2 * g_n:3 * g_n]
    o_ref, m_scr, l_scr, acc_scr, r_scr, lf_scr, s_scr = rest[3 * g_n:]
    c = pl.program_id(1)
    pg = k_refs[0].shape[1]

    @pl.when(c == 0)
    def _():
        m_scr[...] = jnp.full(m_scr.shape, NEG, F32)
        l_scr[...] = jnp.zeros_like(l_scr)
        acc_scr[...] = jnp.zeros_like(acc_scr)
        r_scr[...] = jnp.zeros_like(r_scr)

    qb = jnp.broadcast_to(q_ref[...], (GW, pg))
    for g in range(g_n):
        prod = k_refs[g][...] * qb
        for h in range(NH):
            lf_scr[h * g_n + g:h * g_n + g + 1, :] = f_refs[g][h:h + 1, :]
            s_scr[h * g_n + g:h * g_n + g + 1, :] = jnp.sum(prod[h * HD:(h + 1) * HD], axis=0, keepdims=True)
    both = _dot_sel(lf_scr[...], ts_ref[...], "a")
    suf = both[:, :pg]
    tot = both[:, pg:]
    pre = _dot_sel(pre_ref[...], tot, "b")
    for h in range(NH):
        r0, r1 = h * g_n, (h + 1) * g_n
        base = lfn_ref[h:h + 1, :] + r_scr[h:h + 1, :]
        s_h = s_scr[r0:r1, :] + (base + pre[r0:r1] + suf[r0:r1])
        m_old = m_scr[h:h + 1, :]
        m_new = jnp.maximum(m_old, jnp.max(jnp.max(s_h, axis=0, keepdims=True), axis=1, keepdims=True))
        alpha = jnp.exp(m_old - m_new)
        p = jnp.exp(s_h - m_new)
        m_scr[h:h + 1, :] = m_new
        l_scr[h:h + 1, :] = l_scr[h:h + 1, :] * alpha + jnp.sum(p, axis=0, keepdims=True)
        r_scr[h:h + 1, :] = r_scr[h:h + 1, :] + jnp.sum(tot[r0:r1], axis=0, keepdims=True)
        acc = acc_scr[h * HD:(h + 1) * HD, :] * alpha
        for g in range(g_n):
            acc = acc + p[g:g + 1, :] * v_refs[g][h * HD:(h + 1) * HD, :]
        acc_scr[h * HD:(h + 1) * HD, :] = acc

    @pl.when(c == pl.num_programs(1) - 1)
    def _():
        prod = q_ref[...] * kn_ref[...]
        for h in range(NH):
            s_self = jnp.sum(prod[h * HD:(h + 1) * HD], axis=0, keepdims=True)
            m_h = m_scr[h:h + 1, 0:1]
            m_fin = jnp.maximum(m_h, s_self)
            a_h = jnp.exp(m_h - m_fin)
            p_self = jnp.exp(s_self - m_fin)
            l_tot = jnp.sum(l_scr[h:h + 1, :], axis=1, keepdims=True) * a_h + p_self
            num = jnp.sum(acc_scr[h * HD:(h + 1) * HD, :], axis=1, keepdims=True) * a_h \
                + p_self * vn_ref[h * HD:(h + 1) * HD, :]
            o_ref[h * HD:(h + 1) * HD, :] = num / l_tot


def fox_sample(q, k_new, v_new, lf_new, cache_kt, cache_vt, cache_lf, page_table, li):
    nb, n_pages = page_table.shape
    pg = cache_kt.shape[-1]
    g_n = FOX_PAGES
    nc = n_pages // g_n
    col = lambda x: x.reshape(nb, GW, 1)
    lfn = jnp.broadcast_to(lf_new[:, :, None], (nb, NH, pg))
    t = np.arange(pg)
    ts = jnp.asarray(np.concatenate([(t[:, None] > t[None, :]).astype(np.float32), np.ones((pg, pg), np.float32)], axis=1))
    r = np.arange(NH * g_n)
    pre = jnp.asarray(((r[:, None] // g_n == r[None, :] // g_n) & (r[None, :] < r[:, None])).astype(np.float32))

    def page(g):
        return lambda b, c, pt: (pt[b * n_pages + (n_pages - 1 - (c * g_n + g))], li, 0, 0)

    cspec = pl.BlockSpec((None, GW, 1), lambda b, c, pt: (b, 0, 0))
    in_specs = [cspec, cspec, cspec,
                pl.BlockSpec((None, NH, pg), lambda b, c, pt: (b, 0, 0)),
                pl.BlockSpec((pg, 2 * pg), lambda b, c, pt: (0, 0)),
                pl.BlockSpec((NH * g_n, NH * g_n), lambda b, c, pt: (0, 0))]
    in_specs += [pl.BlockSpec((None, None, GW, pg), page(g)) for g in range(g_n)]
    in_specs += [pl.BlockSpec((None, None, GW, pg), page(g)) for g in range(g_n)]
    in_specs += [pl.BlockSpec((None, None, NH, pg), page(g)) for g in range(g_n)]
    o = pl.pallas_call(
        _fox_sample_kernel,
        grid_spec=pltpu.PrefetchScalarGridSpec(
            num_scalar_prefetch=1,
            grid=(nb, nc),
            in_specs=in_specs,
            out_specs=pl.BlockSpec((None, GW, 1), lambda b, c, pt: (b, 0, 0)),
            scratch_shapes=[pltpu.VMEM((NH, pg), F32), pltpu.VMEM((NH, pg), F32),
                            pltpu.VMEM((GW, pg), F32), pltpu.VMEM((NH, pg), F32),
                            pltpu.VMEM((NH * g_n, pg), F32), pltpu.VMEM((NH * g_n, pg), F32)]),
        out_shape=jax.ShapeDtypeStruct((nb, GW, 1), F32),
        compiler_params=_cp(("arbitrary", "arbitrary")),
        name="fox_sample",
    )(page_table.reshape(-1), col(q), col(k_new), col(v_new), lfn, ts, pre,
      *([cache_kt] * g_n), *([cache_vt] * g_n), *([cache_lf] * g_n))
    return o.reshape(nb, GW)


def _sample_out_kernel(oa_ref, ob_ref, oc_ref, od_ref, go_ref, rg_ref, gng_ref, rng_ref, hs_ref, w_ref, x_ref, g1_ref, o_ref):
    hs = hs_ref[...]
    oc = _head_rms(oc_ref[...], hs, gng_ref[...]) * _silu(go_ref[...])
    od = _head_rms(od_ref[...], hs, rng_ref[...]) * _silu(rg_ref[...])
    cat = jnp.concatenate([oa_ref[...].astype(BF16), ob_ref[...].astype(BF16), oc.astype(BF16), od.astype(BF16)], axis=1)
    o_ref[...] = x_ref[...] + g1_ref[...] * _dot(cat, w_ref[...])


def _prep_w_in(w):
    o_ff = 5 * GW
    o_lr = o_ff + NH + 3 * GW
    main = jnp.concatenate([w[:, :o_ff], w[:, o_ff + NH:o_lr], w[:, o_lr + GLA_LR:]], axis=1)
    tail = jnp.concatenate([w[:, o_ff:o_ff + NH], w[:, o_lr:o_lr + GLA_LR],
                            jnp.zeros((w.shape[0], TAIL_W - NH - GLA_LR), w.dtype)], axis=1)
    return jnp.concatenate([main, tail], axis=1).astype(BF16)


def _ffn_apply(li, x, sc, sh, g, gate, fw, tm, rows_per_group, n_valid):
    if li % 2 == 0:
        w1, w3, w2 = fw
        return ffn_dense(x, sc, sh, g, gate, w1, w3, w2, tm, w1.shape[1] // 2, rows_per_group)
    router_t, w1, w3, w2 = fw
    return moe_ffn(x, sc, sh, g, gate, router_t, w1, w3, w2, tm, w1.shape[2] // 4, rows_per_group, n_valid)


def kernel(x_prompt, x_sample, c_prompt, c_sample, state_conv, cache_k, cache_v, cache_logf, state_gla, state_ret, page_table, w_in, w_out, conv_w, conv_b, conv_ln_g, conv_ln_b, fox_qn_g, fox_kn_g, fox_fb, gla_w_gate2, gla_b_gate, gla_norm_g, ret_norm_g, norm1_g, norm2_g, w_ada, b_ada, ffn_w1, ffn_w3, ffn_w2, moe_router, moe_w1, moe_w3, moe_w2):
    nbp, l, d = x_prompt.shape
    nbs = x_sample.shape[0]
    depth = w_in.shape[0]
    n_pages, pg = page_table.shape[1], cache_k.shape[2]
    p_len = n_pages * pg
    mp = nbp * l
    ms = 128

    mod = ada_mod(jnp.concatenate([c_prompt, c_sample], axis=0), w_ada, b_ada)
    mod = mod.reshape(depth, nbp + nbs, 6, d)
    cache_kt = cache_k.transpose(0, 1, 3, 4, 2).reshape(cache_k.shape[0], depth, GW, pg)
    cache_vt = cache_v.transpose(0, 1, 3, 4, 2).reshape(cache_v.shape[0], depth, GW, pg)
    cache_lf = cache_logf.transpose(0, 1, 3, 2)
    conv_state = state_conv.transpose(0, 2, 1, 3)
    cos_s, sin_s = _rope_tables(jnp.full((1,), p_len, F32))
    hs = jnp.asarray(_np_hsum())
    rlg = jnp.asarray(_ret_lg_row())

    xp = x_prompt.reshape(mp, d)
    xs = jnp.zeros((ms, d), F32).at[:nbs].set(x_sample.reshape(nbs, d))
    outs_p, outs_s = [], []
    for li in range(depth):
        mp_l = [mod[li, :nbp, i].reshape(nbp, 1, d) for i in range(6)]
        ms_l = [jnp.zeros((1, ms, d), F32).at[0, :nbs].set(mod[li, nbp:, i]) for i in range(6)]
        n1 = norm1_g[li].reshape(1, d)
        n2 = norm2_g[li].reshape(1, d)
        w_in_b = _prep_w_in(w_in[li])
        w_out_b = w_out[li].astype(BF16)
        if li % 2 == 0:
            fw = (ffn_w1[li // 2].astype(BF16), ffn_w3[li // 2].astype(BF16), ffn_w2[li // 2].astype(BF16))
        else:
            fw = (jnp.pad(moe_router[li // 2], ((0, 0), (0, 128 - N_EXPERTS))), moe_w1[li // 2].astype(BF16), moe_w3[li // 2].astype(BF16),
                  moe_w2[li // 2].astype(BF16))

        sh1, sc1, g1, sh2, sc2, g2 = mp_l
        z = in_proj(xp, sc1, sh1, n1, w_in_b, 512, l)
        out_a, conv_new = conv_prompt(z, nbp, l, conv_w[li], conv_b[li], conv_ln_g[li], conv_ln_b[li], 512)
        out_b, k_f, v_f, lf = fox_prompt(z, nbp, l, fox_qn_g[li], fox_kn_g[li], fox_fb[li], 512, 256)
        out_c, gla_st = gla_prompt(z, nbp, l, gla_w_gate2[li], gla_b_gate[li], gla_norm_g[li], 256)
        out_d, ret_st = ret_prompt(z, nbp, l, ret_norm_g[li], RET_CHUNK)
        xp = out_proj((out_a, out_b, out_c, out_d), w_out_b, xp, g1, 512, l)
        xp = _ffn_apply(li, xp, sc2, sh2, n2, g2, fw, 512 if li % 2 == 0 else 1024, l, mp)
        outs_p.append((conv_new, k_f.reshape(nbp, l, NH, HD), v_f.reshape(nbp, l, NH, HD), lf.reshape(nbp, l, NH),
                       _state_from_blockdiag(gla_st), _state_from_blockdiag(ret_st)))

        sh1, sc1, g1, sh2, sc2, g2 = ms_l
        zs = in_proj(xs, sc1, sh1, n1, w_in_b, ms, ms)[:nbs]
        row = lambda v_: v_.reshape(1, GW)
        w2p = jnp.zeros((TAIL_W, GW), F32).at[NH:NH + GLA_LR].set(gla_w_gate2[li]).astype(BF16)
        fbp = jnp.zeros((1, TAIL_W), F32).at[0, :NH].set(fox_fb[li])
        r32 = jax.ShapeDtypeStruct((nbs, GW), F32)
        (oa, cn, fq, fk, lfs, gq, ga, rq, rk, ra) = pl.pallas_call(
            _sample_rows_kernel,
            out_shape=[r32, jax.ShapeDtypeStruct((CONV_K - 1, nbs, GW), F32), r32, r32,
                       jax.ShapeDtypeStruct((nbs, TAIL_W), F32), r32, r32, r32, r32, r32],
            compiler_params=pltpu.CompilerParams(vmem_limit_bytes=VMEM_LIMIT),
            name="sample_rows",
        )(zs, conv_state[li], conv_w[li], row(conv_b[li]), row(conv_ln_g[li]), row(conv_ln_b[li]),
          row(jnp.tile(fox_qn_g[li], NH)), row(jnp.tile(fox_kn_g[li], NH)), fbp, w2p, row(gla_b_gate[li]),
          cos_s, sin_s, rlg, hs)
        grp = lambda g_: zs[:, g_ * GW:(g_ + 1) * GW]
        f_v, g_k, g_v, r_v = grp(G_FV), grp(G_GK), grp(G_GV), grp(G_RV)
        ob = fox_sample(fq, fk, f_v, lfs[:, :NH], cache_kt, cache_vt, cache_lf, page_table, li)
        oc, gla_new = rec_step(gq, g_k, ga, g_v, state_gla, li)
        od, ret_new = rec_step(rq, rk, ra, r_v, state_ret, li)
        pad = lambda v_: jnp.zeros((ms, v_.shape[1]), F32).at[:nbs].set(v_)
        xs = pl.pallas_call(
            _sample_out_kernel,
            out_shape=jax.ShapeDtypeStruct((ms, d), F32),
            compiler_params=pltpu.CompilerParams(vmem_limit_bytes=VMEM_LIMIT),
            name="sample_out",
        )(pad(oa), pad(ob), pad(oc), pad(od), pad(grp(G_GO)), pad(grp(G_RG)), row(gla_norm_g[li]), row(ret_norm_g[li]),
          hs, w_out_b, xs, g1[0])
        xs = _ffn_apply(li, xs, sc2, sh2, n2, g2, fw, ms, ms, nbs)
        outs_s.append((cn.transpose(1, 0, 2), fk.reshape(nbs, 1, NH, HD), f_v.reshape(nbs, 1, NH, HD),
                       lfs[:, :NH].reshape(nbs, 1, NH), gla_new, ret_new))

    st = lambda lst, i, ax: jnp.stack([s[i] for s in lst], axis=ax)
    return (xp.reshape(nbp, l, d), xs[:nbs].reshape(nbs, 1, d),
            st(outs_p, 0, 0), st(outs_s, 0, 0),
            st(outs_p, 1, 1), st(outs_s, 1, 1),
            st(outs_p, 2, 1), st(outs_s, 2, 1),
            st(outs_p, 3, 1), st(outs_s, 3, 1),
            st(outs_p, 4, 0), st(outs_s, 4, 0),
            st(outs_p, 5, 0), st(outs_s, 5, 0))
```

```python
import functools
import math

import numpy as np
import jax
import jax.numpy as jnp
from jax import lax
from jax.experimental import pallas as pl
from jax.experimental.pallas import tpu as pltpu

F32 = jnp.float32
BF16 = jnp.bfloat16
HI = lax.Precision.HIGHEST

D_MODEL = 1024
NH = 4
HD = 64
GW = NH * HD
CONV_K = 31
GLA_LR = 16
GLA_TAU = 16.0
GLA_CHUNK = 16
RET_CHUNK = 128
N_EXPERTS = 8
EPS = 1e-6
N_MAIN = 13 * GW
TAIL_W = 128
IN_PAD = N_MAIN + TAIL_W
TAIL_BLK = N_MAIN // TAIL_W
(G_CA, G_CG, G_FQ, G_FK, G_FV, G_GQ, G_GK, G_GV, G_GO, G_RQ, G_RK, G_RV, G_RG) = range(13)
VMEM_LIMIT = 56 * 1024 * 1024
MOE_ROWS = 128
NEG = -1e30


def _cp(sem, vmem=VMEM_LIMIT):
    return pltpu.CompilerParams(dimension_semantics=sem, vmem_limit_bytes=vmem)


def _silu(x):
    return x * jax.nn.sigmoid(x)


def _log_sigmoid(x):
    return jnp.minimum(x, 0.0) - jnp.log1p(jnp.exp(-jnp.abs(x)))


def _dot(a, b):
    return jnp.dot(a, b, preferred_element_type=F32)


def _split3(x):
    hi = x.astype(BF16)
    r1 = x - hi.astype(F32)
    mid = r1.astype(BF16)
    lo = (r1 - mid.astype(F32)).astype(BF16)
    return hi, mid, lo


def _dot_sel(a, b, data):
    if data == "a":
        sel = b.astype(BF16)
        return sum(_dot(p, sel) for p in _split3(a))
    sel = a.astype(BF16)
    return sum(_dot(sel, p) for p in _split3(b))


def _head_ids(shape, dim):
    return lax.shift_right_logical(lax.broadcasted_iota(jnp.int32, shape, dim), int(math.log2(HD)))


def _dot_nt(a, b, **kw):
    return lax.dot_general(a, b, (((1,), (1,)), ((), ())), preferred_element_type=F32, **kw)


def _dot_tn(a, b, **kw):
    return lax.dot_general(a, b, (((0,), (0,)), ((), ())), preferred_element_type=F32, **kw)


def _wdot(a, w, nt=False):
    if w.dtype == BF16:
        a, kw = a.astype(BF16), {}
    else:
        kw = {"precision": HI}
    return _dot_nt(a, w, **kw) if nt else jnp.dot(a, w, preferred_element_type=F32, **kw)


def _modnorm(x, g, sc, sh):
    y = x * lax.rsqrt(jnp.mean(x * x, axis=-1, keepdims=True) + EPS)
    return (y * g) * (1.0 + sc) + sh


def _head_rms(x, hsum, g):
    ms = _dot_sel(x * x, hsum, "a") * (1.0 / HD)
    return x * lax.rsqrt(ms + EPS) * g


def _np_hsum():
    i = np.arange(GW) // HD
    return (i[:, None] == i[None, :]).astype(np.float32)


def _ada_kernel(c_ref, w_ref, b_ref, o_ref):
    c = c_ref[...]
    o_ref[...] = _wdot(_silu(c), w_ref[...]) + b_ref[...]


def ada_mod(c_all, w_ada, b_ada):
    depth, d, n = w_ada.shape
    r = c_all.shape[0]
    tn = 1536
    return pl.pallas_call(
        _ada_kernel,
        grid=(depth, n // tn),
        in_specs=[pl.BlockSpec((r, d), lambda l, j: (0, 0)),
                  pl.BlockSpec((None, d, tn), lambda l, j: (l, 0, j)),
                  pl.BlockSpec((None, 1, tn), lambda l, j: (l, 0, j))],
        out_specs=pl.BlockSpec((None, r, tn), lambda l, j: (l, 0, j)),
        out_shape=jax.ShapeDtypeStruct((depth, r, n), F32),
        compiler_params=_cp(("arbitrary", "arbitrary")),
        name="ada_mod",
    )(c_all, w_ada, b_ada.reshape(depth, 1, n))


def _in_proj_kernel(x_ref, sc_ref, sh_ref, g_ref, w_ref, z_ref):
    h = _modnorm(x_ref[...], g_ref[...], sc_ref[...], sh_ref[...])
    z_ref[...] = _wdot(h, w_ref[...], nt=True)


def _mod_spec(mod, tm, rows_per_group):
    r = mod.shape[1]
    tiles = max(rows_per_group // tm, 1)
    return pl.BlockSpec((None, r, mod.shape[2]), lambda i, *_: (i // tiles, 0, 0))


def in_proj(x, sc, sh, g, w, tm, rows_per_group):
    m, d = x.shape
    n = w.shape[0]
    return pl.pallas_call(
        _in_proj_kernel,
        grid=(m // tm,),
        in_specs=[pl.BlockSpec((tm, d), lambda i: (i, 0)),
                  _mod_spec(sc, tm, rows_per_group), _mod_spec(sh, tm, rows_per_group),
                  pl.BlockSpec((1, d), lambda i: (0, 0)),
                  pl.BlockSpec((n, d), lambda i: (0, 0))],
        out_specs=pl.BlockSpec((tm, n), lambda i: (i, 0)),
        out_shape=jax.ShapeDtypeStruct((m, n), F32),
        compiler_params=_cp(("arbitrary",)),
        name="in_proj",
    )(x, sc, sh, g, w)


def _out_proj_kernel(a_ref, b_ref, c_ref, d_ref, w_ref, x_ref, g1_ref, o_ref):
    cat = jnp.concatenate([a_ref[...], b_ref[...], c_ref[...], d_ref[...]], axis=1)
    o_ref[...] = x_ref[...] + g1_ref[...] * _dot(cat, w_ref[...])


def out_proj(parts, w, x, g1, tm, rows_per_group):
    m, d = x.shape
    return pl.pallas_call(
        _out_proj_kernel,
        grid=(m // tm,),
        in_specs=[pl.BlockSpec((tm, GW), lambda i: (i, 0))] * 4 + [
            pl.BlockSpec((4 * GW, d), lambda i: (0, 0)),
            pl.BlockSpec((tm, d), lambda i: (i, 0)),
            _mod_spec(g1, tm, rows_per_group)],
        out_specs=pl.BlockSpec((tm, d), lambda i: (i, 0)),
        out_shape=jax.ShapeDtypeStruct((m, d), F32),
        compiler_params=_cp(("arbitrary",)),
        name="out_proj",
    )(*parts, w, x, g1)


def _ffn_kernel(x_ref, sc_ref, sh_ref, g_ref, gate_ref, w1_ref, w3_ref, w2_ref, o_ref, h_scr, acc_scr):
    j = pl.program_id(1)

    @pl.when(j == 0)
    def _():
        h_scr[...] = _modnorm(x_ref[...], g_ref[...], sc_ref[...], sh_ref[...]).astype(h_scr.dtype)
        acc_scr[...] = jnp.zeros_like(acc_scr)

    h = h_scr[...]
    acc_scr[...] += _wdot(_silu(_wdot(h, w1_ref[...])) * _wdot(h, w3_ref[...]), w2_ref[...])

    @pl.when(j == pl.num_programs(1) - 1)
    def _():
        o_ref[...] = x_ref[...] + gate_ref[...] * acc_scr[...]


def ffn_dense(x, sc, sh, g, gate, w1, w3, w2, tm, tf, rows_per_group):
    m, d = x.shape
    ff = w1.shape[1]
    return pl.pallas_call(
        _ffn_kernel,
        grid=(m // tm, ff // tf),
        in_specs=[pl.BlockSpec((tm, d), lambda i, j: (i, 0)),
                  _mod_spec(sc, tm, rows_per_group), _mod_spec(sh, tm, rows_per_group),
                  pl.BlockSpec((1, d), lambda i, j: (0, 0)),
                  _mod_spec(gate, tm, rows_per_group),
                  pl.BlockSpec((d, tf), lambda i, j: (0, j)),
                  pl.BlockSpec((d, tf), lambda i, j: (0, j)),
                  pl.BlockSpec((tf, d), lambda i, j: (j, 0))],
        out_specs=pl.BlockSpec((tm, d), lambda i, j: (i, 0)),
        out_shape=jax.ShapeDtypeStruct((m, d), F32),
        scratch_shapes=[pltpu.VMEM((tm, d), w1.dtype), pltpu.VMEM((tm, d), F32)],
        compiler_params=_cp(("arbitrary", "arbitrary")),
        name="ffn_dense",
    )(x, sc, sh, g, gate, w1, w3, w2)


def _route_kernel(n_valid, x_ref, sc_ref, sh_ref, g_ref, rt_ref, tri_ref, h_ref, rank_ref, gate_ref, cnt_ref):
    t = x_ref.shape[0]
    h = _modnorm(x_ref[...], g_ref[...], sc_ref[...], sh_ref[...])
    h_ref[...] = h.astype(BF16)
    logits = jnp.dot(h, rt_ref[...], preferred_element_type=F32, precision=HI).T[:N_EXPERTS]
    e_iota = lax.broadcasted_iota(jnp.int32, logits.shape, 0).astype(F32)
    m1 = jnp.max(logits, axis=0, keepdims=True)
    i1 = jnp.min(jnp.where(logits == m1, e_iota, float(N_EXPERTS)), axis=0, keepdims=True)
    sel1 = e_iota == i1
    rest = jnp.where(sel1, -jnp.inf, logits)
    m2 = jnp.max(rest, axis=0, keepdims=True)
    i2 = jnp.min(jnp.where(rest == m2, e_iota, float(N_EXPERTS)), axis=0, keepdims=True)
    sel2 = e_iota == i2
    e2 = jnp.exp(m2 - m1)
    den = 1.0 + e2
    gate = jnp.where(sel1, 1.0 / den, 0.0) + jnp.where(sel2, e2 / den, 0.0)
    tok = pl.program_id(0) * t + lax.broadcasted_iota(jnp.int32, logits.shape, 1)
    sel = jnp.logical_and(jnp.logical_or(sel1, sel2), tok < n_valid)
    incl = _dot(jnp.where(sel, 1.0, 0.0).astype(BF16), tri_ref[...])
    rank_ref[...] = jnp.where(sel, incl - 1.0, -1.0)
    gate_ref[...] = gate
    cnt_ref[...] = jnp.broadcast_to(incl[:, t - 1:t], cnt_ref.shape)


def _moe_kernel(nblk_ref, h_ref, rank_ref, gate_ref, x_ref, g2_ref, w1_ref, w3_ref, w2_ref, o_ref, xs_scr, y_scr):
    i, e, j = pl.program_id(0), pl.program_id(1), pl.program_id(2)
    nff = pl.num_programs(2)
    t = h_ref.shape[0]
    r = MOE_ROWS
    nb = nblk_ref[i * N_EXPERTS + e]
    row = lax.broadcasted_iota(jnp.int32, (r, t), 0).astype(F32)

    @pl.when(jnp.logical_and(e == 0, j == 0))
    def _():
        o_ref[...] = jnp.zeros_like(o_ref)

    @pl.when(j == 0)
    def _():
        def gather(s, c):
            r0 = pl.multiple_of(s * r, r)
            onehot = jnp.where(rank_ref[...] == row + (s * r).astype(F32), 1.0, 0.0).astype(BF16)
            xs_scr[pl.ds(r0, r), :] = _dot(onehot, h_ref[...]).astype(BF16)
            y_scr[pl.ds(r0, r), :] = jnp.zeros((r, y_scr.shape[1]), F32)
            return c
        lax.fori_loop(0, nb, gather, 0)

    def ffn(s, c):
        r0 = pl.multiple_of(s * r, r)
        xs = xs_scr[pl.ds(r0, r), :]
        act = (_silu(_dot(xs, w1_ref[...])) * _dot(xs, w3_ref[...])).astype(BF16)
        y_scr[pl.ds(r0, r), :] += _dot(act, w2_ref[...])
        return c
    lax.fori_loop(0, nb, ffn, 0)

    @pl.when(j == nff - 1)
    def _():
        def scatter(s, c):
            r0 = pl.multiple_of(s * r, r)
            hit = rank_ref[...] == row + (s * r).astype(F32)
            wgt = jnp.where(hit, gate_ref[...], 0.0).astype(BF16)
            o_ref[...] += _dot_tn(wgt, y_scr[pl.ds(r0, r), :].astype(BF16))
            return c
        lax.fori_loop(0, nb, scatter, 0)

    @pl.when(jnp.logical_and(e == N_EXPERTS - 1, j == nff - 1))
    def _():
        o_ref[...] = x_ref[...] + g2_ref[...] * o_ref[...]


def moe_ffn(x, sc, sh, g, gate2, router_t, w1, w3, w2, t, tf, rows_per_group, n_valid):
    m, d = x.shape
    nt = m // t
    ne, _, ff = w1.shape
    assert ne == N_EXPERTS
    tri = jnp.asarray(np.triu(np.ones((t, t), np.float32)), BF16)
    h, rank, gate, cnt = pl.pallas_call(
        functools.partial(_route_kernel, n_valid),
        grid=(nt,),
        in_specs=[pl.BlockSpec((t, d), lambda i: (i, 0)),
                  _mod_spec(sc, t, rows_per_group), _mod_spec(sh, t, rows_per_group),
                  pl.BlockSpec((1, d), lambda i: (0, 0)),
                  pl.BlockSpec((d, 128), lambda i: (0, 0)),
                  pl.BlockSpec((t, t), lambda i: (0, 0))],
        out_specs=[pl.BlockSpec((t, d), lambda i: (i, 0)),
                   pl.BlockSpec((None, ne, t), lambda i: (i, 0, 0)),
                   pl.BlockSpec((None, ne, t), lambda i: (i, 0, 0)),
                   pl.BlockSpec((None, ne, 128), lambda i: (i, 0, 0))],
        out_shape=[jax.ShapeDtypeStruct((m, d), BF16),
                   jax.ShapeDtypeStruct((nt, ne, t), F32),
                   jax.ShapeDtypeStruct((nt, ne, t), F32),
                   jax.ShapeDtypeStruct((nt, ne, 128), F32)],
        compiler_params=_cp(("arbitrary",)),
        name="moe_route",
    )(x, sc, sh, g, router_t, tri)
    nblk = ((cnt[:, :, 0].astype(jnp.int32) + (MOE_ROWS - 1)) // MOE_ROWS).reshape(nt * ne)
    rank = rank.reshape(nt, ne, 1, t)
    gate = gate.reshape(nt, ne, 1, t)
    return pl.pallas_call(
        _moe_kernel,
        grid_spec=pltpu.PrefetchScalarGridSpec(
            num_scalar_prefetch=1,
            grid=(nt, ne, ff // tf),
            in_specs=[pl.BlockSpec((t, d), lambda i, e, j, nb: (i, 0)),
                      pl.BlockSpec((None, None, 1, t), lambda i, e, j, nb: (i, e, 0, 0)),
                      pl.BlockSpec((None, None, 1, t), lambda i, e, j, nb: (i, e, 0, 0)),
                      pl.BlockSpec((t, d), lambda i, e, j, nb: (i, 0)),
                      _mod_spec(gate2, t, rows_per_group),
                      pl.BlockSpec((None, d, tf), lambda i, e, j, nb: (e, 0, j)),
                      pl.BlockSpec((None, d, tf), lambda i, e, j, nb: (e, 0, j)),
                      pl.BlockSpec((None, tf, d), lambda i, e, j, nb: (e, j, 0))],
            out_specs=pl.BlockSpec((t, d), lambda i, e, j, nb: (i, 0)),
            scratch_shapes=[pltpu.VMEM((t, d), BF16), pltpu.VMEM((t, d), F32)]),
        out_shape=jax.ShapeDtypeStruct((m, d), F32),
        compiler_params=_cp(("arbitrary", "arbitrary", "arbitrary")),
        name="moe_ffn",
    )(nblk, h, rank, gate, x, gate2, w1, w3, w2)


def _conv_kernel(a_ref, g_ref, w_ref, b_ref, lg_ref, lb_ref, o_ref, cn_ref, ext_scr):
    tc = a_ref.shape[0]
    halo = 32
    off = halo - (CONV_K - 1)

    @pl.when(pl.program_id(1) == 0)
    def _():
        ext_scr[0:halo, :] = jnp.zeros((halo, GW), F32)

    ext_scr[halo:halo + tc, :] = a_ref[...] * jax.nn.sigmoid(g_ref[...])
    rc = 64
    for c in range(tc // rc):
        acc = jnp.zeros((rc, GW), F32)
        for k in range(CONV_K):
            acc = acc + w_ref[k:k + 1, :] * ext_scr[c * rc + k + off:c * rc + k + off + rc, :]
        y = acc + b_ref[...]
        mu = jnp.mean(y, axis=-1, keepdims=True)
        yc = y - mu
        var = jnp.mean(yc * yc, axis=-1, keepdims=True)
        o_ref[c * rc:(c + 1) * rc, :] = _silu(yc * lax.rsqrt(var + EPS) * lg_ref[...] + lb_ref[...]).astype(o_ref.dtype)
    cn_ref[...] = ext_scr[tc + off:tc + halo, :]
    ext_scr[0:halo, :] = ext_scr[tc:tc + halo, :]


def conv_prompt(z, nb, l, conv_w, conv_b, ln_g, ln_b, tc):
    m = z.shape[0]
    nl = l // tc
    row = lambda v: v.reshape(1, GW)
    return pl.pallas_call(
        _conv_kernel,
        grid=(nb, nl),
        in_specs=[pl.BlockSpec((tc, GW), lambda b, i: (b * nl + i, G_CA)),
                  pl.BlockSpec((tc, GW), lambda b, i: (b * nl + i, G_CG)),
                  pl.BlockSpec((CONV_K, GW), lambda b, i: (0, 0)),
                  pl.BlockSpec((1, GW), lambda b, i: (0, 0)),
                  pl.BlockSpec((1, GW), lambda b, i: (0, 0)),
                  pl.BlockSpec((1, GW), lambda b, i: (0, 0))],
        out_specs=[pl.BlockSpec((tc, GW), lambda b, i: (b * nl + i, 0)),
                   pl.BlockSpec((None, CONV_K - 1, GW), lambda b, i: (b, 0, 0))],
        out_shape=[jax.ShapeDtypeStruct((m, GW), BF16),
                   jax.ShapeDtypeStruct((nb, CONV_K - 1, GW), F32)],
        scratch_shapes=[pltpu.VMEM((tc + 32, GW), F32)],
        compiler_params=_cp(("arbitrary", "arbitrary")),
        name="conv_prompt",
    )(z, z, conv_w, row(conv_b), row(ln_g), row(ln_b))


def _fox_prep_kernel(q_ref, k_ref, v_ref, t_ref, gq_ref, gk_ref, fb_ref, hs_ref, tri_ref,
                     qb_ref, kb_ref, vb_ref, kf_ref, vf_ref, lf_ref, fc_ref, carry_scr):
    @pl.when(pl.program_id(1) == 0)
    def _():
        carry_scr[...] = jnp.zeros_like(carry_scr)

    hs = hs_ref[...]
    qn = _head_rms(q_ref[...], hs, gq_ref[...])
    kn = _head_rms(k_ref[...], hs, gk_ref[...])
    v = v_ref[...]
    qb_ref[...] = (qn * (HD ** -0.5)).astype(BF16)
    kb_ref[...] = kn.astype(BF16)
    vb_ref[...] = v.T.astype(BF16)
    kf_ref[...] = kn
    vf_ref[...] = v
    lf = _log_sigmoid(t_ref[...] + fb_ref[...])
    lf_ref[...] = lf
    cum = _dot_sel(tri_ref[...], lf, "b") + carry_scr[...]
    fc_ref[...] = cum
    carry_scr[...] = cum[cum.shape[0] - 1:, :]


def _fox_flash_kernel(q_ref, k_ref, vt_ref, fc_ref, ft_ref, o_ref, ot_scr):
    tq = q_ref.shape[0]
    tk = tq
    qi = pl.program_id(1)
    q = q_ref[...]
    head = _head_ids((1, GW), 1)
    kpos = lax.broadcasted_iota(jnp.int32, (tk, tq), 0)
    qpos = lax.broadcasted_iota(jnp.int32, (tk, tq), 1)
    ftq = ft_ref[...]
    qhs = [jnp.where(head == h, q, jnp.zeros_like(q)) for h in range(NH)]

    def step(jb, carry, diagonal):
        k0 = pl.multiple_of(jb * tk, tk)
        kb = k_ref[pl.ds(k0, tk), :]
        fcb = fc_ref[pl.ds(k0, tk), :]
        out = []
        for h in range(NH):
            m, l, acc = carry[h]
            s = _dot_nt(kb, qhs[h]) + (ftq[h:h + 1, :] - fcb[:, h:h + 1])
            if diagonal:
                s = jnp.where(kpos <= qpos, s, NEG)
            m_new = jnp.maximum(m, jnp.max(s, axis=0, keepdims=True))
            alpha = jnp.exp(m - m_new)
            p = jnp.exp(s - m_new)
            l = alpha * l + jnp.sum(p, axis=0, keepdims=True)
            acc = alpha * acc + _dot(vt_ref[jb, h * HD:(h + 1) * HD, :], p.astype(BF16))
            out.append((m_new, l, acc))
        return tuple(out)

    init = tuple((jnp.full((1, tq), NEG, F32), jnp.zeros((1, tq), F32), jnp.zeros((HD, tq), F32)) for _ in range(NH))
    carry = lax.fori_loop(0, qi, functools.partial(step, diagonal=False), init)
    carry = step(qi, carry, True)
    for h in range(NH):
        _, l, acc = carry[h]
        ot_scr[h * HD:(h + 1) * HD, :] = acc * (1.0 / l)
    o_ref[...] = ot_scr[...].T.astype(o_ref.dtype)


def fox_prompt(z, nb, l, gq, gk, fb, tq):
    m = z.shape[0]
    tp = tq
    nl = l // tp
    hs = jnp.asarray(_np_hsum())
    tri = jnp.asarray(np.tril(np.ones((tp, tp), np.float32)))
    row = lambda v: jnp.tile(v, NH).reshape(1, GW)
    fbp = jnp.zeros((1, TAIL_W), F32).at[0, :NH].set(fb)
    blk = lambda g: pl.BlockSpec((tp, GW), lambda b, i: (b * nl + i, g))
    oblk = lambda w: pl.BlockSpec((tp, w), lambda b, i: (b * nl + i, 0))
    qb, kb, vb, kf, vf, lf, fc = pl.pallas_call(
        _fox_prep_kernel,
        grid=(nb, nl),
        in_specs=[blk(G_FQ), blk(G_FK), blk(G_FV),
                  pl.BlockSpec((tp, TAIL_W), lambda b, i: (b * nl + i, TAIL_BLK)),
                  pl.BlockSpec((1, GW), lambda b, i: (0, 0)),
                  pl.BlockSpec((1, GW), lambda b, i: (0, 0)),
                  pl.BlockSpec((1, TAIL_W), lambda b, i: (0, 0)),
                  pl.BlockSpec((GW, GW), lambda b, i: (0, 0)),
                  pl.BlockSpec((tp, tp), lambda b, i: (0, 0))],
        out_specs=[oblk(GW)] * 2 + [pl.BlockSpec((None, None, GW, tp), lambda b, i: (b, i, 0, 0))]
                  + [oblk(GW)] * 2 + [oblk(TAIL_W)] * 2,
        out_shape=[jax.ShapeDtypeStruct((m, GW), BF16)] * 2 + [jax.ShapeDtypeStruct((nb, nl, GW, tp), BF16)]
                  + [jax.ShapeDtypeStruct((m, GW), F32)] * 2 + [jax.ShapeDtypeStruct((m, TAIL_W), F32)] * 2,
        scratch_shapes=[pltpu.VMEM((1, TAIL_W), F32)],
        compiler_params=_cp(("arbitrary", "arbitrary")),
        name="fox_prep",
    )(z, z, z, z, row(gq), row(gk), fbp, hs, tri)
    nq = l // tq
    ft = jnp.pad(fc.reshape(nb, nq, tq, TAIL_W)[..., :NH].transpose(0, 1, 3, 2), ((0, 0), (0, 0), (0, 8 - NH), (0, 0)))
    o = pl.pallas_call(
        _fox_flash_kernel,
        grid=(nb, nq),
        in_specs=[pl.BlockSpec((tq, GW), lambda b, i: (b * nq + i, 0)),
                  pl.BlockSpec((l, GW), lambda b, i: (b, 0)),
                  pl.BlockSpec((None, nq, GW, tq), lambda b, i: (b, 0, 0, 0)),
                  pl.BlockSpec((l, TAIL_W), lambda b, i: (b, 0)),
                  pl.BlockSpec((None, None, 8, tq), lambda b, i: (b, i, 0, 0))],
        out_specs=pl.BlockSpec((tq, GW), lambda b, i: (b * nq + i, 0)),
        out_shape=jax.ShapeDtypeStruct((m, GW), BF16),
        scratch_shapes=[pltpu.VMEM((GW, tq), F32)],
        compiler_params=_cp(("arbitrary", "arbitrary")),
        name="fox_flash",
    )(qb, kb, vb, fc, ft)
    return o, kf, vf, lf[:, :NH]


GLA_SAFE_LOG = -80.0


def _gla_kernel(q_ref, k_ref, v_ref, t_ref, go_ref, w2_ref, bg_ref, gn_ref, hs_ref, cs_ref, bd_ref,
                o_ref, st_ref, st_scr, x_scr, o_scr):
    tg = q_ref.shape[0]
    c = GLA_CHUNK

    @pl.when(pl.program_id(1) == 0)
    def _():
        st_scr[...] = jnp.zeros_like(st_scr)

    hs = hs_ref[...]
    bd = bd_ref[...]
    glog = _log_sigmoid(_dot(t_ref[...].astype(BF16), w2_ref[...]) + bg_ref[...]) * (1.0 / GLA_TAU)
    q = q_ref[...] * (HD ** -0.5)
    k = k_ref[...]
    v = v_ref[...]
    vb = v.astype(BF16)
    lower = lax.broadcasted_iota(jnp.int32, (tg, tg), 0) >= lax.broadcasted_iota(jnp.int32, (tg, tg), 1)
    bfull = _dot_sel(jnp.where(lower, 1.0, 0.0), glog, "b")
    safe = jnp.min(bfull) >= GLA_SAFE_LOG

    @pl.when(safe)
    def _():
        head = _head_ids((1, GW), 1)
        btot = bfull[tg - 1:tg]
        qe = (q * jnp.exp(bfull)).astype(BF16)
        kinv = (k * jnp.exp(-bfull)).astype(BF16)
        kend = (k * jnp.exp(btot - bfull)).astype(BF16)
        o = _dot_nt(qe, st_scr[...].astype(BF16))
        for h in range(NH):
            s = _dot_nt(jnp.where(head == h, qe, jnp.zeros_like(qe)), kinv)
            o = o + jnp.where(head == h, _dot(jnp.where(lower, s, 0.0).astype(BF16), vb), 0.0)
        o_scr[...] = o
        st_scr[...] = st_scr[...] * jnp.exp(btot) + _dot_tn(vb, kend) * bd

    @pl.when(jnp.logical_not(safe))
    def _():
        hs_b = hs.astype(BF16)
        cs = cs_ref[...]
        bcum = _dot_sel(jnp.where(lower, cs, 0.0), glog, "b")
        blast = _dot_sel(cs, glog, "b")
        qe = (q * jnp.exp(bcum)).astype(BF16)
        ke = (k * jnp.exp(blast - bcum)).astype(BF16)
        ii = lax.broadcasted_iota(jnp.int32, (c, GW), 0)
        for n in range(tg // c):
            r0 = n * c
            bc = bcum[r0:r0 + c]
            qc = q[r0:r0 + c]
            kc = k[r0:r0 + c]
            vc = v[r0:r0 + c]
            for j in range(c):
                ex = jnp.exp(jnp.where(ii >= j, bc - bc[j:j + 1], -jnp.inf))
                x_scr[j * c:(j + 1) * c, :] = (qc * ex * kc[j:j + 1]).astype(BF16)
            att = _dot(x_scr[...], hs_b)
            o = _dot_nt(qe[r0:r0 + c], st_scr[...].astype(BF16))
            for j in range(c):
                o = o + att[j * c:(j + 1) * c] * vc[j:j + 1]
            o_scr[r0:r0 + c, :] = o
            kv = _dot_tn(vb[r0:r0 + c], ke[r0:r0 + c])
            st_scr[...] = st_scr[...] * jnp.exp(blast[r0:r0 + 1]) + kv * bd

    o_ref[...] = (_head_rms(o_scr[...], hs, gn_ref[...]) * _silu(go_ref[...])).astype(o_ref.dtype)
    st_ref[...] = st_scr[...]


def gla_prompt(z, nb, l, w_gate2, b_gate, norm_g, tg):
    m = z.shape[0]
    nl = l // tg
    hs = jnp.asarray(_np_hsum())
    ch = np.arange(tg) // GLA_CHUNK
    cs = jnp.asarray((ch[:, None] == ch[None, :]).astype(np.float32))
    w2p = jnp.zeros((TAIL_W, GW), F32).at[NH:NH + GLA_LR].set(w_gate2).astype(BF16)
    blk = lambda g: pl.BlockSpec((tg, GW), lambda b, i: (b * nl + i, g))
    full = lambda r, c: pl.BlockSpec((r, c), lambda b, i: (0, 0))
    return pl.pallas_call(
        _gla_kernel,
        grid=(nb, nl),
        in_specs=[blk(G_GQ), blk(G_GK), blk(G_GV),
                  pl.BlockSpec((tg, TAIL_W), lambda b, i: (b * nl + i, TAIL_BLK)),
                  blk(G_GO), full(TAIL_W, GW), full(1, GW), full(1, GW), full(GW, GW), full(tg, tg), full(GW, GW)],
        out_specs=[pl.BlockSpec((tg, GW), lambda b, i: (b * nl + i, 0)),
                   pl.BlockSpec((None, GW, GW), lambda b, i: (b, 0, 0))],
        out_shape=[jax.ShapeDtypeStruct((m, GW), BF16), jax.ShapeDtypeStruct((nb, GW, GW), F32)],
        scratch_shapes=[pltpu.VMEM((GW, GW), F32), pltpu.VMEM((GLA_CHUNK * GLA_CHUNK, GW), BF16),
                        pltpu.VMEM((tg, GW), F32)],
        compiler_params=_cp(("arbitrary", "arbitrary")),
        name="gla_prompt",
    )(z, z, z, z, z, w2p, b_gate.reshape(1, GW), norm_g.reshape(1, GW), hs, cs, hs)


def _ret_lg_row():
    lg = np.log(1.0 - np.exp2(-5.0 - np.arange(NH, dtype=np.float32))).astype(np.float32)
    return np.repeat(lg, HD).reshape(1, GW)


def _ret_decay_mask(c):
    lg = _ret_lg_row()[0, ::HD]
    i = np.arange(c, dtype=np.float32)
    rel = i[:, None] - i[None, :]
    return np.where(rel >= 0, np.exp(np.maximum(rel, 0.0)[None] * lg[:, None, None]), 0.0).astype(np.float32)


def _rope(x, cos, sin_signed):
    first = (lax.broadcasted_iota(jnp.int32, (1, GW), 1) & (HD - 1)) < (HD // 2)
    swapped = jnp.where(first, pltpu.roll(x, GW - HD // 2, axis=1), pltpu.roll(x, HD // 2, axis=1))
    return x * cos + swapped * sin_signed


def _ret_kernel(q_ref, k_ref, v_ref, g_ref, cos_ref, sin_ref, lg_ref, gn_ref, hs_ref, dm_ref, o_ref, st_ref, st_scr):
    tr = dm_ref.shape[1]

    @pl.when(pl.program_id(1) == 0)
    def _():
        st_scr[...] = jnp.zeros_like(st_scr)

    hs = hs_ref[...]
    lg = lg_ref[...]
    head = _head_ids((1, GW), 1)
    ri = lax.broadcasted_iota(jnp.int32, (tr, 1), 0).astype(F32)
    dq = jnp.exp((ri + 1.0) * lg)
    dk = jnp.exp((tr - 1.0 - ri) * lg)
    ds = jnp.exp(tr * lg)
    for n in range(q_ref.shape[0] // tr):
        rows = slice(n * tr, (n + 1) * tr)
        cos = cos_ref[rows, :]
        sin = sin_ref[rows, :]
        q = _rope(q_ref[rows, :], cos, sin)
        k = _rope(k_ref[rows, :], cos, sin) * (HD ** -0.5)
        qb = q.astype(BF16)
        kb = k.astype(BF16)
        vb = v_ref[rows, :].astype(BF16)
        o = _dot_nt((q * dq).astype(BF16), st_scr[...].astype(BF16))
        for h in range(NH):
            att = _dot_nt(jnp.where(head == h, qb, jnp.zeros_like(qb)), kb) * dm_ref[h]
            o = o + jnp.where(head == h, _dot(att.astype(BF16), vb), 0.0)
        st_scr[...] = st_scr[...] * ds + _dot_tn(vb, (k * dk).astype(BF16)) * hs
        o_ref[rows, :] = (_head_rms(o, hs, gn_ref[...]) * _silu(g_ref[rows, :])).astype(o_ref.dtype)
    st_ref[...] = st_scr[...]


def _rope_tables(pos):
    half = HD // 2
    inv = 10000.0 ** (-jnp.arange(half, dtype=F32) / half)
    ang = pos[:, None] * inv[None, :]
    cos = jnp.cos(ang)
    sin = jnp.sin(ang)
    cos_t = jnp.tile(jnp.concatenate([cos, cos], axis=1), (1, NH))
    sin_t = jnp.tile(jnp.concatenate([-sin, sin], axis=1), (1, NH))
    return cos_t, sin_t


def ret_prompt(z, nb, l, norm_g, tt):
    m = z.shape[0]
    nl = l // tt
    tr = RET_CHUNK
    hs = jnp.asarray(_np_hsum())
    cos_t, sin_t = _rope_tables(jnp.arange(l, dtype=F32))
    blk = lambda g: pl.BlockSpec((tt, GW), lambda b, i: (b * nl + i, g))
    full = lambda r, c: pl.BlockSpec((r, c), lambda b, i: (0, 0))
    tab = pl.BlockSpec((tt, GW), lambda b, i: (i, 0))
    return pl.pallas_call(
        _ret_kernel,
        grid=(nb, nl),
        in_specs=[blk(G_RQ), blk(G_RK), blk(G_RV), blk(G_RG), tab, tab, full(1, GW), full(1, GW), full(GW, GW),
                  pl.BlockSpec((NH, tr, tr), lambda b, i: (0, 0, 0))],
        out_specs=[pl.BlockSpec((tt, GW), lambda b, i: (b * nl + i, 0)),
                   pl.BlockSpec((None, GW, GW), lambda b, i: (b, 0, 0))],
        out_shape=[jax.ShapeDtypeStruct((m, GW), BF16), jax.ShapeDtypeStruct((nb, GW, GW), F32)],
        scratch_shapes=[pltpu.VMEM((GW, GW), F32)],
        compiler_params=_cp(("arbitrary", "arbitrary")),
        name="ret_prompt",
    )(z, z, z, z, cos_t, sin_t, jnp.asarray(_ret_lg_row()), norm_g.reshape(1, GW), hs, jnp.asarray(_ret_decay_mask(tr)))


def _state_from_blockdiag(st):
    nb = st.shape[0]
    s5 = st.reshape(nb, NH, HD, NH, HD)
    diag = jnp.stack([s5[:, h, :, h, :] for h in range(NH)], axis=1)
    return diag.transpose(0, 1, 3, 2)


def _sample_rows_kernel(z_ref, buf_ref, cw_ref, cb_ref, lg_ref, lb_ref, gq_ref, gk_ref, fb_ref, w2_ref, bg_ref,
                        cos_ref, sin_ref, rlg_ref, hs_ref,
                        oa_ref, cn_ref, fq_ref, fk_ref, lf_ref, gq_o, ga_o, rq_o, rk_o, ra_o):
    grp = lambda g: z_ref[:, g * GW:(g + 1) * GW]
    tail = z_ref[:, N_MAIN:N_MAIN + TAIL_W]
    hs = hs_ref[...]
    u = grp(G_CA) * jax.nn.sigmoid(grp(G_CG))
    y = cw_ref[CONV_K - 1:CONV_K, :] * u + cb_ref[...]
    for k in range(CONV_K - 1):
        y = y + cw_ref[k:k + 1, :] * buf_ref[k]
    mu = jnp.mean(y, axis=-1, keepdims=True)
    yc = y - mu
    var = jnp.mean(yc * yc, axis=-1, keepdims=True)
    oa_ref[...] = _silu(yc * lax.rsqrt(var + EPS) * lg_ref[...] + lb_ref[...]).astype(oa_ref.dtype)
    for k in range(CONV_K - 2):
        cn_ref[k] = buf_ref[k + 1]
    cn_ref[CONV_K - 2] = u
    fq_ref[...] = _head_rms(grp(G_FQ), hs, gq_ref[...]) * (HD ** -0.5)
    fk_ref[...] = _head_rms(grp(G_FK), hs, gk_ref[...])
    lf_ref[...] = _log_sigmoid(tail + fb_ref[...])
    glog = _log_sigmoid(_wdot(tail, w2_ref[...]) + bg_ref[...]) * (1.0 / GLA_TAU)
    gq_o[...] = grp(G_GQ) * (HD ** -0.5)
    ga_o[...] = jnp.exp(glog)
    rq_o[...] = _rope(grp(G_RQ), cos_ref[...], sin_ref[...])
    rk_o[...] = _rope(grp(G_RK), cos_ref[...], sin_ref[...]) * (HD ** -0.5)
    ra_o[...] = jnp.broadcast_to(jnp.exp(rlg_ref[...]), ra_o.shape)


def _rec_step_kernel(q_ref, k_ref, a_ref, v_ref, s_ref, o_ref, sn_ref):
    for b in range(q_ref.shape[0]):
        for h in range(NH):
            q = q_ref[b, h]
            k = k_ref[b, h]
            a = a_ref[b, h]
            v = v_ref[b, h]
            s = s_ref[b, h]
            qk = jnp.sum(q * k, axis=0, keepdims=True)
            o_ref[b, h] = qk * v + jnp.sum((q * a) * s, axis=0, keepdims=True)
            sn_ref[b, h] = a * s + k * v


REC_SEQS = 8


def rec_step(q, k, a, v, state, li):
    nb = q.shape[0]
    bt = math.gcd(nb, REC_SEQS)
    col = lambda x: x.reshape(nb, NH, HD, 1)
    cspec = pl.BlockSpec((bt, NH, HD, 1), lambda b: (b, 0, 0, 0))
    rspec = pl.BlockSpec((bt, NH, 1, HD), lambda b: (b, 0, 0, 0))
    o, sn = pl.pallas_call(
        _rec_step_kernel,
        grid=(nb // bt,),
        in_specs=[cspec, cspec, cspec, rspec,
                  pl.BlockSpec((None, bt, NH, HD, HD), lambda b: (li, b, 0, 0, 0))],
        out_specs=[rspec, pl.BlockSpec((bt, NH, HD, HD), lambda b: (b, 0, 0, 0))],
        out_shape=[jax.ShapeDtypeStruct((nb, NH, 1, HD), F32), jax.ShapeDtypeStruct((nb, NH, HD, HD), F32)],
        compiler_params=_cp(("arbitrary",)),
        name="rec_step",
    )(col(q), col(k), col(a), v.reshape(nb, NH, 1, HD), state)
    return o.reshape(nb, GW), sn


FOX_PAGES = 16


def _fox_sample_kernel(pt_ref, q_ref, kn_ref, vn_ref, lfn_ref, ts_ref, pre_ref, *rest):
    g_n = FOX_PAGES
    k_refs = rest[:g_n]
    v_refs = rest[g_n:2 * g_n]
    f_refs = rest[2 * g_n:3 * g_n]
    o_ref, m_scr, l_scr, acc_scr, r_scr, lf_scr, s_scr = rest[3 * g_n:]
    c = pl.program_id(1)
    pg = k_refs[0].shape[1]

    @pl.when(c == 0)
    def _():
        m_scr[...] = jnp.full(m_scr.shape, NEG, F32)
        l_scr[...] = jnp.zeros_like(l_scr)
        acc_scr[...] = jnp.zeros_like(acc_scr)
        r_scr[...] = jnp.zeros_like(r_scr)

    qb = jnp.broadcast_to(q_ref[...], (GW, pg))
    for g in range(g_n):
        prod = k_refs[g][...] * qb
        for h in range(NH):
            lf_scr[h * g_n + g:h * g_n + g + 1, :] = f_refs[g][h:h + 1, :]
            s_scr[h * g_n + g:h * g_n + g + 1, :] = jnp.sum(prod[h * HD:(h + 1) * HD], axis=0, keepdims=True)
    both = _dot_sel(lf_scr[...], ts_ref[...], "a")
    suf = both[:, :pg]
    tot = both[:, pg:]
    pre = _dot_sel(pre_ref[...], tot, "b")
    for h in range(NH):
        r0, r1 = h * g_n, (h + 1) * g_n
        base = lfn_ref[h:h + 1, :] + r_scr[h:h + 1, :]
        s_h = s_scr[r0:r1, :] + (base + pre[r0:r1] + suf[r0:r1])
        m_old = m_scr[h:h + 1, :]
        m_new = jnp.maximum(m_old, jnp.max(jnp.max(s_h, axis=0, keepdims=True), axis=1, keepdims=True))
        alpha = jnp.exp(m_old - m_new)
        p = jnp.exp(s_h - m_new)
        m_scr[h:h + 1, :] = m_new
        l_scr[h:h + 1, :] = l_scr[h:h + 1, :] * alpha + jnp.sum(p, axis=0, keepdims=True)
        r_scr[h:h + 1, :] = r_scr[h:h + 1, :] + jnp.sum(tot[r0:r1], axis=0, keepdims=True)
        acc = acc_scr[h * HD:(h + 1) * HD, :] * alpha
        for g in range(g_n):
            acc = acc + p[g:g + 1, :] * v_refs[g][h * HD:(h + 1) * HD, :]
        acc_scr[h * HD:(h + 1) * HD, :] = acc

    @pl.when(c == pl.num_programs(1) - 1)
    def _():
        prod = q_ref[...] * kn_ref[...]
        for h in range(NH):
            s_self = jnp.sum(prod[h * HD:(h + 1) * HD], axis=0, keepdims=True)
            m_h = m_scr[h:h + 1, 0:1]
            m_fin = jnp.maximum(m_h, s_self)
            a_h = jnp.exp(m_h - m_fin)
            p_self = jnp.exp(s_self - m_fin)
            l_tot = jnp.sum(l_scr[h:h + 1, :], axis=1, keepdims=True) * a_h + p_self
            num = jnp.sum(acc_scr[h * HD:(h + 1) * HD, :], axis=1, keepdims=True) * a_h \
                + p_self * vn_ref[h * HD:(h + 1) * HD, :]
            o_ref[h * HD:(h + 1) * HD, :] = num / l_tot


def fox_sample(q, k_new, v_new, lf_new, cache_kt, cache_vt, cache_lf, page_table, li):
    nb, n_pages = page_table.shape
    pg = cache_kt.shape[-1]
    g_n = FOX_PAGES
    nc = n_pages // g_n
    col = lambda x: x.reshape(nb, GW, 1)
    lfn = jnp.broadcast_to(lf_new[:, :, None], (nb, NH, pg))
    t = np.arange(pg)
    ts = jnp.asarray(np.concatenate([(t[:, None] > t[None, :]).astype(np.float32), np.ones((pg, pg), np.float32)], axis=1))
    r = np.arange(NH * g_n)
    pre = jnp.asarray(((r[:, None] // g_n == r[None, :] // g_n) & (r[None, :] < r[:, None])).astype(np.float32))

    def page(g):
        return lambda b, c, pt: (pt[b * n_pages + (n_pages - 1 - (c * g_n + g))], li, 0, 0)

    cspec = pl.BlockSpec((None, GW, 1), lambda b, c, pt: (b, 0, 0))
    in_specs = [cspec, cspec, cspec,
                pl.BlockSpec((None, NH, pg), lambda b, c, pt: (b, 0, 0)),
                pl.BlockSpec((pg, 2 * pg), lambda b, c, pt: (0, 0)),
                pl.BlockSpec((NH * g_n, NH * g_n), lambda b, c, pt: (0, 0))]
    in_specs += [pl.BlockSpec((None, None, GW, pg), page(g)) for g in range(g_n)]
    in_specs += [pl.BlockSpec((None, None, GW, pg), page(g)) for g in range(g_n)]
    in_specs += [pl.BlockSpec((None, None, NH, pg), page(g)) for g in range(g_n)]
    o = pl.pallas_call(
        _fox_sample_kernel,
        grid_spec=pltpu.PrefetchScalarGridSpec(
            num_scalar_prefetch=1,
            grid=(nb, nc),
            in_specs=in_specs,
            out_specs=pl.BlockSpec((None, GW, 1), lambda b, c, pt: (b, 0, 0)),
            scratch_shapes=[pltpu.VMEM((NH, pg), F32), pltpu.VMEM((NH, pg), F32),
                            pltpu.VMEM((GW, pg), F32), pltpu.VMEM((NH, pg), F32),
                            pltpu.VMEM((NH * g_n, pg), F32), pltpu.VMEM((NH * g_n, pg), F32)]),
        out_shape=jax.ShapeDtypeStruct((nb, GW, 1), F32),
        compiler_params=_cp(("arbitrary", "arbitrary")),
        name="fox_sample",
    )(page_table.reshape(-1), col(q), col(k_new), col(v_new), lfn, ts, pre,
      *([cache_kt] * g_n), *([cache_vt] * g_n), *([cache_lf] * g_n))
    return o.reshape(nb, GW)


def _sample_out_kernel(oa_ref, ob_ref, oc_ref, od_ref, go_ref, rg_ref, gng_ref, rng_ref, hs_ref, w_ref, x_ref, g1_ref, o_ref):
    hs = hs_ref[...]
    oc = _head_rms(oc_ref[...], hs, gng_ref[...]) * _silu(go_ref[...])
    od = _head_rms(od_ref[...], hs, rng_ref[...]) * _silu(rg_ref[...])
    cat = jnp.concatenate([oa_ref[...], ob_ref[...], oc, od], axis=1)
    o_ref[...] = x_ref[...] + g1_ref[...] * _wdot(cat, w_ref[...])


def _prep_w_in(w_in, li):
    wt = w_in.transpose(2, 0, 1)[:, li, :]
    o_ff = 5 * GW
    o_lr = o_ff + NH + 3 * GW
    return jnp.concatenate([wt[:o_ff], wt[o_ff + NH:o_lr], wt[o_lr + GLA_LR:],
                            wt[o_ff:o_ff + NH], wt[o_lr:o_lr + GLA_LR],
                            jnp.zeros((TAIL_W - NH - GLA_LR, wt.shape[1]), wt.dtype)], axis=0)


def _ffn_apply(li, x, sc, sh, g, gate, fw, tm, rows_per_group, n_valid):
    if li % 2 == 0:
        w1, w3, w2 = fw
        return ffn_dense(x, sc, sh, g, gate, w1, w3, w2, tm, w1.shape[1] // 2, rows_per_group)
    router_t, w1, w3, w2 = fw
    return moe_ffn(x, sc, sh, g, gate, router_t, w1, w3, w2, tm, w1.shape[2] // 2, rows_per_group, n_valid)


def kernel(x_prompt, x_sample, c_prompt, c_sample, state_conv, cache_k, cache_v, cache_logf, state_gla, state_ret, page_table, w_in, w_out, conv_w, conv_b, conv_ln_g, conv_ln_b, fox_qn_g, fox_kn_g, fox_fb, gla_w_gate2, gla_b_gate, gla_norm_g, ret_norm_g, norm1_g, norm2_g, w_ada, b_ada, ffn_w1, ffn_w3, ffn_w2, moe_router, moe_w1, moe_w3, moe_w2):
    nbp, l, d = x_prompt.shape
    nbs = x_sample.shape[0]
    depth = w_in.shape[0]
    n_pages, pg = page_table.shape[1], cache_k.shape[2]
    p_len = n_pages * pg
    mp = nbp * l
    ms = 128

    mod = ada_mod(jnp.concatenate([c_prompt, c_sample], axis=0), w_ada, b_ada)
    mod = mod.reshape(depth, nbp + nbs, 6, d)
    cache_kt = cache_k.transpose(0, 1, 3, 4, 2).reshape(cache_k.shape[0], depth, GW, pg)
    cache_vt = cache_v.transpose(0, 1, 3, 4, 2).reshape(cache_v.shape[0], depth, GW, pg)
    cache_lf = cache_logf.transpose(0, 1, 3, 2)
    conv_state = state_conv.transpose(0, 2, 1, 3)
    cos_s, sin_s = _rope_tables(jnp.full((1,), p_len, F32))
    hs = jnp.asarray(_np_hsum())
    rlg = jnp.asarray(_ret_lg_row())

    xp = x_prompt.reshape(mp, d)
    xs = jnp.zeros((ms, d), F32).at[:nbs].set(x_sample.reshape(nbs, d))
    outs_p, outs_s = [], []
    for li in range(depth):
        mp_l = [mod[li, :nbp, i].reshape(nbp, 1, d) for i in range(6)]
        ms_l = [jnp.zeros((1, ms, d), F32).at[0, :nbs].set(mod[li, nbp:, i]) for i in range(6)]
        n1 = norm1_g[li].reshape(1, d)
        n2 = norm2_g[li].reshape(1, d)
        w_in_f = _prep_w_in(w_in, li)
        w_in_b = w_in_f.astype(BF16)
        w_out_f = w_out[li]
        w_out_b = w_out_f.astype(BF16)
        if li % 2 == 0:
            fw_s = (ffn_w1[li // 2], ffn_w3[li // 2], ffn_w2[li // 2])
            fw = tuple(w.astype(BF16) for w in fw_s)
        else:
            fw = (jnp.pad(moe_router[li // 2], ((0, 0), (0, 128 - N_EXPERTS))), moe_w1[li // 2].astype(BF16), moe_w3[li // 2].astype(BF16),
                  moe_w2[li // 2].astype(BF16))
            fw_s = fw

        sh1, sc1, g1, sh2, sc2, g2 = mp_l
        z = in_proj(xp, sc1, sh1, n1, w_in_b, 512, l)
        out_a, conv_new = conv_prompt(z, nbp, l, conv_w[li], conv_b[li], conv_ln_g[li], conv_ln_b[li], 512)
        out_b, k_f, v_f, lf = fox_prompt(z, nbp, l, fox_qn_g[li], fox_kn_g[li], fox_fb[li], 512)
        out_c, gla_st = gla_prompt(z, nbp, l, gla_w_gate2[li], gla_b_gate[li], gla_norm_g[li], 256)
        out_d, ret_st = ret_prompt(z, nbp, l, ret_norm_g[li], 512)
        xp = out_proj((out_a, out_b, out_c, out_d), w_out_b, xp, g1, 512, l)
        xp = _ffn_apply(li, xp, sc2, sh2, n2, g2, fw, 512 if li % 2 == 0 else 1024, l, mp)
        outs_p.append((conv_new, k_f.reshape(nbp, l, NH, HD), v_f.reshape(nbp, l, NH, HD), lf.reshape(nbp, l, NH),
                       _state_from_blockdiag(gla_st), _state_from_blockdiag(ret_st)))

        sh1, sc1, g1, sh2, sc2, g2 = ms_l
        zs = in_proj(xs, sc1, sh1, n1, w_in_f, ms, ms)[:nbs]
        row = lambda v_: v_.reshape(1, GW)
        w2p = jnp.zeros((TAIL_W, GW), F32).at[NH:NH + GLA_LR].set(gla_w_gate2[li])
        fbp = jnp.zeros((1, TAIL_W), F32).at[0, :NH].set(fox_fb[li])
        r32 = jax.ShapeDtypeStruct((nbs, GW), F32)
        (oa, cn, fq, fk, lfs, gq, ga, rq, rk, ra) = pl.pallas_call(
            _sample_rows_kernel,
            out_shape=[r32, jax.ShapeDtypeStruct((CONV_K - 1, nbs, GW), F32), r32, r32,
                       jax.ShapeDtypeStruct((nbs, TAIL_W), F32), r32, r32, r32, r32, r32],
            compiler_params=pltpu.CompilerParams(vmem_limit_bytes=VMEM_LIMIT),
            name="sample_rows",
        )(zs, conv_state[li], conv_w[li], row(conv_b[li]), row(conv_ln_g[li]), row(conv_ln_b[li]),
          row(jnp.tile(fox_qn_g[li], NH)), row(jnp.tile(fox_kn_g[li], NH)), fbp, w2p, row(gla_b_gate[li]),
          cos_s, sin_s, rlg, hs)
        grp = lambda g_: zs[:, g_ * GW:(g_ + 1) * GW]
        f_v, g_k, g_v, r_v = grp(G_FV), grp(G_GK), grp(G_GV), grp(G_RV)
        ob = fox_sample(fq, fk, f_v, lfs[:, :NH], cache_kt, cache_vt, cache_lf, page_table, li)
        oc, gla_new = rec_step(gq, g_k, ga, g_v, state_gla, li)
        od, ret_new = rec_step(rq, rk, ra, r_v, state_ret, li)
        pad = lambda v_: jnp.zeros((ms, v_.shape[1]), F32).at[:nbs].set(v_)
        xs = pl.pallas_call(
            _sample_out_kernel,
            out_shape=jax.ShapeDtypeStruct((ms, d), F32),
            compiler_params=pltpu.CompilerParams(vmem_limit_bytes=VMEM_LIMIT),
            name="sample_out",
        )(pad(oa), pad(ob), pad(oc), pad(od), pad(grp(G_GO)), pad(grp(G_RG)), row(gla_norm_g[li]), row(ret_norm_g[li]),
          hs, w_out_f, xs, g1[0])
        xs = _ffn_apply(li, xs, sc2, sh2, n2, g2, fw_s, ms, ms, nbs)
        outs_s.append((cn.transpose(1, 0, 2), fk.reshape(nbs, 1, NH, HD), f_v.reshape(nbs, 1, NH, HD),
                       lfs[:, :NH].reshape(nbs, 1, NH), gla_new, ret_new))

    st = lambda lst, i, ax: jnp.stack([s[i] for s in lst], axis=ax)
    return (xp.reshape(nbp, l, d), xs[:nbs].reshape(nbs, 1, d),
            st(outs_p, 0, 0), st(outs_s, 0, 0),
            st(outs_p, 1, 1), st(outs_s, 1, 1),
            st(outs_p, 2, 1), st(outs_s, 2, 1),
            st(outs_p, 3, 1), st(outs_s, 3, 1),
            st(outs_p, 4, 0), st(outs_s, 4, 0),
            st(outs_p, 5, 0), st(outs_s, 5, 0))
```

```python
import functools
import math

import numpy as np
import jax
import jax.numpy as jnp
from jax import lax
from jax.experimental import pallas as pl
from jax.experimental.pallas import tpu as pltpu

F32 = jnp.float32
BF16 = jnp.bfloat16
HI = lax.Precision.HIGHEST

D_MODEL = 1024
NH = 4
HD = 64
GW = NH * HD
CONV_K = 31
GLA_LR = 16
GLA_TAU = 16.0
GLA_CHUNK = 16
RET_CHUNK = 128
N_EXPERTS = 8
EPS = 1e-6
N_MAIN = 13 * GW
TAIL_W = 128
IN_PAD = N_MAIN + TAIL_W
TAIL_BLK = N_MAIN // TAIL_W
(G_CA, G_CG, G_FQ, G_FK, G_FV, G_GQ, G_GK, G_GV, G_GO, G_RQ, G_RK, G_RV, G_RG) = range(13)
VMEM_LIMIT = 56 * 1024 * 1024
MOE_ROWS = 128
NEG = -1e30


def _cp(sem, vmem=VMEM_LIMIT):
    return pltpu.CompilerParams(dimension_semantics=sem, vmem_limit_bytes=vmem)


def _silu(x):
    return x * jax.nn.sigmoid(x)


def _log_sigmoid(x):
    return jnp.minimum(x, 0.0) - jnp.log1p(jnp.exp(-jnp.abs(x)))


def _dot(a, b):
    return jnp.dot(a, b, preferred_element_type=F32)


def _split3(x):
    hi = x.astype(BF16)
    r1 = x - hi.astype(F32)
    mid = r1.astype(BF16)
    lo = (r1 - mid.astype(F32)).astype(BF16)
    return hi, mid, lo


def _dot_sel(a, b, data):
    if data == "a":
        sel = b.astype(BF16)
        return sum(_dot(p, sel) for p in _split3(a))
    sel = a.astype(BF16)
    return sum(_dot(sel, p) for p in _split3(b))


def _head_ids(shape, dim):
    return lax.shift_right_logical(lax.broadcasted_iota(jnp.int32, shape, dim), int(math.log2(HD)))


def _dot_nt(a, b, **kw):
    return lax.dot_general(a, b, (((1,), (1,)), ((), ())), preferred_element_type=F32, **kw)


def _dot_tn(a, b, **kw):
    return lax.dot_general(a, b, (((0,), (0,)), ((), ())), preferred_element_type=F32, **kw)


def _wdot(a, w, nt=False):
    if w.dtype == BF16:
        a, kw = a.astype(BF16), {}
    else:
        kw = {"precision": HI}
    return _dot_nt(a, w, **kw) if nt else jnp.dot(a, w, preferred_element_type=F32, **kw)


def _modnorm(x, g, sc, sh):
    y = x * lax.rsqrt(jnp.mean(x * x, axis=-1, keepdims=True) + EPS)
    return (y * g) * (1.0 + sc) + sh


def _head_rms(x, hsum, g):
    ms = _dot_sel(x * x, hsum, "a") * (1.0 / HD)
    return x * lax.rsqrt(ms + EPS) * g


def _np_hsum():
    i = np.arange(GW) // HD
    return (i[:, None] == i[None, :]).astype(np.float32)


def _ada_kernel(c_ref, w_ref, b_ref, o_ref):
    c = c_ref[...]
    o_ref[...] = _wdot(_silu(c), w_ref[...]) + b_ref[...]


def ada_mod(c_all, w_ada, b_ada):
    depth, d, n = w_ada.shape
    r = c_all.shape[0]
    tn = 1536
    return pl.pallas_call(
        _ada_kernel,
        grid=(depth, n // tn),
        in_specs=[pl.BlockSpec((r, d), lambda l, j: (0, 0)),
                  pl.BlockSpec((None, d, tn), lambda l, j: (l, 0, j)),
                  pl.BlockSpec((None, 1, tn), lambda l, j: (l, 0, j))],
        out_specs=pl.BlockSpec((None, r, tn), lambda l, j: (l, 0, j)),
        out_shape=jax.ShapeDtypeStruct((depth, r, n), F32),
        compiler_params=_cp(("arbitrary", "arbitrary")),
        name="ada_mod",
    )(c_all, w_ada, b_ada.reshape(depth, 1, n))


def _in_proj_kernel(x_ref, sc_ref, sh_ref, g_ref, w_ref, z_ref):
    h = _modnorm(x_ref[...], g_ref[...], sc_ref[...], sh_ref[...])
    z_ref[...] = _wdot(h, w_ref[...], nt=True)


def _mod_spec(mod, tm, rows_per_group):
    r = mod.shape[1]
    tiles = max(rows_per_group // tm, 1)
    return pl.BlockSpec((None, r, mod.shape[2]), lambda i, *_: (i // tiles, 0, 0))


def in_proj(x, sc, sh, g, w, tm, rows_per_group):
    m, d = x.shape
    n = w.shape[0]
    return pl.pallas_call(
        _in_proj_kernel,
        grid=(m // tm,),
        in_specs=[pl.BlockSpec((tm, d), lambda i: (i, 0)),
                  _mod_spec(sc, tm, rows_per_group), _mod_spec(sh, tm, rows_per_group),
                  pl.BlockSpec((1, d), lambda i: (0, 0)),
                  pl.BlockSpec((n, d), lambda i: (0, 0))],
        out_specs=pl.BlockSpec((tm, n), lambda i: (i, 0)),
        out_shape=jax.ShapeDtypeStruct((m, n), F32),
        compiler_params=_cp(("arbitrary",)),
        name="in_proj",
    )(x, sc, sh, g, w)


def _out_proj_kernel(a_ref, b_ref, c_ref, d_ref, w_ref, x_ref, g1_ref, o_ref):
    cat = jnp.concatenate([a_ref[...], b_ref[...], c_ref[...], d_ref[...]], axis=1)
    o_ref[...] = x_ref[...] + g1_ref[...] * _dot(cat, w_ref[...])


def out_proj(parts, w, x, g1, tm, rows_per_group):
    m, d = x.shape
    return pl.pallas_call(
        _out_proj_kernel,
        grid=(m // tm,),
        in_specs=[pl.BlockSpec((tm, GW), lambda i: (i, 0))] * 4 + [
            pl.BlockSpec((4 * GW, d), lambda i: (0, 0)),
            pl.BlockSpec((tm, d), lambda i: (i, 0)),
            _mod_spec(g1, tm, rows_per_group)],
        out_specs=pl.BlockSpec((tm, d), lambda i: (i, 0)),
        out_shape=jax.ShapeDtypeStruct((m, d), F32),
        compiler_params=_cp(("arbitrary",)),
        name="out_proj",
    )(*parts, w, x, g1)


def _ffn_kernel(x_ref, sc_ref, sh_ref, g_ref, gate_ref, w1_ref, w3_ref, w2_ref, o_ref, h_scr, acc_scr):
    j = pl.program_id(1)

    @pl.when(j == 0)
    def _():
        h_scr[...] = _modnorm(x_ref[...], g_ref[...], sc_ref[...], sh_ref[...]).astype(h_scr.dtype)
        acc_scr[...] = jnp.zeros_like(acc_scr)

    h = h_scr[...]
    acc_scr[...] += _wdot(_silu(_wdot(h, w1_ref[...])) * _wdot(h, w3_ref[...]), w2_ref[...])

    @pl.when(j == pl.num_programs(1) - 1)
    def _():
        o_ref[...] = x_ref[...] + gate_ref[...] * acc_scr[...]


def ffn_dense(x, sc, sh, g, gate, w1, w3, w2, tm, tf, rows_per_group):
    m, d = x.shape
    ff = w1.shape[1]
    return pl.pallas_call(
        _ffn_kernel,
        grid=(m // tm, ff // tf),
        in_specs=[pl.BlockSpec((tm, d), lambda i, j: (i, 0)),
                  _mod_spec(sc, tm, rows_per_group), _mod_spec(sh, tm, rows_per_group),
                  pl.BlockSpec((1, d), lambda i, j: (0, 0)),
                  _mod_spec(gate, tm, rows_per_group),
                  pl.BlockSpec((d, tf), lambda i, j: (0, j)),
                  pl.BlockSpec((d, tf), lambda i, j: (0, j)),
                  pl.BlockSpec((tf, d), lambda i, j: (j, 0))],
        out_specs=pl.BlockSpec((tm, d), lambda i, j: (i, 0)),
        out_shape=jax.ShapeDtypeStruct((m, d), F32),
        scratch_shapes=[pltpu.VMEM((tm, d), w1.dtype), pltpu.VMEM((tm, d), F32)],
        compiler_params=_cp(("arbitrary", "arbitrary")),
        name="ffn_dense",
    )(x, sc, sh, g, gate, w1, w3, w2)


def _route_kernel(n_valid, x_ref, sc_ref, sh_ref, g_ref, rt_ref, tri_ref, h_ref, rank_ref, gate_ref, cnt_ref):
    t = x_ref.shape[0]
    h = _modnorm(x_ref[...], g_ref[...], sc_ref[...], sh_ref[...])
    h_ref[...] = h.astype(BF16)
    logits = jnp.dot(h, rt_ref[...], preferred_element_type=F32, precision=HI).T[:N_EXPERTS]
    e_iota = lax.broadcasted_iota(jnp.int32, logits.shape, 0).astype(F32)
    m1 = jnp.max(logits, axis=0, keepdims=True)
    i1 = jnp.min(jnp.where(logits == m1, e_iota, float(N_EXPERTS)), axis=0, keepdims=True)
    sel1 = e_iota == i1
    rest = jnp.where(sel1, -jnp.inf, logits)
    m2 = jnp.max(rest, axis=0, keepdims=True)
    i2 = jnp.min(jnp.where(rest == m2, e_iota, float(N_EXPERTS)), axis=0, keepdims=True)
    sel2 = e_iota == i2
    e2 = jnp.exp(m2 - m1)
    den = 1.0 + e2
    gate = jnp.where(sel1, 1.0 / den, 0.0) + jnp.where(sel2, e2 / den, 0.0)
    tok = pl.program_id(0) * t + lax.broadcasted_iota(jnp.int32, logits.shape, 1)
    sel = jnp.logical_and(jnp.logical_or(sel1, sel2), tok < n_valid)
    incl = _dot(jnp.where(sel, 1.0, 0.0).astype(BF16), tri_ref[...])
    rank_ref[...] = jnp.where(sel, incl - 1.0, -1.0)
    gate_ref[...] = gate
    cnt_ref[...] = jnp.broadcast_to(incl[:, t - 1:t], cnt_ref.shape)


def _moe_kernel(nblk_ref, h_ref, rank_ref, gate_ref, x_ref, g2_ref, w1_ref, w3_ref, w2_ref, o_ref, xs_scr, y_scr):
    i, e, j = pl.program_id(0), pl.program_id(1), pl.program_id(2)
    nff = pl.num_programs(2)
    t = h_ref.shape[0]
    r = MOE_ROWS
    nb = nblk_ref[i * N_EXPERTS + e]
    row = lax.broadcasted_iota(jnp.int32, (r, t), 0).astype(F32)

    @pl.when(jnp.logical_and(e == 0, j == 0))
    def _():
        o_ref[...] = jnp.zeros_like(o_ref)

    @pl.when(j == 0)
    def _():
        def gather(s, c):
            r0 = pl.multiple_of(s * r, r)
            onehot = jnp.where(rank_ref[...] == row + (s * r).astype(F32), 1.0, 0.0).astype(BF16)
            xs_scr[pl.ds(r0, r), :] = _dot(onehot, h_ref[...]).astype(BF16)
            y_scr[pl.ds(r0, r), :] = jnp.zeros((r, y_scr.shape[1]), F32)
            return c
        lax.fori_loop(0, nb, gather, 0)

    def ffn(s, c):
        r0 = pl.multiple_of(s * r, r)
        xs = xs_scr[pl.ds(r0, r), :]
        act = (_silu(_dot(xs, w1_ref[...])) * _dot(xs, w3_ref[...])).astype(BF16)
        y_scr[pl.ds(r0, r), :] += _dot(act, w2_ref[...])
        return c
    lax.fori_loop(0, nb, ffn, 0)

    @pl.when(j == nff - 1)
    def _():
        def scatter(s, c):
            r0 = pl.multiple_of(s * r, r)
            hit = rank_ref[...] == row + (s * r).astype(F32)
            wgt = jnp.where(hit, gate_ref[...], 0.0).astype(BF16)
            o_ref[...] += _dot_tn(wgt, y_scr[pl.ds(r0, r), :].astype(BF16))
            return c
        lax.fori_loop(0, nb, scatter, 0)

    @pl.when(jnp.logical_and(e == N_EXPERTS - 1, j == nff - 1))
    def _():
        o_ref[...] = x_ref[...] + g2_ref[...] * o_ref[...]


def moe_ffn(x, sc, sh, g, gate2, router_t, w1, w3, w2, t, tf, rows_per_group, n_valid):
    m, d = x.shape
    nt = m // t
    ne, _, ff = w1.shape
    assert ne == N_EXPERTS
    tri = jnp.asarray(np.triu(np.ones((t, t), np.float32)), BF16)
    h, rank, gate, cnt = pl.pallas_call(
        functools.partial(_route_kernel, n_valid),
        grid=(nt,),
        in_specs=[pl.BlockSpec((t, d), lambda i: (i, 0)),
                  _mod_spec(sc, t, rows_per_group), _mod_spec(sh, t, rows_per_group),
                  pl.BlockSpec((1, d), lambda i: (0, 0)),
                  pl.BlockSpec((d, 128), lambda i: (0, 0)),
                  pl.BlockSpec((t, t), lambda i: (0, 0))],
        out_specs=[pl.BlockSpec((t, d), lambda i: (i, 0)),
                   pl.BlockSpec((None, ne, t), lambda i: (i, 0, 0)),
                   pl.BlockSpec((None, ne, t), lambda i: (i, 0, 0)),
                   pl.BlockSpec((None, ne, 128), lambda i: (i, 0, 0))],
        out_shape=[jax.ShapeDtypeStruct((m, d), BF16),
                   jax.ShapeDtypeStruct((nt, ne, t), F32),
                   jax.ShapeDtypeStruct((nt, ne, t), F32),
                   jax.ShapeDtypeStruct((nt, ne, 128), F32)],
        compiler_params=_cp(("arbitrary",)),
        name="moe_route",
    )(x, sc, sh, g, router_t, tri)
    nblk = ((cnt[:, :, 0].astype(jnp.int32) + (MOE_ROWS - 1)) // MOE_ROWS).reshape(nt * ne)
    rank = rank.reshape(nt, ne, 1, t)
    gate = gate.reshape(nt, ne, 1, t)
    return pl.pallas_call(
        _moe_kernel,
        grid_spec=pltpu.PrefetchScalarGridSpec(
            num_scalar_prefetch=1,
            grid=(nt, ne, ff // tf),
            in_specs=[pl.BlockSpec((t, d), lambda i, e, j, nb: (i, 0)),
                      pl.BlockSpec((None, None, 1, t), lambda i, e, j, nb: (i, e, 0, 0)),
                      pl.BlockSpec((None, None, 1, t), lambda i, e, j, nb: (i, e, 0, 0)),
                      pl.BlockSpec((t, d), lambda i, e, j, nb: (i, 0)),
                      _mod_spec(gate2, t, rows_per_group),
                      pl.BlockSpec((None, d, tf), lambda i, e, j, nb: (e, 0, j)),
                      pl.BlockSpec((None, d, tf), lambda i, e, j, nb: (e, 0, j)),
                      pl.BlockSpec((None, tf, d), lambda i, e, j, nb: (e, j, 0))],
            out_specs=pl.BlockSpec((t, d), lambda i, e, j, nb: (i, 0)),
            scratch_shapes=[pltpu.VMEM((t, d), BF16), pltpu.VMEM((t, d), F32)]),
        out_shape=jax.ShapeDtypeStruct((m, d), F32),
        compiler_params=_cp(("arbitrary", "arbitrary", "arbitrary")),
        name="moe_ffn",
    )(nblk, h, rank, gate, x, gate2, w1, w3, w2)


def _conv_kernel(a_ref, g_ref, w_ref, b_ref, lg_ref, lb_ref, o_ref, cn_ref, ext_scr, sh_scr):
    tc = a_ref.shape[0]
    halo = 32
    off = halo - (CONV_K - 1)

    @pl.when(pl.program_id(1) == 0)
    def _():
        ext_scr[0:halo, :] = jnp.zeros((halo, GW), F32)

    ext_scr[halo:halo + tc, :] = a_ref[...] * jax.nn.sigmoid(g_ref[...])
    for s in range(1, 8):
        sh_scr[s - 1] = ext_scr[s:s + tc + halo - 8, :]
    rc = 64
    for c in range(tc // rc):
        acc = jnp.zeros((rc, GW), F32)
        for k in range(CONV_K):
            s, r0 = (k + off) % 8, c * rc + (k + off) // 8 * 8
            rows = ext_scr[r0:r0 + rc, :] if s == 0 else sh_scr[s - 1, r0:r0 + rc, :]
            acc = acc + w_ref[k:k + 1, :] * rows
        y = acc + b_ref[...]
        mu = jnp.mean(y, axis=-1, keepdims=True)
        yc = y - mu
        var = jnp.mean(yc * yc, axis=-1, keepdims=True)
        o_ref[c * rc:(c + 1) * rc, :] = _silu(yc * lax.rsqrt(var + EPS) * lg_ref[...] + lb_ref[...]).astype(o_ref.dtype)
    cn_ref[...] = ext_scr[tc + off:tc + halo, :]
    ext_scr[0:halo, :] = ext_scr[tc:tc + halo, :]


def conv_prompt(z, nb, l, conv_w, conv_b, ln_g, ln_b, tc):
    m = z.shape[0]
    nl = l // tc
    row = lambda v: v.reshape(1, GW)
    return pl.pallas_call(
        _conv_kernel,
        grid=(nb, nl),
        in_specs=[pl.BlockSpec((tc, GW), lambda b, i: (b * nl + i, G_CA)),
                  pl.BlockSpec((tc, GW), lambda b, i: (b * nl + i, G_CG)),
                  pl.BlockSpec((CONV_K, GW), lambda b, i: (0, 0)),
                  pl.BlockSpec((1, GW), lambda b, i: (0, 0)),
                  pl.BlockSpec((1, GW), lambda b, i: (0, 0)),
                  pl.BlockSpec((1, GW), lambda b, i: (0, 0))],
        out_specs=[pl.BlockSpec((tc, GW), lambda b, i: (b * nl + i, 0)),
                   pl.BlockSpec((None, CONV_K - 1, GW), lambda b, i: (b, 0, 0))],
        out_shape=[jax.ShapeDtypeStruct((m, GW), BF16),
                   jax.ShapeDtypeStruct((nb, CONV_K - 1, GW), F32)],
        scratch_shapes=[pltpu.VMEM((tc + 32, GW), F32), pltpu.VMEM((7, tc + 24, GW), F32)],
        compiler_params=_cp(("arbitrary", "arbitrary")),
        name="conv_prompt",
    )(z, z, conv_w, row(conv_b), row(ln_g), row(ln_b))


def _fox_prep_kernel(q_ref, k_ref, v_ref, t_ref, gq_ref, gk_ref, fb_ref, hs_ref, tri_ref,
                     k_all, v_all, lf_all, qb_ref, kb_ref, vb_ref, kt_ref, vt_ref, lft_ref, fc_ref, ft_ref, carry_scr):
    del k_all, v_all, lf_all

    @pl.when(pl.program_id(1) == 0)
    def _():
        carry_scr[...] = jnp.zeros_like(carry_scr)

    hs = hs_ref[...]
    qn = _head_rms(q_ref[...], hs, gq_ref[...])
    kn = _head_rms(k_ref[...], hs, gk_ref[...])
    v = v_ref[...]
    vt = v.T
    qb_ref[...] = (qn * (HD ** -0.5)).astype(BF16)
    kb_ref[...] = kn.astype(BF16)
    vb_ref[...] = vt.astype(BF16)
    kt_ref[...] = kn.T
    vt_ref[...] = vt
    lf = _log_sigmoid(t_ref[...] + fb_ref[...])
    lft_ref[...] = lf.T[:NH]
    cum = _dot_sel(tri_ref[...], lf, "b") + carry_scr[...]
    fc_ref[...] = cum
    ft_ref[...] = cum.T[:8]
    carry_scr[...] = cum[cum.shape[0] - 1:, :]


def _fox_flash_kernel(q_ref, k_ref, vt_ref, fc_ref, ft_ref, o_ref, ot_scr):
    tq = q_ref.shape[0]
    tk = tq
    qi = pl.program_id(1)
    q = q_ref[...]
    head = _head_ids((1, GW), 1)
    kpos = lax.broadcasted_iota(jnp.int32, (tk, tq), 0)
    qpos = lax.broadcasted_iota(jnp.int32, (tk, tq), 1)
    ftq = ft_ref[...]
    qhs = [jnp.where(head == h, q, jnp.zeros_like(q)) for h in range(NH)]

    def step(jb, carry, diagonal):
        k0 = pl.multiple_of(jb * tk, tk)
        kb = k_ref[pl.ds(k0, tk), :]
        fcb = fc_ref[pl.ds(k0, tk), :]
        out = []
        for h in range(NH):
            m, l, acc = carry[h]
            s = _dot_nt(kb, qhs[h]) + (ftq[h:h + 1, :] - fcb[:, h:h + 1])
            if diagonal:
                s = jnp.where(kpos <= qpos, s, NEG)
            m_new = jnp.maximum(m, jnp.max(s, axis=0, keepdims=True))
            alpha = jnp.exp(m - m_new)
            p = jnp.exp(s - m_new)
            l = alpha * l + jnp.sum(p, axis=0, keepdims=True)
            acc = alpha * acc + _dot(vt_ref[jb, h * HD:(h + 1) * HD, :], p.astype(BF16))
            out.append((m_new, l, acc))
        return tuple(out)

    init = tuple((jnp.full((1, tq), NEG, F32), jnp.zeros((1, tq), F32), jnp.zeros((HD, tq), F32)) for _ in range(NH))
    carry = lax.fori_loop(0, qi, functools.partial(step, diagonal=False), init)
    carry = step(qi, carry, True)
    for h in range(NH):
        _, l, acc = carry[h]
        ot_scr[h * HD:(h + 1) * HD, :] = acc * (1.0 / l)
    o_ref[...] = ot_scr[...].T.astype(o_ref.dtype)


def fox_prompt(z, nb, l, gq, gk, fb, tq, li, k_all, v_all, lf_all):
    m = z.shape[0]
    tp = tq
    nl = l // tp
    hs = jnp.asarray(_np_hsum())
    tri = jnp.asarray(np.tril(np.ones((tp, tp), np.float32)))
    row = lambda v: jnp.tile(v, NH).reshape(1, GW)
    fbp = jnp.zeros((1, TAIL_W), F32).at[0, :NH].set(fb)
    blk = lambda g: pl.BlockSpec((tp, GW), lambda b, i: (b * nl + i, g))
    oblk = lambda w: pl.BlockSpec((tp, w), lambda b, i: (b * nl + i, 0))
    anyspec = pl.BlockSpec(memory_space=pl.ANY)
    stacked = lambda r: pl.BlockSpec((None, None, r, tp), lambda b, i: (b, li, 0, i))
    qb, kb, vb, k_all, v_all, lf_all, fc, ft = pl.pallas_call(
        _fox_prep_kernel,
        grid=(nb, nl),
        in_specs=[blk(G_FQ), blk(G_FK), blk(G_FV),
                  pl.BlockSpec((tp, TAIL_W), lambda b, i: (b * nl + i, TAIL_BLK)),
                  pl.BlockSpec((1, GW), lambda b, i: (0, 0)),
                  pl.BlockSpec((1, GW), lambda b, i: (0, 0)),
                  pl.BlockSpec((1, TAIL_W), lambda b, i: (0, 0)),
                  pl.BlockSpec((GW, GW), lambda b, i: (0, 0)),
                  pl.BlockSpec((tp, tp), lambda b, i: (0, 0)),
                  anyspec, anyspec, anyspec],
        out_specs=[oblk(GW)] * 2 + [pl.BlockSpec((None, None, GW, tp), lambda b, i: (b, i, 0, 0)),
                                    stacked(GW), stacked(GW), stacked(NH), oblk(TAIL_W),
                                    pl.BlockSpec((None, None, 8, tp), lambda b, i: (b, i, 0, 0))],
        out_shape=[jax.ShapeDtypeStruct((m, GW), BF16)] * 2 + [jax.ShapeDtypeStruct((nb, nl, GW, tp), BF16),
                   jax.ShapeDtypeStruct(k_all.shape, F32), jax.ShapeDtypeStruct(v_all.shape, F32),
                   jax.ShapeDtypeStruct(lf_all.shape, F32), jax.ShapeDtypeStruct((m, TAIL_W), F32),
                   jax.ShapeDtypeStruct((nb, nl, 8, tp), F32)],
        input_output_aliases={9: 3, 10: 4, 11: 5},
        scratch_shapes=[pltpu.VMEM((1, TAIL_W), F32)],
        compiler_params=_cp(("arbitrary", "arbitrary")),
        name="fox_prep",
    )(z, z, z, z, row(gq), row(gk), fbp, hs, tri, k_all, v_all, lf_all)
    nq = l // tq
    o = pl.pallas_call(
        _fox_flash_kernel,
        grid=(nb, nq),
        in_specs=[pl.BlockSpec((tq, GW), lambda b, i: (b * nq + i, 0)),
                  pl.BlockSpec((l, GW), lambda b, i: (b, 0)),
                  pl.BlockSpec((None, nq, GW, tq), lambda b, i: (b, 0, 0, 0)),
                  pl.BlockSpec((l, TAIL_W), lambda b, i: (b, 0)),
                  pl.BlockSpec((None, None, 8, tq), lambda b, i: (b, i, 0, 0))],
        out_specs=pl.BlockSpec((tq, GW), lambda b, i: (b * nq + i, 0)),
        out_shape=jax.ShapeDtypeStruct((m, GW), BF16),
        scratch_shapes=[pltpu.VMEM((GW, tq), F32)],
        compiler_params=_cp(("arbitrary", "arbitrary")),
        name="fox_flash",
    )(qb, kb, vb, fc, ft)
    return o, k_all, v_all, lf_all


GLA_SAFE_LOG = -80.0


def _gla_kernel(q_ref, k_ref, v_ref, t_ref, go_ref, w2_ref, bg_ref, gn_ref, hs_ref, cs_ref, bd_ref,
                o_ref, st_ref, st_scr, x_scr, o_scr):
    tg = q_ref.shape[0]
    c = GLA_CHUNK

    @pl.when(pl.program_id(1) == 0)
    def _():
        st_scr[...] = jnp.zeros_like(st_scr)

    hs = hs_ref[...]
    bd = bd_ref[...]
    glog = _log_sigmoid(_dot(t_ref[...].astype(BF16), w2_ref[...]) + bg_ref[...]) * (1.0 / GLA_TAU)
    q = q_ref[...] * (HD ** -0.5)
    k = k_ref[...]
    v = v_ref[...]
    vb = v.astype(BF16)
    lower = lax.broadcasted_iota(jnp.int32, (tg, tg), 0) >= lax.broadcasted_iota(jnp.int32, (tg, tg), 1)
    bfull = _dot_sel(jnp.where(lower, 1.0, 0.0), glog, "b")
    safe = jnp.min(bfull) >= GLA_SAFE_LOG

    @pl.when(safe)
    def _():
        head = _head_ids((1, GW), 1)
        btot = bfull[tg - 1:tg]
        qe = (q * jnp.exp(bfull)).astype(BF16)
        kinv = (k * jnp.exp(-bfull)).astype(BF16)
        kend = (k * jnp.exp(btot - bfull)).astype(BF16)
        o = _dot_nt(qe, st_scr[...].astype(BF16))
        for h in range(NH):
            s = _dot_nt(jnp.where(head == h, qe, jnp.zeros_like(qe)), kinv)
            o = o + jnp.where(head == h, _dot(jnp.where(lower, s, 0.0).astype(BF16), vb), 0.0)
        o_scr[...] = o
        st_scr[...] = st_scr[...] * jnp.exp(btot) + _dot_tn(vb, kend) * bd

    @pl.when(jnp.logical_not(safe))
    def _():
        hs_b = hs.astype(BF16)
        cs = cs_ref[...]
        bcum = _dot_sel(jnp.where(lower, cs, 0.0), glog, "b")
        blast = _dot_sel(cs, glog, "b")
        qe = (q * jnp.exp(bcum)).astype(BF16)
        ke = (k * jnp.exp(blast - bcum)).astype(BF16)
        ii = lax.broadcasted_iota(jnp.int32, (c, GW), 0)
        for n in range(tg // c):
            r0 = n * c
            bc = bcum[r0:r0 + c]
            qc = q[r0:r0 + c]
            kc = k[r0:r0 + c]
            vc = v[r0:r0 + c]
            for j in range(c):
                ex = jnp.exp(jnp.where(ii >= j, bc - bc[j:j + 1], -jnp.inf))
                x_scr[j * c:(j + 1) * c, :] = (qc * ex * kc[j:j + 1]).astype(BF16)
            att = _dot(x_scr[...], hs_b)
            o = _dot_nt(qe[r0:r0 + c], st_scr[...].astype(BF16))
            for j in range(c):
                o = o + att[j * c:(j + 1) * c] * vc[j:j + 1]
            o_scr[r0:r0 + c, :] = o
            kv = _dot_tn(vb[r0:r0 + c], ke[r0:r0 + c])
            st_scr[...] = st_scr[...] * jnp.exp(blast[r0:r0 + 1]) + kv * bd

    o_ref[...] = (_head_rms(o_scr[...], hs, gn_ref[...]) * _silu(go_ref[...])).astype(o_ref.dtype)
    st_ref[...] = st_scr[...]


def gla_prompt(z, nb, l, w_gate2, b_gate, norm_g, tg):
    m = z.shape[0]
    nl = l // tg
    hs = jnp.asarray(_np_hsum())
    ch = np.arange(tg) // GLA_CHUNK
    cs = jnp.asarray((ch[:, None] == ch[None, :]).astype(np.float32))
    w2p = jnp.zeros((TAIL_W, GW), F32).at[NH:NH + GLA_LR].set(w_gate2).astype(BF16)
    blk = lambda g: pl.BlockSpec((tg, GW), lambda b, i: (b * nl + i, g))
    full = lambda r, c: pl.BlockSpec((r, c), lambda b, i: (0, 0))
    return pl.pallas_call(
        _gla_kernel,
        grid=(nb, nl),
        in_specs=[blk(G_GQ), blk(G_GK), blk(G_GV),
                  pl.BlockSpec((tg, TAIL_W), lambda b, i: (b * nl + i, TAIL_BLK)),
                  blk(G_GO), full(TAIL_W, GW), full(1, GW), full(1, GW), full(GW, GW), full(tg, tg), full(GW, GW)],
        out_specs=[pl.BlockSpec((tg, GW), lambda b, i: (b * nl + i, 0)),
                   pl.BlockSpec((None, GW, GW), lambda b, i: (b, 0, 0))],
        out_shape=[jax.ShapeDtypeStruct((m, GW), BF16), jax.ShapeDtypeStruct((nb, GW, GW), F32)],
        scratch_shapes=[pltpu.VMEM((GW, GW), F32), pltpu.VMEM((GLA_CHUNK * GLA_CHUNK, GW), BF16),
                        pltpu.VMEM((tg, GW), F32)],
        compiler_params=_cp(("arbitrary", "arbitrary")),
        name="gla_prompt",
    )(z, z, z, z, z, w2p, b_gate.reshape(1, GW), norm_g.reshape(1, GW), hs, cs, hs)


def _ret_lg_row():
    lg = np.log(1.0 - np.exp2(-5.0 - np.arange(NH, dtype=np.float32))).astype(np.float32)
    return np.repeat(lg, HD).reshape(1, GW)


def _ret_decay_mask(c):
    lg = _ret_lg_row()[0, ::HD]
    i = np.arange(c, dtype=np.float32)
    rel = i[:, None] - i[None, :]
    return np.where(rel >= 0, np.exp(np.maximum(rel, 0.0)[None] * lg[:, None, None]), 0.0).astype(np.float32)


def _rope(x, cos, sin_signed):
    first = (lax.broadcasted_iota(jnp.int32, (1, GW), 1) & (HD - 1)) < (HD // 2)
    swapped = jnp.where(first, pltpu.roll(x, GW - HD // 2, axis=1), pltpu.roll(x, HD // 2, axis=1))
    return x * cos + swapped * sin_signed


def _ret_kernel(q_ref, k_ref, v_ref, g_ref, cos_ref, sin_ref, lg_ref, gn_ref, hs_ref, dm_ref, o_ref, st_ref, st_scr):
    tr = dm_ref.shape[1]

    @pl.when(pl.program_id(1) == 0)
    def _():
        st_scr[...] = jnp.zeros_like(st_scr)

    hs = hs_ref[...]
    lg = lg_ref[...]
    head = _head_ids((1, GW), 1)
    ri = lax.broadcasted_iota(jnp.int32, (tr, 1), 0).astype(F32)
    dq = jnp.exp((ri + 1.0) * lg)
    dk = jnp.exp((tr - 1.0 - ri) * lg)
    ds = jnp.exp(tr * lg)
    for n in range(q_ref.shape[0] // tr):
        rows = slice(n * tr, (n + 1) * tr)
        cos = cos_ref[rows, :]
        sin = sin_ref[rows, :]
        q = _rope(q_ref[rows, :], cos, sin)
        k = _rope(k_ref[rows, :], cos, sin) * (HD ** -0.5)
        qb = q.astype(BF16)
        kb = k.astype(BF16)
        vb = v_ref[rows, :].astype(BF16)
        o = _dot_nt((q * dq).astype(BF16), st_scr[...].astype(BF16))
        for h in range(NH):
            att = _dot_nt(jnp.where(head == h, qb, jnp.zeros_like(qb)), kb) * dm_ref[h]
            o = o + jnp.where(head == h, _dot(att.astype(BF16), vb), 0.0)
        st_scr[...] = st_scr[...] * ds + _dot_tn(vb, (k * dk).astype(BF16)) * hs
        o_ref[rows, :] = (_head_rms(o, hs, gn_ref[...]) * _silu(g_ref[rows, :])).astype(o_ref.dtype)
    st_ref[...] = st_scr[...]


def _rope_tables(pos):
    half = HD // 2
    inv = 10000.0 ** (-jnp.arange(half, dtype=F32) / half)
    ang = pos[:, None] * inv[None, :]
    cos = jnp.cos(ang)
    sin = jnp.sin(ang)
    cos_t = jnp.tile(jnp.concatenate([cos, cos], axis=1), (1, NH))
    sin_t = jnp.tile(jnp.concatenate([-sin, sin], axis=1), (1, NH))
    return cos_t, sin_t


def ret_prompt(z, nb, l, norm_g, tt):
    m = z.shape[0]
    nl = l // tt
    tr = RET_CHUNK
    hs = jnp.asarray(_np_hsum())
    cos_t, sin_t = _rope_tables(jnp.arange(l, dtype=F32))
    blk = lambda g: pl.BlockSpec((tt, GW), lambda b, i: (b * nl + i, g))
    full = lambda r, c: pl.BlockSpec((r, c), lambda b, i: (0, 0))
    tab = pl.BlockSpec((tt, GW), lambda b, i: (i, 0))
    return pl.pallas_call(
        _ret_kernel,
        grid=(nb, nl),
        in_specs=[blk(G_RQ), blk(G_RK), blk(G_RV), blk(G_RG), tab, tab, full(1, GW), full(1, GW), full(GW, GW),
                  pl.BlockSpec((NH, tr, tr), lambda b, i: (0, 0, 0))],
        out_specs=[pl.BlockSpec((tt, GW), lambda b, i: (b * nl + i, 0)),
                   pl.BlockSpec((None, GW, GW), lambda b, i: (b, 0, 0))],
        out_shape=[jax.ShapeDtypeStruct((m, GW), BF16), jax.ShapeDtypeStruct((nb, GW, GW), F32)],
        scratch_shapes=[pltpu.VMEM((GW, GW), F32)],
        compiler_params=_cp(("arbitrary", "arbitrary")),
        name="ret_prompt",
    )(z, z, z, z, cos_t, sin_t, jnp.asarray(_ret_lg_row()), norm_g.reshape(1, GW), hs, jnp.asarray(_ret_decay_mask(tr)))


def _state_from_blockdiag(st):
    nb = st.shape[0]
    s5 = st.reshape(nb, NH, HD, NH, HD)
    diag = jnp.stack([s5[:, h, :, h, :] for h in range(NH)], axis=1)
    return diag.transpose(0, 1, 3, 2)


def _sample_rows_kernel(z_ref, buf_ref, cw_ref, cb_ref, lg_ref, lb_ref, gq_ref, gk_ref, fb_ref, w2_ref, bg_ref,
                        cos_ref, sin_ref, rlg_ref, hs_ref,
                        oa_ref, cn_ref, fq_ref, fk_ref, lf_ref, gq_o, ga_o, rq_o, rk_o, ra_o):
    grp = lambda g: z_ref[:, g * GW:(g + 1) * GW]
    tail = z_ref[:, N_MAIN:N_MAIN + TAIL_W]
    hs = hs_ref[...]
    u = grp(G_CA) * jax.nn.sigmoid(grp(G_CG))
    y = cw_ref[CONV_K - 1:CONV_K, :] * u + cb_ref[...]
    for k in range(CONV_K - 1):
        y = y + cw_ref[k:k + 1, :] * buf_ref[k]
    mu = jnp.mean(y, axis=-1, keepdims=True)
    yc = y - mu
    var = jnp.mean(yc * yc, axis=-1, keepdims=True)
    oa_ref[...] = _silu(yc * lax.rsqrt(var + EPS) * lg_ref[...] + lb_ref[...]).astype(oa_ref.dtype)
    for k in range(CONV_K - 2):
        cn_ref[k] = buf_ref[k + 1]
    cn_ref[CONV_K - 2] = u
    fq_ref[...] = _head_rms(grp(G_FQ), hs, gq_ref[...]) * (HD ** -0.5)
    fk_ref[...] = _head_rms(grp(G_FK), hs, gk_ref[...])
    lf_ref[...] = _log_sigmoid(tail + fb_ref[...])
    glog = _log_sigmoid(_wdot(tail, w2_ref[...]) + bg_ref[...]) * (1.0 / GLA_TAU)
    gq_o[...] = grp(G_GQ) * (HD ** -0.5)
    ga_o[...] = jnp.exp(glog)
    rq_o[...] = _rope(grp(G_RQ), cos_ref[...], sin_ref[...])
    rk_o[...] = _rope(grp(G_RK), cos_ref[...], sin_ref[...]) * (HD ** -0.5)
    ra_o[...] = jnp.broadcast_to(jnp.exp(rlg_ref[...]), ra_o.shape)


def _rec_step_kernel(q_ref, k_ref, a_ref, v_ref, s_ref, o_ref, sn_ref):
    for b in range(q_ref.shape[0]):
        for h in range(NH):
            q = q_ref[b, h]
            k = k_ref[b, h]
            a = a_ref[b, h]
            v = v_ref[b, h]
            s = s_ref[b, h]
            qk = jnp.sum(q * k, axis=0, keepdims=True)
            o_ref[b, h] = qk * v + jnp.sum((q * a) * s, axis=0, keepdims=True)
            sn_ref[b, h] = a * s + k * v


REC_SEQS = 8


def rec_step(q, k, a, v, state, li):
    nb = q.shape[0]
    bt = math.gcd(nb, REC_SEQS)
    col = lambda x: x.reshape(nb, NH, HD, 1)
    cspec = pl.BlockSpec((bt, NH, HD, 1), lambda b: (b, 0, 0, 0))
    rspec = pl.BlockSpec((bt, NH, 1, HD), lambda b: (b, 0, 0, 0))
    o, sn = pl.pallas_call(
        _rec_step_kernel,
        grid=(nb // bt,),
        in_specs=[cspec, cspec, cspec, rspec,
                  pl.BlockSpec((None, bt, NH, HD, HD), lambda b: (li, b, 0, 0, 0))],
        out_specs=[rspec, pl.BlockSpec((bt, NH, HD, HD), lambda b: (b, 0, 0, 0))],
        out_shape=[jax.ShapeDtypeStruct((nb, NH, 1, HD), F32), jax.ShapeDtypeStruct((nb, NH, HD, HD), F32)],
        compiler_params=_cp(("arbitrary",)),
        name="rec_step",
    )(col(q), col(k), col(a), v.reshape(nb, NH, 1, HD), state)
    return o.reshape(nb, GW), sn


FOX_PAGES = 32


def _fox_bias_kernel(pt_ref, lfn_ref, lf_ref, ts_ref, pre_ref, o_ref, lf_scr):
    b = pl.program_id(0)
    n_pages = o_ref.shape[1]
    pg = o_ref.shape[2]
    for p in range(n_pages):
        page = lf_ref[pt_ref[b * n_pages + p]]
        for h in range(NH):
            lf_scr[h * n_pages + p:h * n_pages + p + 1, :] = page[h:h + 1, :]
    both = _dot_sel(lf_scr[...], ts_ref[...], "a")
    suf = both[:, :pg]
    tot = both[:, pg:]
    later = _dot_sel(pre_ref[...], tot, "b")
    for h in range(NH):
        r0, r1 = h * n_pages, (h + 1) * n_pages
        o_ref[h] = lfn_ref[h:h + 1, :] + later[r0:r1] + suf[r0:r1]


def _fox_sample_kernel(pt_ref, q_ref, kn_ref, vn_ref, bias_ref, *rest):
    g_n = FOX_PAGES
    k_refs = rest[:g_n]
    v_refs = rest[g_n:2 * g_n]
    o_ref, m_scr, l_scr, acc_scr, s_scr = rest[2 * g_n:]
    c = pl.program_id(1)
    pg = k_refs[0].shape[1]

    @pl.when(c == 0)
    def _():
        m_scr[...] = jnp.full(m_scr.shape, NEG, F32)
        l_scr[...] = jnp.zeros_like(l_scr)
        acc_scr[...] = jnp.zeros_like(acc_scr)

    qb = jnp.broadcast_to(q_ref[...], (GW, pg))
    for g in range(g_n):
        prod = k_refs[g][...] * qb
        for h in range(NH):
            s_scr[h * g_n + g:h * g_n + g + 1, :] = jnp.sum(prod[h * HD:(h + 1) * HD], axis=0, keepdims=True)
    for h in range(NH):
        s_h = s_scr[h * g_n:(h + 1) * g_n, :] + bias_ref[h]
        m_old = m_scr[h:h + 1, :]
        m_new = jnp.maximum(m_old, jnp.max(jnp.max(s_h, axis=0, keepdims=True), axis=1, keepdims=True))
        alpha = jnp.exp(m_old - m_new)
        p = jnp.exp(s_h - m_new)
        m_scr[h:h + 1, :] = m_new
        l_scr[h:h + 1, :] = l_scr[h:h + 1, :] * alpha + jnp.sum(p, axis=0, keepdims=True)
        acc = acc_scr[h * HD:(h + 1) * HD, :] * alpha
        for g in range(g_n):
            acc = acc + p[g:g + 1, :] * v_refs[g][h * HD:(h + 1) * HD, :]
        acc_scr[h * HD:(h + 1) * HD, :] = acc

    @pl.when(c == pl.num_programs(1) - 1)
    def _():
        prod = q_ref[...] * kn_ref[...]
        for h in range(NH):
            s_self = jnp.sum(prod[h * HD:(h + 1) * HD], axis=0, keepdims=True)
            m_h = m_scr[h:h + 1, 0:1]
            m_fin = jnp.maximum(m_h, s_self)
            a_h = jnp.exp(m_h - m_fin)
            p_self = jnp.exp(s_self - m_fin)
            l_tot = jnp.sum(l_scr[h:h + 1, :], axis=1, keepdims=True) * a_h + p_self
            num = jnp.sum(acc_scr[h * HD:(h + 1) * HD, :], axis=1, keepdims=True) * a_h \
                + p_self * vn_ref[h * HD:(h + 1) * HD, :]
            o_ref[h * HD:(h + 1) * HD, :] = num / l_tot


def fox_sample(q, k_new, v_new, lf_new, cache_kt, cache_vt, cache_lf, page_table, li):
    nb, n_pages = page_table.shape
    n_phys = cache_lf.shape[0]
    pg = cache_kt.shape[-1]
    g_n = FOX_PAGES
    assert n_pages % g_n == 0
    nc = n_pages // g_n
    col = lambda x: x.reshape(nb, GW, 1)
    pt = page_table.reshape(-1)
    lfn = jnp.broadcast_to(lf_new[:, :, None], (nb, NH, pg))
    t = np.arange(pg)
    ts = jnp.asarray(np.concatenate([(t[:, None] > t[None, :]).astype(np.float32), np.ones((pg, pg), np.float32)], axis=1))
    r = np.arange(NH * n_pages)
    later = jnp.asarray(((r[:, None] // n_pages == r[None, :] // n_pages) & (r[None, :] > r[:, None])).astype(np.float32))
    bias = pl.pallas_call(
        _fox_bias_kernel,
        grid_spec=pltpu.PrefetchScalarGridSpec(
            num_scalar_prefetch=1,
            grid=(nb,),
            in_specs=[pl.BlockSpec((None, NH, pg), lambda b, pt: (b, 0, 0)),
                      pl.BlockSpec((n_phys, None, NH, pg), lambda b, pt: (0, li, 0, 0)),
                      pl.BlockSpec((pg, 2 * pg), lambda b, pt: (0, 0)),
                      pl.BlockSpec((NH * n_pages, NH * n_pages), lambda b, pt: (0, 0))],
            out_specs=pl.BlockSpec((None, NH, n_pages, pg), lambda b, pt: (b, 0, 0, 0)),
            scratch_shapes=[pltpu.VMEM((NH * n_pages, pg), F32)]),
        out_shape=jax.ShapeDtypeStruct((nb, NH, n_pages, pg), F32),
        compiler_params=_cp(("arbitrary",)),
        name="fox_bias",
    )(pt, lfn, cache_lf, ts, later)

    def page(g):
        return lambda b, c, pt: (pt[b * n_pages + c * g_n + g], li, 0, 0)

    cspec = pl.BlockSpec((None, GW, 1), lambda b, c, pt: (b, 0, 0))
    in_specs = [cspec, cspec, cspec, pl.BlockSpec((None, NH, g_n, pg), lambda b, c, pt: (b, 0, c, 0))]
    in_specs += [pl.BlockSpec((None, None, GW, pg), page(g)) for g in range(g_n)]
    in_specs += [pl.BlockSpec((None, None, GW, pg), page(g)) for g in range(g_n)]
    o = pl.pallas_call(
        _fox_sample_kernel,
        grid_spec=pltpu.PrefetchScalarGridSpec(
            num_scalar_prefetch=1,
            grid=(nb, nc),
            in_specs=in_specs,
            out_specs=pl.BlockSpec((None, GW, 1), lambda b, c, pt: (b, 0, 0)),
            scratch_shapes=[pltpu.VMEM((NH, pg), F32), pltpu.VMEM((NH, pg), F32),
                            pltpu.VMEM((GW, pg), F32), pltpu.VMEM((NH * g_n, pg), F32)]),
        out_shape=jax.ShapeDtypeStruct((nb, GW, 1), F32),
        compiler_params=_cp(("arbitrary", "arbitrary")),
        name="fox_sample",
    )(pt, col(q), col(k_new), col(v_new), bias, *([cache_kt] * g_n), *([cache_vt] * g_n))
    return o.reshape(nb, GW)


def _sample_out_kernel(oa_ref, ob_ref, oc_ref, od_ref, go_ref, rg_ref, gng_ref, rng_ref, hs_ref, w_ref, x_ref, g1_ref, o_ref):
    hs = hs_ref[...]
    oc = _head_rms(oc_ref[...], hs, gng_ref[...]) * _silu(go_ref[...])
    od = _head_rms(od_ref[...], hs, rng_ref[...]) * _silu(rg_ref[...])
    cat = jnp.concatenate([oa_ref[...], ob_ref[...], oc, od], axis=1)
    o_ref[...] = x_ref[...] + g1_ref[...] * _wdot(cat, w_ref[...])


def _prep_w_in(w_in, li):
    wt = w_in.transpose(2, 0, 1)[:, li, :]
    o_ff = 5 * GW
    o_lr = o_ff + NH + 3 * GW
    return jnp.concatenate([wt[:o_ff], wt[o_ff + NH:o_lr], wt[o_lr + GLA_LR:],
                            wt[o_ff:o_ff + NH], wt[o_lr:o_lr + GLA_LR],
                            jnp.zeros((TAIL_W - NH - GLA_LR, wt.shape[1]), wt.dtype)], axis=0)


def _ffn_apply(li, x, sc, sh, g, gate, fw, tm, rows_per_group, n_valid):
    if li % 2 == 0:
        w1, w3, w2 = fw
        return ffn_dense(x, sc, sh, g, gate, w1, w3, w2, tm, w1.shape[1] // 2, rows_per_group)
    router_t, w1, w3, w2 = fw
    return moe_ffn(x, sc, sh, g, gate, router_t, w1, w3, w2, tm, w1.shape[2] // 2, rows_per_group, n_valid)


def kernel(x_prompt, x_sample, c_prompt, c_sample, state_conv, cache_k, cache_v, cache_logf, state_gla, state_ret, page_table, w_in, w_out, conv_w, conv_b, conv_ln_g, conv_ln_b, fox_qn_g, fox_kn_g, fox_fb, gla_w_gate2, gla_b_gate, gla_norm_g, ret_norm_g, norm1_g, norm2_g, w_ada, b_ada, ffn_w1, ffn_w3, ffn_w2, moe_router, moe_w1, moe_w3, moe_w2):
    nbp, l, d = x_prompt.shape
    nbs = x_sample.shape[0]
    depth = w_in.shape[0]
    n_pages, pg = page_table.shape[1], cache_k.shape[2]
    p_len = n_pages * pg
    mp = nbp * l
    ms = 128

    mod = ada_mod(jnp.concatenate([c_prompt, c_sample], axis=0), w_ada, b_ada)
    mod = mod.reshape(depth, nbp + nbs, 6, d)
    cache_kt = cache_k.transpose(0, 1, 3, 4, 2).reshape(cache_k.shape[0], depth, GW, pg)
    cache_vt = cache_v.transpose(0, 1, 3, 4, 2).reshape(cache_v.shape[0], depth, GW, pg)
    cache_lf = cache_logf.transpose(0, 1, 3, 2)
    conv_state = state_conv.transpose(0, 2, 1, 3)
    cos_s, sin_s = _rope_tables(jnp.full((1,), p_len, F32))
    hs = jnp.asarray(_np_hsum())
    rlg = jnp.asarray(_ret_lg_row())

    xp = x_prompt.reshape(mp, d)
    xs = jnp.zeros((ms, d), F32).at[:nbs].set(x_sample.reshape(nbs, d))
    outs_p, outs_s = [], []
    k_all = jnp.zeros((nbp, depth, GW, l), F32)
    v_all = jnp.zeros((nbp, depth, GW, l), F32)
    lf_all = jnp.zeros((nbp, depth, NH, l), F32)
    for li in range(depth):
        mp_l = [mod[li, :nbp, i].reshape(nbp, 1, d) for i in range(6)]
        ms_l = [jnp.zeros((1, ms, d), F32).at[0, :nbs].set(mod[li, nbp:, i]) for i in range(6)]
        n1 = norm1_g[li].reshape(1, d)
        n2 = norm2_g[li].reshape(1, d)
        w_in_f = _prep_w_in(w_in, li)
        w_in_b = w_in_f.astype(BF16)
        w_out_f = w_out[li]
        w_out_b = w_out_f.astype(BF16)
        if li % 2 == 0:
            fw_s = (ffn_w1[li // 2], ffn_w3[li // 2], ffn_w2[li // 2])
            fw = tuple(w.astype(BF16) for w in fw_s)
        else:
            fw = (jnp.pad(moe_router[li // 2], ((0, 0), (0, 128 - N_EXPERTS))), moe_w1[li // 2].astype(BF16), moe_w3[li // 2].astype(BF16),
                  moe_w2[li // 2].astype(BF16))
            fw_s = fw

        sh1, sc1, g1, sh2, sc2, g2 = mp_l
        z = in_proj(xp, sc1, sh1, n1, w_in_b, 512, l)
        out_a, conv_new = conv_prompt(z, nbp, l, conv_w[li], conv_b[li], conv_ln_g[li], conv_ln_b[li], 512)
        out_b, k_all, v_all, lf_all = fox_prompt(z, nbp, l, fox_qn_g[li], fox_kn_g[li], fox_fb[li], 512, li,
                                                 k_all, v_all, lf_all)
        out_c, gla_st = gla_prompt(z, nbp, l, gla_w_gate2[li], gla_b_gate[li], gla_norm_g[li], 256)
        out_d, ret_st = ret_prompt(z, nbp, l, ret_norm_g[li], 512)
        xp = out_proj((out_a, out_b, out_c, out_d), w_out_b, xp, g1, 512, l)
        xp = _ffn_apply(li, xp, sc2, sh2, n2, g2, fw, 512 if li % 2 == 0 else 1024, l, mp)
        outs_p.append((conv_new, None, None, None, _state_from_blockdiag(gla_st), _state_from_blockdiag(ret_st)))

        sh1, sc1, g1, sh2, sc2, g2 = ms_l
        zs = in_proj(xs, sc1, sh1, n1, w_in_f, ms, ms)[:nbs]
        row = lambda v_: v_.reshape(1, GW)
        w2p = jnp.zeros((TAIL_W, GW), F32).at[NH:NH + GLA_LR].set(gla_w_gate2[li])
        fbp = jnp.zeros((1, TAIL_W), F32).at[0, :NH].set(fox_fb[li])
        r32 = jax.ShapeDtypeStruct((nbs, GW), F32)
        (oa, cn, fq, fk, lfs, gq, ga, rq, rk, ra) = pl.pallas_call(
            _sample_rows_kernel,
            out_shape=[r32, jax.ShapeDtypeStruct((CONV_K - 1, nbs, GW), F32), r32, r32,
                       jax.ShapeDtypeStruct((nbs, TAIL_W), F32), r32, r32, r32, r32, r32],
            compiler_params=pltpu.CompilerParams(vmem_limit_bytes=VMEM_LIMIT),
            name="sample_rows",
        )(zs, conv_state[li], conv_w[li], row(conv_b[li]), row(conv_ln_g[li]), row(conv_ln_b[li]),
          row(jnp.tile(fox_qn_g[li], NH)), row(jnp.tile(fox_kn_g[li], NH)), fbp, w2p, row(gla_b_gate[li]),
          cos_s, sin_s, rlg, hs)
        grp = lambda g_: zs[:, g_ * GW:(g_ + 1) * GW]
        f_v, g_k, g_v, r_v = grp(G_FV), grp(G_GK), grp(G_GV), grp(G_RV)
        ob = fox_sample(fq, fk, f_v, lfs[:, :NH], cache_kt, cache_vt, cache_lf, page_table, li)
        oc, gla_new = rec_step(gq, g_k, ga, g_v, state_gla, li)
        od, ret_new = rec_step(rq, rk, ra, r_v, state_ret, li)
        pad = lambda v_: jnp.zeros((ms, v_.shape[1]), F32).at[:nbs].set(v_)
        xs = pl.pallas_call(
            _sample_out_kernel,
            out_shape=jax.ShapeDtypeStruct((ms, d), F32),
            compiler_params=pltpu.CompilerParams(vmem_limit_bytes=VMEM_LIMIT),
            name="sample_out",
        )(pad(oa), pad(ob), pad(oc), pad(od), pad(grp(G_GO)), pad(grp(G_RG)), row(gla_norm_g[li]), row(ret_norm_g[li]),
          hs, w_out_f, xs, g1[0])
        xs = _ffn_apply(li, xs, sc2, sh2, n2, g2, fw_s, ms, ms, nbs)
        outs_s.append((cn.transpose(1, 0, 2), fk.reshape(nbs, 1, NH, HD), f_v.reshape(nbs, 1, NH, HD),
                       lfs[:, :NH].reshape(nbs, 1, NH), gla_new, ret_new))

    st = lambda lst, i, ax: jnp.stack([s[i] for s in lst], axis=ax)
    heads_last = lambda a: a.reshape(nbp, depth, NH, HD, l).transpose(0, 1, 4, 2, 3)
    return (xp.reshape(nbp, l, d), xs[:nbs].reshape(nbs, 1, d),
            st(outs_p, 0, 0), st(outs_s, 0, 0),
            heads_last(k_all), st(outs_s, 1, 1),
            heads_last(v_all), st(outs_s, 2, 1),
            lf_all.transpose(0, 1, 3, 2), st(outs_s, 3, 1),
            st(outs_p, 4, 0), st(outs_s, 4, 0),
            st(outs_p, 5, 0), st(outs_s, 5, 0))
```

```python
import functools
import math

import numpy as np
import jax
import jax.numpy as jnp
from jax import lax
from jax.experimental import pallas as pl
from jax.experimental.pallas import tpu as pltpu

F32 = jnp.float32
BF16 = jnp.bfloat16
HI = lax.Precision.HIGHEST

D_MODEL = 1024
NH = 4
HD = 64
GW = NH * HD
CONV_K = 31
GLA_LR = 16
GLA_TAU = 16.0
GLA_CHUNK = 16
RET_CHUNK = 128
N_EXPERTS = 8
EPS = 1e-6
N_MAIN = 13 * GW
TAIL_W = 128
IN_PAD = N_MAIN + TAIL_W
TAIL_BLK = N_MAIN // TAIL_W
(G_CA, G_CG, G_FQ, G_FK, G_FV, G_GQ, G_GK, G_GV, G_GO, G_RQ, G_RK, G_RV, G_RG) = range(13)
VMEM_LIMIT = 56 * 1024 * 1024
MOE_ROWS = 128
NEG = -1e30


def _cp(sem, vmem=VMEM_LIMIT):
    return pltpu.CompilerParams(dimension_semantics=sem, vmem_limit_bytes=vmem)


def _silu(x):
    return x * jax.nn.sigmoid(x)


def _log_sigmoid(x):
    return jnp.minimum(x, 0.0) - jnp.log1p(jnp.exp(-jnp.abs(x)))


def _dot(a, b):
    return jnp.dot(a, b, preferred_element_type=F32)


def _split3(x):
    hi = x.astype(BF16)
    r1 = x - hi.astype(F32)
    mid = r1.astype(BF16)
    lo = (r1 - mid.astype(F32)).astype(BF16)
    return hi, mid, lo


def _dot_sel(a, b, data):
    if data == "a":
        sel = b.astype(BF16)
        return sum(_dot(p, sel) for p in _split3(a))
    sel = a.astype(BF16)
    return sum(_dot(sel, p) for p in _split3(b))


def _head_ids(shape, dim):
    return lax.shift_right_logical(lax.broadcasted_iota(jnp.int32, shape, dim), int(math.log2(HD)))


def _dot_nt(a, b, **kw):
    return lax.dot_general(a, b, (((1,), (1,)), ((), ())), preferred_element_type=F32, **kw)


def _dot_tn(a, b, **kw):
    return lax.dot_general(a, b, (((0,), (0,)), ((), ())), preferred_element_type=F32, **kw)


def _wdot(a, w, nt=False):
    if w.dtype == BF16:
        a, kw = a.astype(BF16), {}
    else:
        kw = {"precision": HI}
    return _dot_nt(a, w, **kw) if nt else jnp.dot(a, w, preferred_element_type=F32, **kw)


def _modnorm(x, g, sc, sh):
    y = x * lax.rsqrt(jnp.mean(x * x, axis=-1, keepdims=True) + EPS)
    return (y * g) * (1.0 + sc) + sh


def _head_rms(x, hsum, g):
    ms = _dot_sel(x * x, hsum, "a") * (1.0 / HD)
    return x * lax.rsqrt(ms + EPS) * g


def _np_hsum():
    i = np.arange(GW) // HD
    return (i[:, None] == i[None, :]).astype(np.float32)


def _ada_kernel(c_ref, w_ref, b_ref, o_ref):
    c = c_ref[...]
    o_ref[...] = _wdot(_silu(c), w_ref[...]) + b_ref[...]


def ada_mod(c_all, w_ada, b_ada):
    depth, d, n = w_ada.shape
    r = c_all.shape[0]
    tn = 1536
    return pl.pallas_call(
        _ada_kernel,
        grid=(depth, n // tn),
        in_specs=[pl.BlockSpec((r, d), lambda l, j: (0, 0)),
                  pl.BlockSpec((None, d, tn), lambda l, j: (l, 0, j)),
                  pl.BlockSpec((None, 1, tn), lambda l, j: (l, 0, j))],
        out_specs=pl.BlockSpec((None, r, tn), lambda l, j: (l, 0, j)),
        out_shape=jax.ShapeDtypeStruct((depth, r, n), F32),
        compiler_params=_cp(("arbitrary", "arbitrary")),
        name="ada_mod",
    )(c_all, w_ada, b_ada.reshape(depth, 1, n))


def _in_proj_kernel(x_ref, sc_ref, sh_ref, g_ref, w_ref, z_ref):
    h = _modnorm(x_ref[...], g_ref[...], sc_ref[...], sh_ref[...])
    z_ref[...] = _wdot(h, w_ref[...], nt=True)


def _mod_spec(mod, tm, rows_per_group):
    r = mod.shape[1]
    tiles = max(rows_per_group // tm, 1)
    return pl.BlockSpec((None, r, mod.shape[2]), lambda i, *_: (i // tiles, 0, 0))


def in_proj(x, sc, sh, g, w, tm, rows_per_group):
    m, d = x.shape
    n = w.shape[0]
    return pl.pallas_call(
        _in_proj_kernel,
        grid=(m // tm,),
        in_specs=[pl.BlockSpec((tm, d), lambda i: (i, 0)),
                  _mod_spec(sc, tm, rows_per_group), _mod_spec(sh, tm, rows_per_group),
                  pl.BlockSpec((1, d), lambda i: (0, 0)),
                  pl.BlockSpec((n, d), lambda i: (0, 0))],
        out_specs=pl.BlockSpec((tm, n), lambda i: (i, 0)),
        out_shape=jax.ShapeDtypeStruct((m, n), F32),
        compiler_params=_cp(("arbitrary",)),
        name="in_proj",
    )(x, sc, sh, g, w)


def _out_proj_kernel(a_ref, b_ref, c_ref, d_ref, w_ref, x_ref, g1_ref, o_ref):
    cat = jnp.concatenate([a_ref[...], b_ref[...], c_ref[...], d_ref[...]], axis=1)
    o_ref[...] = x_ref[...] + g1_ref[...] * _dot(cat, w_ref[...])


def out_proj(parts, w, x, g1, tm, rows_per_group):
    m, d = x.shape
    return pl.pallas_call(
        _out_proj_kernel,
        grid=(m // tm,),
        in_specs=[pl.BlockSpec((tm, GW), lambda i: (i, 0))] * 4 + [
            pl.BlockSpec((4 * GW, d), lambda i: (0, 0)),
            pl.BlockSpec((tm, d), lambda i: (i, 0)),
            _mod_spec(g1, tm, rows_per_group)],
        out_specs=pl.BlockSpec((tm, d), lambda i: (i, 0)),
        out_shape=jax.ShapeDtypeStruct((m, d), F32),
        compiler_params=_cp(("arbitrary",)),
        name="out_proj",
    )(*parts, w, x, g1)


def _ffn_kernel(x_ref, sc_ref, sh_ref, g_ref, gate_ref, w1_ref, w3_ref, w2_ref, o_ref, h_scr, acc_scr):
    j = pl.program_id(1)

    @pl.when(j == 0)
    def _():
        h_scr[...] = _modnorm(x_ref[...], g_ref[...], sc_ref[...], sh_ref[...]).astype(h_scr.dtype)
        acc_scr[...] = jnp.zeros_like(acc_scr)

    h = h_scr[...]
    acc_scr[...] += _wdot(_silu(_wdot(h, w1_ref[...])) * _wdot(h, w3_ref[...]), w2_ref[...])

    @pl.when(j == pl.num_programs(1) - 1)
    def _():
        o_ref[...] = x_ref[...] + gate_ref[...] * acc_scr[...]


def ffn_dense(x, sc, sh, g, gate, w1, w3, w2, tm, tf, rows_per_group):
    m, d = x.shape
    ff = w1.shape[1]
    return pl.pallas_call(
        _ffn_kernel,
        grid=(m // tm, ff // tf),
        in_specs=[pl.BlockSpec((tm, d), lambda i, j: (i, 0)),
                  _mod_spec(sc, tm, rows_per_group), _mod_spec(sh, tm, rows_per_group),
                  pl.BlockSpec((1, d), lambda i, j: (0, 0)),
                  _mod_spec(gate, tm, rows_per_group),
                  pl.BlockSpec((d, tf), lambda i, j: (0, j)),
                  pl.BlockSpec((d, tf), lambda i, j: (0, j)),
                  pl.BlockSpec((tf, d), lambda i, j: (j, 0))],
        out_specs=pl.BlockSpec((tm, d), lambda i, j: (i, 0)),
        out_shape=jax.ShapeDtypeStruct((m, d), F32),
        scratch_shapes=[pltpu.VMEM((tm, d), w1.dtype), pltpu.VMEM((tm, d), F32)],
        compiler_params=_cp(("arbitrary", "arbitrary")),
        name="ffn_dense",
    )(x, sc, sh, g, gate, w1, w3, w2)


def _route_kernel(n_valid, x_ref, sc_ref, sh_ref, g_ref, rt_ref, tri_ref, h_ref, rank_ref, gate_ref, cnt_ref):
    t = x_ref.shape[0]
    h = _modnorm(x_ref[...], g_ref[...], sc_ref[...], sh_ref[...])
    h_ref[...] = h.astype(BF16)
    logits = jnp.dot(h, rt_ref[...], preferred_element_type=F32, precision=HI).T[:N_EXPERTS]
    e_iota = lax.broadcasted_iota(jnp.int32, logits.shape, 0).astype(F32)
    m1 = jnp.max(logits, axis=0, keepdims=True)
    i1 = jnp.min(jnp.where(logits == m1, e_iota, float(N_EXPERTS)), axis=0, keepdims=True)
    sel1 = e_iota == i1
    rest = jnp.where(sel1, -jnp.inf, logits)
    m2 = jnp.max(rest, axis=0, keepdims=True)
    i2 = jnp.min(jnp.where(rest == m2, e_iota, float(N_EXPERTS)), axis=0, keepdims=True)
    sel2 = e_iota == i2
    e2 = jnp.exp(m2 - m1)
    den = 1.0 + e2
    gate = jnp.where(sel1, 1.0 / den, 0.0) + jnp.where(sel2, e2 / den, 0.0)
    tok = pl.program_id(0) * t + lax.broadcasted_iota(jnp.int32, logits.shape, 1)
    sel = jnp.logical_and(jnp.logical_or(sel1, sel2), tok < n_valid)
    incl = _dot(jnp.where(sel, 1.0, 0.0).astype(BF16), tri_ref[...])
    rank_ref[...] = jnp.where(sel, incl - 1.0, -1.0)
    gate_ref[...] = gate
    cnt_ref[...] = jnp.broadcast_to(incl[:, t - 1:t], cnt_ref.shape)


def _moe_kernel(cnt_ref, h_ref, rank_ref, gate_ref, x_ref, g2_ref, w1_ref, w3_ref, w2_ref, o_ref, xs_scr, y_scr):
    i, e, j = pl.program_id(0), pl.program_id(1), pl.program_id(2)
    nff = pl.num_programs(2)
    t = h_ref.shape[0]
    r = min(MOE_ROWS, t)
    cnt = cnt_ref[i * N_EXPERTS + e]
    nb = (cnt + (r - 1)) // r
    row = lax.broadcasted_iota(jnp.int32, (r, t), 0).astype(F32)

    @pl.when(jnp.logical_and(e == 0, j == 0))
    def _():
        o_ref[...] = jnp.zeros_like(o_ref)
        y_scr[...] = jnp.zeros_like(y_scr)

    @pl.when(j == 0)
    def _():
        def gather(s, c):
            r0 = pl.multiple_of(s * r, r)
            onehot = jnp.where(rank_ref[...] == row + (s * r).astype(F32), 1.0, 0.0).astype(BF16)
            xs_scr[pl.ds(r0, r), :] = _dot(onehot, h_ref[...]).astype(BF16)
            y_scr[pl.ds(r0, r), :] = jnp.zeros((r, y_scr.shape[1]), F32)
            return c
        lax.fori_loop(0, nb, gather, 0)

    def ffn(s, c, rows=r):
        r0 = pl.multiple_of(s * r, r)
        xs = xs_scr[pl.ds(r0, rows), :]
        act = (_silu(_dot(xs, w1_ref[...])) * _dot(xs, w3_ref[...])).astype(BF16)
        y_scr[pl.ds(r0, rows), :] += _dot(act, w2_ref[...])
        return c

    n_full = cnt // r
    rem = cnt - n_full * r
    lax.fori_loop(0, n_full, ffn, 0)
    @pl.when(rem > 0)
    def _():
        ffn(n_full, 0)

    @pl.when(j == nff - 1)
    def _():
        rp = min(max(r, 256), t)
        rowp = lax.broadcasted_iota(jnp.int32, (rp, t), 0).astype(F32)

        def scatter(s, c):
            r0 = pl.multiple_of(s * rp, rp)
            hit = rank_ref[...] == rowp + (s * rp).astype(F32)
            wgt = jnp.where(hit, gate_ref[...], 0.0).astype(BF16)
            o_ref[...] += _dot_tn(wgt, y_scr[pl.ds(r0, rp), :].astype(BF16))
            return c
        lax.fori_loop(0, (nb * r + rp - 1) // rp, scatter, 0)

    @pl.when(jnp.logical_and(e == N_EXPERTS - 1, j == nff - 1))
    def _():
        o_ref[...] = x_ref[...] + g2_ref[...] * o_ref[...]


def moe_ffn(x, sc, sh, g, gate2, router_t, w1, w3, w2, t, tf, rows_per_group, n_valid):
    m, d = x.shape
    nt = m // t
    ne, _, ff = w1.shape
    assert ne == N_EXPERTS
    tri = jnp.asarray(np.triu(np.ones((t, t), np.float32)), BF16)
    h, rank, gate, cnt = pl.pallas_call(
        functools.partial(_route_kernel, n_valid),
        grid=(nt,),
        in_specs=[pl.BlockSpec((t, d), lambda i: (i, 0)),
                  _mod_spec(sc, t, rows_per_group), _mod_spec(sh, t, rows_per_group),
                  pl.BlockSpec((1, d), lambda i: (0, 0)),
                  pl.BlockSpec((d, 128), lambda i: (0, 0)),
                  pl.BlockSpec((t, t), lambda i: (0, 0))],
        out_specs=[pl.BlockSpec((t, d), lambda i: (i, 0)),
                   pl.BlockSpec((None, ne, t), lambda i: (i, 0, 0)),
                   pl.BlockSpec((None, ne, t), lambda i: (i, 0, 0)),
                   pl.BlockSpec((None, ne, 128), lambda i: (i, 0, 0))],
        out_shape=[jax.ShapeDtypeStruct((m, d), BF16),
                   jax.ShapeDtypeStruct((nt, ne, t), F32),
                   jax.ShapeDtypeStruct((nt, ne, t), F32),
                   jax.ShapeDtypeStruct((nt, ne, 128), F32)],
        compiler_params=_cp(("arbitrary",)),
        name="moe_route",
    )(x, sc, sh, g, router_t, tri)
    nblk = cnt[:, :, 0].astype(jnp.int32).reshape(nt * ne)
    rank = rank.reshape(nt, ne, 1, t)
    gate = gate.reshape(nt, ne, 1, t)
    return pl.pallas_call(
        _moe_kernel,
        grid_spec=pltpu.PrefetchScalarGridSpec(
            num_scalar_prefetch=1,
            grid=(nt, ne, ff // tf),
            in_specs=[pl.BlockSpec((t, d), lambda i, e, j, nb: (i, 0)),
                      pl.BlockSpec((None, None, 1, t), lambda i, e, j, nb: (i, e, 0, 0)),
                      pl.BlockSpec((None, None, 1, t), lambda i, e, j, nb: (i, e, 0, 0)),
                      pl.BlockSpec((t, d), lambda i, e, j, nb: (i, 0)),
                      _mod_spec(gate2, t, rows_per_group),
                      pl.BlockSpec((None, d, tf), lambda i, e, j, nb: (e, 0, j)),
                      pl.BlockSpec((None, d, tf), lambda i, e, j, nb: (e, 0, j)),
                      pl.BlockSpec((None, tf, d), lambda i, e, j, nb: (e, j, 0))],
            out_specs=pl.BlockSpec((t, d), lambda i, e, j, nb: (i, 0)),
            scratch_shapes=[pltpu.VMEM((t, d), BF16), pltpu.VMEM((t, d), F32)]),
        out_shape=jax.ShapeDtypeStruct((m, d), F32),
        compiler_params=_cp(("arbitrary", "arbitrary", "arbitrary")),
        name="moe_ffn",
    )(nblk, h, rank, gate, x, gate2, w1, w3, w2)


def _conv_kernel(a_ref, g_ref, w_ref, b_ref, lg_ref, lb_ref, o_ref, cn_ref, ext_scr, sh_scr):
    tc = a_ref.shape[0]
    halo = 32
    off = halo - (CONV_K - 1)

    @pl.when(pl.program_id(1) == 0)
    def _():
        ext_scr[0:halo, :] = jnp.zeros((halo, GW), F32)

    ext_scr[halo:halo + tc, :] = a_ref[...] * jax.nn.sigmoid(g_ref[...])
    for s in range(1, 8):
        sh_scr[s - 1] = ext_scr[s:s + tc + halo - 8, :]
    rc = 64
    for c in range(tc // rc):
        acc = jnp.zeros((rc, GW), F32)
        for k in range(CONV_K):
            s, r0 = (k + off) % 8, c * rc + (k + off) // 8 * 8
            rows = ext_scr[r0:r0 + rc, :] if s == 0 else sh_scr[s - 1, r0:r0 + rc, :]
            acc = acc + w_ref[k:k + 1, :] * rows
        y = acc + b_ref[...]
        mu = jnp.mean(y, axis=-1, keepdims=True)
        yc = y - mu
        var = jnp.mean(yc * yc, axis=-1, keepdims=True)
        o_ref[c * rc:(c + 1) * rc, :] = _silu(yc * lax.rsqrt(var + EPS) * lg_ref[...] + lb_ref[...]).astype(o_ref.dtype)
    cn_ref[...] = ext_scr[tc + off:tc + halo, :]
    ext_scr[0:halo, :] = ext_scr[tc:tc + halo, :]


def conv_prompt(z, nb, l, conv_w, conv_b, ln_g, ln_b, tc):
    m = z.shape[0]
    nl = l // tc
    row = lambda v: v.reshape(1, GW)
    return pl.pallas_call(
        _conv_kernel,
        grid=(nb, nl),
        in_specs=[pl.BlockSpec((tc, GW), lambda b, i: (b * nl + i, G_CA)),
                  pl.BlockSpec((tc, GW), lambda b, i: (b * nl + i, G_CG)),
                  pl.BlockSpec((CONV_K, GW), lambda b, i: (0, 0)),
                  pl.BlockSpec((1, GW), lambda b, i: (0, 0)),
                  pl.BlockSpec((1, GW), lambda b, i: (0, 0)),
                  pl.BlockSpec((1, GW), lambda b, i: (0, 0))],
        out_specs=[pl.BlockSpec((tc, GW), lambda b, i: (b * nl + i, 0)),
                   pl.BlockSpec((None, CONV_K - 1, GW), lambda b, i: (b, 0, 0))],
        out_shape=[jax.ShapeDtypeStruct((m, GW), BF16),
                   jax.ShapeDtypeStruct((nb, CONV_K - 1, GW), F32)],
        scratch_shapes=[pltpu.VMEM((tc + 32, GW), F32), pltpu.VMEM((7, tc + 24, GW), F32)],
        compiler_params=_cp(("arbitrary", "arbitrary")),
        name="conv_prompt",
    )(z, z, conv_w, row(conv_b), row(ln_g), row(ln_b))


def _fox_prep_kernel(q_ref, k_ref, v_ref, t_ref, gq_ref, gk_ref, fb_ref, hs_ref, tri_ref,
                     k_all, v_all, lf_all, qb_ref, kb_ref, vb_ref, kt_ref, vt_ref, lft_ref, fc_ref, carry_scr):
    del k_all, v_all, lf_all

    @pl.when(pl.program_id(1) == 0)
    def _():
        carry_scr[...] = jnp.zeros_like(carry_scr)

    hs = hs_ref[...]
    qn = _head_rms(q_ref[...], hs, gq_ref[...])
    kn = _head_rms(k_ref[...], hs, gk_ref[...])
    v = v_ref[...]
    vt = v.T
    qb_ref[...] = (qn * (HD ** -0.5)).astype(BF16)
    kb_ref[...] = kn.astype(BF16)
    vb_ref[...] = vt.astype(BF16)
    kt_ref[...] = kn.T
    vt_ref[...] = vt
    lf = _log_sigmoid(t_ref[...] + fb_ref[...])
    lft_ref[...] = lf.T[:NH]
    cum = _dot_sel(tri_ref[...], lf, "b") + carry_scr[...]
    fc_ref[...] = cum
    carry_scr[...] = cum[cum.shape[0] - 1:, :]


def _fox_flash_kernel(q_ref, k_ref, vt_ref, fc_ref, o_ref, ot_scr):
    tq = q_ref.shape[0]
    tk = tq
    qi = pl.program_id(1)
    q = q_ref[...]
    head = _head_ids((1, GW), 1)
    kpos = lax.broadcasted_iota(jnp.int32, (tk, tq), 0)
    qpos = lax.broadcasted_iota(jnp.int32, (tk, tq), 1)
    qhs = [jnp.where(head == h, q, jnp.zeros_like(q)) for h in range(NH)]

    def step(jb, carry, diagonal):
        k0 = pl.multiple_of(jb * tk, tk)
        kb = k_ref[pl.ds(k0, tk), :]
        fcb = fc_ref[pl.ds(k0, tk), :]
        out = []
        for h in range(NH):
            m, l, acc = carry[h]
            s = _dot_nt(kb, qhs[h]) - fcb[:, h:h + 1]
            if diagonal:
                s = jnp.where(kpos <= qpos, s, NEG)
            m_new = jnp.maximum(m, jnp.max(s, axis=0, keepdims=True))
            alpha = jnp.exp(m - m_new)
            p = jnp.exp(s - m_new)
            l = alpha * l + jnp.sum(p, axis=0, keepdims=True)
            acc = alpha * acc + _dot(vt_ref[jb, h * HD:(h + 1) * HD, :], p.astype(BF16))
            out.append((m_new, l, acc))
        return tuple(out)

    init = tuple((jnp.full((1, tq), NEG, F32), jnp.zeros((1, tq), F32), jnp.zeros((HD, tq), F32)) for _ in range(NH))
    carry = lax.fori_loop(0, qi, functools.partial(step, diagonal=False), init)
    carry = step(qi, carry, True)
    for h in range(NH):
        _, l, acc = carry[h]
        ot_scr[h * HD:(h + 1) * HD, :] = acc * (1.0 / l)
    o_ref[...] = ot_scr[...].T.astype(o_ref.dtype)


def fox_prompt(z, nb, l, gq, gk, fb, tq, li, k_all, v_all, lf_all):
    m = z.shape[0]
    tp = tq
    nl = l // tp
    hs = jnp.asarray(_np_hsum())
    tri = jnp.asarray(np.tril(np.ones((tp, tp), np.float32)))
    row = lambda v: jnp.tile(v, NH).reshape(1, GW)
    fbp = jnp.zeros((1, TAIL_W), F32).at[0, :NH].set(fb)
    blk = lambda g: pl.BlockSpec((tp, GW), lambda b, i: (b * nl + i, g))
    oblk = lambda w: pl.BlockSpec((tp, w), lambda b, i: (b * nl + i, 0))
    anyspec = pl.BlockSpec(memory_space=pl.ANY)
    stacked = lambda r: pl.BlockSpec((None, None, r, tp), lambda b, i: (b, li, 0, i))
    qb, kb, vb, k_all, v_all, lf_all, fc = pl.pallas_call(
        _fox_prep_kernel,
        grid=(nb, nl),
        in_specs=[blk(G_FQ), blk(G_FK), blk(G_FV),
                  pl.BlockSpec((tp, TAIL_W), lambda b, i: (b * nl + i, TAIL_BLK)),
                  pl.BlockSpec((1, GW), lambda b, i: (0, 0)),
                  pl.BlockSpec((1, GW), lambda b, i: (0, 0)),
                  pl.BlockSpec((1, TAIL_W), lambda b, i: (0, 0)),
                  pl.BlockSpec((GW, GW), lambda b, i: (0, 0)),
                  pl.BlockSpec((tp, tp), lambda b, i: (0, 0)),
                  anyspec, anyspec, anyspec],
        out_specs=[oblk(GW)] * 2 + [pl.BlockSpec((None, None, GW, tp), lambda b, i: (b, i, 0, 0)),
                                    stacked(GW), stacked(GW), stacked(NH), oblk(TAIL_W)],
        out_shape=[jax.ShapeDtypeStruct((m, GW), BF16)] * 2 + [jax.ShapeDtypeStruct((nb, nl, GW, tp), BF16),
                   jax.ShapeDtypeStruct(k_all.shape, F32), jax.ShapeDtypeStruct(v_all.shape, F32),
                   jax.ShapeDtypeStruct(lf_all.shape, F32), jax.ShapeDtypeStruct((m, TAIL_W), F32)],
        input_output_aliases={9: 3, 10: 4, 11: 5},
        scratch_shapes=[pltpu.VMEM((1, TAIL_W), F32)],
        compiler_params=_cp(("arbitrary", "arbitrary")),
        name="fox_prep",
    )(z, z, z, z, row(gq), row(gk), fbp, hs, tri, k_all, v_all, lf_all)
    nq = l // tq
    o = pl.pallas_call(
        _fox_flash_kernel,
        grid=(nb, nq),
        in_specs=[pl.BlockSpec((tq, GW), lambda b, i: (b * nq + i, 0)),
                  pl.BlockSpec((l, GW), lambda b, i: (b, 0)),
                  pl.BlockSpec((None, nq, GW, tq), lambda b, i: (b, 0, 0, 0)),
                  pl.BlockSpec((l, TAIL_W), lambda b, i: (b, 0))],
        out_specs=pl.BlockSpec((tq, GW), lambda b, i: (b * nq + i, 0)),
        out_shape=jax.ShapeDtypeStruct((m, GW), BF16),
        scratch_shapes=[pltpu.VMEM((GW, tq), F32)],
        compiler_params=_cp(("arbitrary", "arbitrary")),
        name="fox_flash",
    )(qb, kb, vb, fc)
    return o, k_all, v_all, lf_all


GLA_SAFE_LOG = -80.0


def _gla_kernel(q_ref, k_ref, v_ref, t_ref, go_ref, w2_ref, bg_ref, gn_ref, hs_ref, cs_ref, bd_ref,
                o_ref, st_ref, st_scr, x_scr, o_scr):
    tg = q_ref.shape[0]
    c = GLA_CHUNK

    @pl.when(pl.program_id(1) == 0)
    def _():
        st_scr[...] = jnp.zeros_like(st_scr)

    hs = hs_ref[...]
    bd = bd_ref[...]
    glog = _log_sigmoid(_dot(t_ref[...].astype(BF16), w2_ref[...]) + bg_ref[...]) * (1.0 / GLA_TAU)
    q = q_ref[...] * (HD ** -0.5)
    k = k_ref[...]
    v = v_ref[...]
    vb = v.astype(BF16)
    lower = lax.broadcasted_iota(jnp.int32, (tg, tg), 0) >= lax.broadcasted_iota(jnp.int32, (tg, tg), 1)
    bfull = _dot_sel(jnp.where(lower, 1.0, 0.0), glog, "b")
    safe = jnp.min(bfull) >= GLA_SAFE_LOG

    @pl.when(safe)
    def _():
        head = _head_ids((1, GW), 1)
        btot = bfull[tg - 1:tg]
        qe = (q * jnp.exp(bfull)).astype(BF16)
        kinv = (k * jnp.exp(-bfull)).astype(BF16)
        kend = (k * jnp.exp(btot - bfull)).astype(BF16)
        o = _dot_nt(qe, st_scr[...].astype(BF16))
        for h in range(NH):
            s = _dot_nt(jnp.where(head == h, qe, jnp.zeros_like(qe)), kinv)
            o = o + jnp.where(head == h, _dot(jnp.where(lower, s, 0.0).astype(BF16), vb), 0.0)
        o_scr[...] = o
        st_scr[...] = st_scr[...] * jnp.exp(btot) + _dot_tn(vb, kend) * bd

    @pl.when(jnp.logical_not(safe))
    def _():
        hs_b = hs.astype(BF16)
        cs = cs_ref[...]
        bcum = _dot_sel(jnp.where(lower, cs, 0.0), glog, "b")
        blast = _dot_sel(cs, glog, "b")
        qe = (q * jnp.exp(bcum)).astype(BF16)
        ke = (k * jnp.exp(blast - bcum)).astype(BF16)
        ii = lax.broadcasted_iota(jnp.int32, (c, GW), 0)
        for n in range(tg // c):
            r0 = n * c
            bc = bcum[r0:r0 + c]
            qc = q[r0:r0 + c]
            kc = k[r0:r0 + c]
            vc = v[r0:r0 + c]
            for j in range(c):
                ex = jnp.exp(jnp.where(ii >= j, bc - bc[j:j + 1], -jnp.inf))
                x_scr[j * c:(j + 1) * c, :] = (qc * ex * kc[j:j + 1]).astype(BF16)
            att = _dot(x_scr[...], hs_b)
            o = _dot_nt(qe[r0:r0 + c], st_scr[...].astype(BF16))
            for j in range(c):
                o = o + att[j * c:(j + 1) * c] * vc[j:j + 1]
            o_scr[r0:r0 + c, :] = o
            kv = _dot_tn(vb[r0:r0 + c], ke[r0:r0 + c])
            st_scr[...] = st_scr[...] * jnp.exp(blast[r0:r0 + 1]) + kv * bd

    o_ref[...] = (_head_rms(o_scr[...], hs, gn_ref[...]) * _silu(go_ref[...])).astype(o_ref.dtype)
    st_ref[...] = st_scr[...]


def gla_prompt(z, nb, l, w_gate2, b_gate, norm_g, tg):
    m = z.shape[0]
    nl = l // tg
    hs = jnp.asarray(_np_hsum())
    ch = np.arange(tg) // GLA_CHUNK
    cs = jnp.asarray((ch[:, None] == ch[None, :]).astype(np.float32))
    w2p = jnp.zeros((TAIL_W, GW), F32).at[NH:NH + GLA_LR].set(w_gate2).astype(BF16)
    blk = lambda g: pl.BlockSpec((tg, GW), lambda b, i: (b * nl + i, g))
    full = lambda r, c: pl.BlockSpec((r, c), lambda b, i: (0, 0))
    return pl.pallas_call(
        _gla_kernel,
        grid=(nb, nl),
        in_specs=[blk(G_GQ), blk(G_GK), blk(G_GV),
                  pl.BlockSpec((tg, TAIL_W), lambda b, i: (b * nl + i, TAIL_BLK)),
                  blk(G_GO), full(TAIL_W, GW), full(1, GW), full(1, GW), full(GW, GW), full(tg, tg), full(GW, GW)],
        out_specs=[pl.BlockSpec((tg, GW), lambda b, i: (b * nl + i, 0)),
                   pl.BlockSpec((None, GW, GW), lambda b, i: (b, 0, 0))],
        out_shape=[jax.ShapeDtypeStruct((m, GW), BF16), jax.ShapeDtypeStruct((nb, GW, GW), F32)],
        scratch_shapes=[pltpu.VMEM((GW, GW), F32), pltpu.VMEM((GLA_CHUNK * GLA_CHUNK, GW), BF16),
                        pltpu.VMEM((tg, GW), F32)],
        compiler_params=_cp(("arbitrary", "arbitrary")),
        name="gla_prompt",
    )(z, z, z, z, z, w2p, b_gate.reshape(1, GW), norm_g.reshape(1, GW), hs, cs, hs)


def _ret_lg_row():
    lg = np.log(1.0 - np.exp2(-5.0 - np.arange(NH, dtype=np.float32))).astype(np.float32)
    return np.repeat(lg, HD).reshape(1, GW)


def _ret_decay_mask(c):
    lg = _ret_lg_row()[0, ::HD]
    i = np.arange(c, dtype=np.float32)
    rel = i[:, None] - i[None, :]
    return np.where(rel >= 0, np.exp(np.maximum(rel, 0.0)[None] * lg[:, None, None]), 0.0).astype(np.float32)


def _rope(x, cos, sin_signed):
    first = (lax.broadcasted_iota(jnp.int32, (1, GW), 1) & (HD - 1)) < (HD // 2)
    swapped = jnp.where(first, pltpu.roll(x, GW - HD // 2, axis=1), pltpu.roll(x, HD // 2, axis=1))
    return x * cos + swapped * sin_signed


def _ret_kernel(q_ref, k_ref, v_ref, g_ref, cos_ref, sin_ref, lg_ref, gn_ref, hs_ref, dm_ref, o_ref, st_ref, st_scr):
    tr = dm_ref.shape[1]

    @pl.when(pl.program_id(1) == 0)
    def _():
        st_scr[...] = jnp.zeros_like(st_scr)

    hs = hs_ref[...]
    lg = lg_ref[...]
    head = _head_ids((1, GW), 1)
    ri = lax.broadcasted_iota(jnp.int32, (tr, 1), 0).astype(F32)
    dq = jnp.exp((ri + 1.0) * lg)
    dk = jnp.exp((tr - 1.0 - ri) * lg)
    ds = jnp.exp(tr * lg)
    for n in range(q_ref.shape[0] // tr):
        rows = slice(n * tr, (n + 1) * tr)
        cos = cos_ref[rows, :]
        sin = sin_ref[rows, :]
        q = _rope(q_ref[rows, :], cos, sin)
        k = _rope(k_ref[rows, :], cos, sin) * (HD ** -0.5)
        qb = q.astype(BF16)
        kb = k.astype(BF16)
        vb = v_ref[rows, :].astype(BF16)
        o = _dot_nt((q * dq).astype(BF16), st_scr[...].astype(BF16))
        for h in range(NH):
            att = _dot_nt(jnp.where(head == h, qb, jnp.zeros_like(qb)), kb) * dm_ref[h]
            o = o + jnp.where(head == h, _dot(att.astype(BF16), vb), 0.0)
        st_scr[...] = st_scr[...] * ds + _dot_tn(vb, (k * dk).astype(BF16)) * hs
        o_ref[rows, :] = (_head_rms(o, hs, gn_ref[...]) * _silu(g_ref[rows, :])).astype(o_ref.dtype)
    st_ref[...] = st_scr[...]


def _rope_tables(pos):
    half = HD // 2
    inv = 10000.0 ** (-jnp.arange(half, dtype=F32) / half)
    ang = pos[:, None] * inv[None, :]
    cos = jnp.cos(ang)
    sin = jnp.sin(ang)
    cos_t = jnp.tile(jnp.concatenate([cos, cos], axis=1), (1, NH))
    sin_t = jnp.tile(jnp.concatenate([-sin, sin], axis=1), (1, NH))
    return cos_t, sin_t


def ret_prompt(z, nb, l, norm_g, tt):
    m = z.shape[0]
    nl = l // tt
    tr = RET_CHUNK
    hs = jnp.asarray(_np_hsum())
    cos_t, sin_t = _rope_tables(jnp.arange(l, dtype=F32))
    blk = lambda g: pl.BlockSpec((tt, GW), lambda b, i: (b * nl + i, g))
    full = lambda r, c: pl.BlockSpec((r, c), lambda b, i: (0, 0))
    tab = pl.BlockSpec((tt, GW), lambda b, i: (i, 0))
    return pl.pallas_call(
        _ret_kernel,
        grid=(nb, nl),
        in_specs=[blk(G_RQ), blk(G_RK), blk(G_RV), blk(G_RG), tab, tab, full(1, GW), full(1, GW), full(GW, GW),
                  pl.BlockSpec((NH, tr, tr), lambda b, i: (0, 0, 0))],
        out_specs=[pl.BlockSpec((tt, GW), lambda b, i: (b * nl + i, 0)),
                   pl.BlockSpec((None, GW, GW), lambda b, i: (b, 0, 0))],
        out_shape=[jax.ShapeDtypeStruct((m, GW), BF16), jax.ShapeDtypeStruct((nb, GW, GW), F32)],
        scratch_shapes=[pltpu.VMEM((GW, GW), F32)],
        compiler_params=_cp(("arbitrary", "arbitrary")),
        name="ret_prompt",
    )(z, z, z, z, cos_t, sin_t, jnp.asarray(_ret_lg_row()), norm_g.reshape(1, GW), hs, jnp.asarray(_ret_decay_mask(tr)))


def _state_from_blockdiag(st):
    nb = st.shape[0]
    s5 = st.reshape(nb, NH, HD, NH, HD)
    diag = jnp.stack([s5[:, h, :, h, :] for h in range(NH)], axis=1)
    return diag.transpose(0, 1, 3, 2)


def _sample_rows_kernel(z_ref, buf_ref, cw_ref, cb_ref, lg_ref, lb_ref, gq_ref, gk_ref, fb_ref, w2_ref, bg_ref,
                        cos_ref, sin_ref, rlg_ref, hs_ref,
                        oa_ref, cn_ref, fq_ref, fk_ref, lf_ref, gq_o, ga_o, rq_o, rk_o, ra_o):
    grp = lambda g: z_ref[:, g * GW:(g + 1) * GW]
    tail = z_ref[:, N_MAIN:N_MAIN + TAIL_W]
    hs = hs_ref[...]
    u = grp(G_CA) * jax.nn.sigmoid(grp(G_CG))
    y = cw_ref[CONV_K - 1:CONV_K, :] * u + cb_ref[...]
    for k in range(CONV_K - 1):
        y = y + cw_ref[k:k + 1, :] * buf_ref[k]
    mu = jnp.mean(y, axis=-1, keepdims=True)
    yc = y - mu
    var = jnp.mean(yc * yc, axis=-1, keepdims=True)
    oa_ref[...] = _silu(yc * lax.rsqrt(var + EPS) * lg_ref[...] + lb_ref[...]).astype(oa_ref.dtype)
    for k in range(CONV_K - 2):
        cn_ref[k] = buf_ref[k + 1]
    cn_ref[CONV_K - 2] = u
    fq_ref[...] = _head_rms(grp(G_FQ), hs, gq_ref[...]) * (HD ** -0.5)
    fk_ref[...] = _head_rms(grp(G_FK), hs, gk_ref[...])
    lf_ref[...] = _log_sigmoid(tail + fb_ref[...])
    glog = _log_sigmoid(_wdot(tail, w2_ref[...]) + bg_ref[...]) * (1.0 / GLA_TAU)
    gq_o[...] = grp(G_GQ) * (HD ** -0.5)
    ga_o[...] = jnp.exp(glog)
    rq_o[...] = _rope(grp(G_RQ), cos_ref[...], sin_ref[...])
    rk_o[...] = _rope(grp(G_RK), cos_ref[...], sin_ref[...]) * (HD ** -0.5)
    ra_o[...] = jnp.broadcast_to(jnp.exp(rlg_ref[...]), ra_o.shape)


def _rec_step_kernel(q_ref, k_ref, a_ref, v_ref, s_ref, o_ref, sn_ref):
    for b in range(q_ref.shape[0]):
        for h in range(NH):
            q = q_ref[b, h]
            k = k_ref[b, h]
            a = a_ref[b, h]
            v = v_ref[b, h]
            s = s_ref[b, h]
            qk = jnp.sum(q * k, axis=0, keepdims=True)
            o_ref[b, h] = qk * v + jnp.sum((q * a) * s, axis=0, keepdims=True)
            sn_ref[b, h] = a * s + k * v


REC_SEQS = 8


def rec_step(q, k, a, v, state, li):
    nb = q.shape[0]
    bt = math.gcd(nb, REC_SEQS)
    col = lambda x: x.reshape(nb, NH, HD, 1)
    cspec = pl.BlockSpec((bt, NH, HD, 1), lambda b: (b, 0, 0, 0))
    rspec = pl.BlockSpec((bt, NH, 1, HD), lambda b: (b, 0, 0, 0))
    o, sn = pl.pallas_call(
        _rec_step_kernel,
        grid=(nb // bt,),
        in_specs=[cspec, cspec, cspec, rspec,
                  pl.BlockSpec((None, bt, NH, HD, HD), lambda b: (li, b, 0, 0, 0))],
        out_specs=[rspec, pl.BlockSpec((bt, NH, HD, HD), lambda b: (b, 0, 0, 0))],
        out_shape=[jax.ShapeDtypeStruct((nb, NH, 1, HD), F32), jax.ShapeDtypeStruct((nb, NH, HD, HD), F32)],
        compiler_params=_cp(("arbitrary",)),
        name="rec_step",
    )(col(q), col(k), col(a), v.reshape(nb, NH, 1, HD), state)
    return o.reshape(nb, GW), sn


FOX_PAGES = 64


def _fox_bias_kernel(pt_ref, lfn_ref, lf_ref, ts_ref, pre_ref, o_ref, lf_scr):
    b = pl.program_id(0)
    n_pages = o_ref.shape[1]
    pg = o_ref.shape[2]
    for p in range(n_pages):
        page = lf_ref[pt_ref[b * n_pages + p]]
        for h in range(NH):
            lf_scr[h * n_pages + p:h * n_pages + p + 1, :] = page[h:h + 1, :]
    both = _dot_sel(lf_scr[...], ts_ref[...], "a")
    suf = both[:, :pg]
    tot = both[:, pg:]
    later = _dot_sel(pre_ref[...], tot, "b")
    for h in range(NH):
        r0, r1 = h * n_pages, (h + 1) * n_pages
        o_ref[h] = lfn_ref[h:h + 1, :] + later[r0:r1] + suf[r0:r1]


def _fox_sample_kernel(pt_ref, q_ref, kn_ref, vn_ref, bias_ref, *rest):
    g_n = bias_ref.shape[1]
    k_refs = rest[:g_n]
    v_refs = rest[g_n:2 * g_n]
    o_ref, m_scr, l_scr, acc_scr, s_scr = rest[2 * g_n:]
    c = pl.program_id(1)
    pg = k_refs[0].shape[1]

    @pl.when(c == 0)
    def _():
        m_scr[...] = jnp.full(m_scr.shape, NEG, F32)
        l_scr[...] = jnp.zeros_like(l_scr)
        acc_scr[...] = jnp.zeros_like(acc_scr)

    qb = jnp.broadcast_to(q_ref[...], (GW, pg))
    for g in range(g_n):
        prod = k_refs[g][...] * qb
        for h in range(NH):
            s_scr[h * g_n + g:h * g_n + g + 1, :] = jnp.sum(prod[h * HD:(h + 1) * HD], axis=0, keepdims=True)
    for h in range(NH):
        s_h = s_scr[h * g_n:(h + 1) * g_n, :] + bias_ref[h]
        m_old = m_scr[h:h + 1, :]
        m_new = jnp.maximum(m_old, jnp.max(jnp.max(s_h, axis=0, keepdims=True), axis=1, keepdims=True))
        alpha = jnp.exp(m_old - m_new)
        p = jnp.exp(s_h - m_new)
        m_scr[h:h + 1, :] = m_new
        l_scr[h:h + 1, :] = l_scr[h:h + 1, :] * alpha + jnp.sum(p, axis=0, keepdims=True)
        acc = acc_scr[h * HD:(h + 1) * HD, :] * alpha
        for g in range(g_n):
            acc = acc + p[g:g + 1, :] * v_refs[g][h * HD:(h + 1) * HD, :]
        acc_scr[h * HD:(h + 1) * HD, :] = acc

    @pl.when(c == pl.num_programs(1) - 1)
    def _():
        prod = q_ref[...] * kn_ref[...]
        for h in range(NH):
            s_self = jnp.sum(prod[h * HD:(h + 1) * HD], axis=0, keepdims=True)
            m_h = m_scr[h:h + 1, 0:1]
            m_fin = jnp.maximum(m_h, s_self)
            a_h = jnp.exp(m_h - m_fin)
            p_self = jnp.exp(s_self - m_fin)
            l_tot = jnp.sum(l_scr[h:h + 1, :], axis=1, keepdims=True) * a_h + p_self
            num = jnp.sum(acc_scr[h * HD:(h + 1) * HD, :], axis=1, keepdims=True) * a_h \
                + p_self * vn_ref[h * HD:(h + 1) * HD, :]
            o_ref[h * HD:(h + 1) * HD, :] = num / l_tot


def fox_sample(q, k_new, v_new, lf_new, cache_kt, cache_vt, cache_lf, page_table, li):
    nb, n_pages = page_table.shape
    n_phys = cache_lf.shape[0]
    pg = cache_kt.shape[-1]
    g_n = math.gcd(FOX_PAGES, n_pages)
    nc = n_pages // g_n
    col = lambda x: x.reshape(nb, GW, 1)
    pt = page_table.reshape(-1)
    lfn = jnp.broadcast_to(lf_new[:, :, None], (nb, NH, pg))
    t = np.arange(pg)
    ts = jnp.asarray(np.concatenate([(t[:, None] > t[None, :]).astype(np.float32), np.ones((pg, pg), np.float32)], axis=1))
    r = np.arange(NH * n_pages)
    later = jnp.asarray(((r[:, None] // n_pages == r[None, :] // n_pages) & (r[None, :] > r[:, None])).astype(np.float32))
    bias = pl.pallas_call(
        _fox_bias_kernel,
        grid_spec=pltpu.PrefetchScalarGridSpec(
            num_scalar_prefetch=1,
            grid=(nb,),
            in_specs=[pl.BlockSpec((None, NH, pg), lambda b, pt: (b, 0, 0)),
                      pl.BlockSpec((n_phys, None, NH, pg), lambda b, pt: (0, li, 0, 0)),
                      pl.BlockSpec((pg, 2 * pg), lambda b, pt: (0, 0)),
                      pl.BlockSpec((NH * n_pages, NH * n_pages), lambda b, pt: (0, 0))],
            out_specs=pl.BlockSpec((None, NH, n_pages, pg), lambda b, pt: (b, 0, 0, 0)),
            scratch_shapes=[pltpu.VMEM((NH * n_pages, pg), F32)]),
        out_shape=jax.ShapeDtypeStruct((nb, NH, n_pages, pg), F32),
        compiler_params=_cp(("arbitrary",)),
        name="fox_bias",
    )(pt, lfn, cache_lf, ts, later)

    def page(g):
        return lambda b, c, pt: (pt[b * n_pages + c * g_n + g], li, 0, 0)

    cspec = pl.BlockSpec((None, GW, 1), lambda b, c, pt: (b, 0, 0))
    in_specs = [cspec, cspec, cspec, pl.BlockSpec((None, NH, g_n, pg), lambda b, c, pt: (b, 0, c, 0))]
    in_specs += [pl.BlockSpec((None, None, GW, pg), page(g)) for g in range(g_n)]
    in_specs += [pl.BlockSpec((None, None, GW, pg), page(g)) for g in range(g_n)]
    o = pl.pallas_call(
        _fox_sample_kernel,
        grid_spec=pltpu.PrefetchScalarGridSpec(
            num_scalar_prefetch=1,
            grid=(nb, nc),
            in_specs=in_specs,
            out_specs=pl.BlockSpec((None, GW, 1), lambda b, c, pt: (b, 0, 0)),
            scratch_shapes=[pltpu.VMEM((NH, pg), F32), pltpu.VMEM((NH, pg), F32),
                            pltpu.VMEM((GW, pg), F32), pltpu.VMEM((NH * g_n, pg), F32)]),
        out_shape=jax.ShapeDtypeStruct((nb, GW, 1), F32),
        compiler_params=_cp(("arbitrary", "arbitrary")),
        name="fox_sample",
    )(pt, col(q), col(k_new), col(v_new), bias, *([cache_kt] * g_n), *([cache_vt] * g_n))
    return o.reshape(nb, GW)


def _sample_out_kernel(oa_ref, ob_ref, oc_ref, od_ref, go_ref, rg_ref, gng_ref, rng_ref, hs_ref, w_ref, x_ref, g1_ref, o_ref):
    hs = hs_ref[...]
    oc = _head_rms(oc_ref[...], hs, gng_ref[...]) * _silu(go_ref[...])
    od = _head_rms(od_ref[...], hs, rng_ref[...]) * _silu(rg_ref[...])
    cat = jnp.concatenate([oa_ref[...], ob_ref[...], oc, od], axis=1)
    o_ref[...] = x_ref[...] + g1_ref[...] * _wdot(cat, w_ref[...])


def _prep_w_in(w_in, li):
    wt = w_in.transpose(2, 0, 1)[:, li, :]
    o_ff = 5 * GW
    o_lr = o_ff + NH + 3 * GW
    return jnp.concatenate([wt[:o_ff], wt[o_ff + NH:o_lr], wt[o_lr + GLA_LR:],
                            wt[o_ff:o_ff + NH], wt[o_lr:o_lr + GLA_LR],
                            jnp.zeros((TAIL_W - NH - GLA_LR, wt.shape[1]), wt.dtype)], axis=0)


def _ffn_apply(li, x, sc, sh, g, gate, fw, tm, rows_per_group, n_valid):
    if li % 2 == 0:
        w1, w3, w2 = fw
        return ffn_dense(x, sc, sh, g, gate, w1, w3, w2, tm, w1.shape[1] // 2, rows_per_group)
    router_t, w1, w3, w2 = fw
    return moe_ffn(x, sc, sh, g, gate, router_t, w1, w3, w2, tm, w1.shape[2] // 2, rows_per_group, n_valid)


def kernel(x_prompt, x_sample, c_prompt, c_sample, state_conv, cache_k, cache_v, cache_logf, state_gla, state_ret, page_table, w_in, w_out, conv_w, conv_b, conv_ln_g, conv_ln_b, fox_qn_g, fox_kn_g, fox_fb, gla_w_gate2, gla_b_gate, gla_norm_g, ret_norm_g, norm1_g, norm2_g, w_ada, b_ada, ffn_w1, ffn_w3, ffn_w2, moe_router, moe_w1, moe_w3, moe_w2):
    nbp, l, d = x_prompt.shape
    nbs = x_sample.shape[0]
    depth = w_in.shape[0]
    n_pages, pg = page_table.shape[1], cache_k.shape[2]
    p_len = n_pages * pg
    mp = nbp * l
    ms = 128

    mod = ada_mod(jnp.concatenate([c_prompt, c_sample], axis=0), w_ada, b_ada)
    mod = mod.reshape(depth, nbp + nbs, 6, d)
    cache_kt = cache_k.transpose(0, 1, 3, 4, 2).reshape(cache_k.shape[0], depth, GW, pg)
    cache_vt = cache_v.transpose(0, 1, 3, 4, 2).reshape(cache_v.shape[0], depth, GW, pg)
    cache_lf = cache_logf.transpose(0, 1, 3, 2)
    conv_state = state_conv.transpose(0, 2, 1, 3)
    cos_s, sin_s = _rope_tables(jnp.full((1,), p_len, F32))
    hs = jnp.asarray(_np_hsum())
    rlg = jnp.asarray(_ret_lg_row())

    xp = x_prompt.reshape(mp, d)
    xs = jnp.zeros((ms, d), F32).at[:nbs].set(x_sample.reshape(nbs, d))
    outs_p, outs_s = [], []
    k_all = jnp.zeros((nbp, depth, GW, l), F32)
    v_all = jnp.zeros((nbp, depth, GW, l), F32)
    lf_all = jnp.zeros((nbp, depth, NH, l), F32)
    for li in range(depth):
        mp_l = [mod[li, :nbp, i].reshape(nbp, 1, d) for i in range(6)]
        ms_l = [jnp.zeros((1, ms, d), F32).at[0, :nbs].set(mod[li, nbp:, i]) for i in range(6)]
        n1 = norm1_g[li].reshape(1, d)
        n2 = norm2_g[li].reshape(1, d)
        w_in_f = _prep_w_in(w_in, li)
        w_in_b = w_in_f.astype(BF16)
        w_out_f = w_out[li]
        w_out_b = w_out_f.astype(BF16)
        if li % 2 == 0:
            fw_s = (ffn_w1[li // 2], ffn_w3[li // 2], ffn_w2[li // 2])
            fw = tuple(w.astype(BF16) for w in fw_s)
        else:
            fw = (jnp.pad(moe_router[li // 2], ((0, 0), (0, 128 - N_EXPERTS))), moe_w1[li // 2].astype(BF16), moe_w3[li // 2].astype(BF16),
                  moe_w2[li // 2].astype(BF16))
            fw_s = fw

        sh1, sc1, g1, sh2, sc2, g2 = mp_l
        z = in_proj(xp, sc1, sh1, n1, w_in_b, 512, l)
        out_a, conv_new = conv_prompt(z, nbp, l, conv_w[li], conv_b[li], conv_ln_g[li], conv_ln_b[li], 512)
        out_b, k_all, v_all, lf_all = fox_prompt(z, nbp, l, fox_qn_g[li], fox_kn_g[li], fox_fb[li], 512, li,
                                                 k_all, v_all, lf_all)
        out_c, gla_st = gla_prompt(z, nbp, l, gla_w_gate2[li], gla_b_gate[li], gla_norm_g[li], 256)
        out_d, ret_st = ret_prompt(z, nbp, l, ret_norm_g[li], 512)
        xp = out_proj((out_a, out_b, out_c, out_d), w_out_b, xp, g1, 512, l)
        xp = _ffn_apply(li, xp, sc2, sh2, n2, g2, fw, 512 if li % 2 == 0 else 1024, l, mp)
        outs_p.append((conv_new, None, None, None, _state_from_blockdiag(gla_st), _state_from_blockdiag(ret_st)))

        sh1, sc1, g1, sh2, sc2, g2 = ms_l
        zs = in_proj(xs, sc1, sh1, n1, w_in_f, ms, ms)[:nbs]
        row = lambda v_: v_.reshape(1, GW)
        w2p = jnp.zeros((TAIL_W, GW), F32).at[NH:NH + GLA_LR].set(gla_w_gate2[li])
        fbp = jnp.zeros((1, TAIL_W), F32).at[0, :NH].set(fox_fb[li])
        r32 = jax.ShapeDtypeStruct((nbs, GW), F32)
        (oa, cn, fq, fk, lfs, gq, ga, rq, rk, ra) = pl.pallas_call(
            _sample_rows_kernel,
            out_shape=[r32, jax.ShapeDtypeStruct((CONV_K - 1, nbs, GW), F32), r32, r32,
                       jax.ShapeDtypeStruct((nbs, TAIL_W), F32), r32, r32, r32, r32, r32],
            compiler_params=pltpu.CompilerParams(vmem_limit_bytes=VMEM_LIMIT),
            name="sample_rows",
        )(zs, conv_state[li], conv_w[li], row(conv_b[li]), row(conv_ln_g[li]), row(conv_ln_b[li]),
          row(jnp.tile(fox_qn_g[li], NH)), row(jnp.tile(fox_kn_g[li], NH)), fbp, w2p, row(gla_b_gate[li]),
          cos_s, sin_s, rlg, hs)
        grp = lambda g_: zs[:, g_ * GW:(g_ + 1) * GW]
        f_v, g_k, g_v, r_v = grp(G_FV), grp(G_GK), grp(G_GV), grp(G_RV)
        ob = fox_sample(fq, fk, f_v, lfs[:, :NH], cache_kt, cache_vt, cache_lf, page_table, li)
        oc, gla_new = rec_step(gq, g_k, ga, g_v, state_gla, li)
        od, ret_new = rec_step(rq, rk, ra, r_v, state_ret, li)
        pad = lambda v_: jnp.zeros((ms, v_.shape[1]), F32).at[:nbs].set(v_)
        xs = pl.pallas_call(
            _sample_out_kernel,
            out_shape=jax.ShapeDtypeStruct((ms, d), F32),
            compiler_params=pltpu.CompilerParams(vmem_limit_bytes=VMEM_LIMIT),
            name="sample_out",
        )(pad(oa), pad(ob), pad(oc), pad(od), pad(grp(G_GO)), pad(grp(G_RG)), row(gla_norm_g[li]), row(ret_norm_g[li]),
          hs, w_out_f, xs, g1[0])
        xs = _ffn_apply(li, xs, sc2, sh2, n2, g2, fw_s, ms, ms, nbs)
        outs_s.append((cn.transpose(1, 0, 2), fk.reshape(nbs, 1, NH, HD), f_v.reshape(nbs, 1, NH, HD),
                       lfs[:, :NH].reshape(nbs, 1, NH), gla_new, ret_new))

    st = lambda lst, i, ax: jnp.stack([s[i] for s in lst], axis=ax)
    heads_last = lambda a: a.reshape(nbp, depth, NH, HD, l).transpose(0, 1, 4, 2, 3)
    return (xp.reshape(nbp, l, d), xs[:nbs].reshape(nbs, 1, d),
            st(outs_p, 0, 0), st(outs_s, 0, 0),
            heads_last(k_all), st(outs_s, 1, 1),
            heads_last(v_all), st(outs_s, 2, 1),
            lf_all.transpose(0, 1, 3, 2), st(outs_s, 3, 1),
            st(outs_p, 4, 0), st(outs_s, 4, 0),
            st(outs_p, 5, 0), st(outs_s, 5, 0))
```

```python
import functools
import math

import numpy as np
import jax
import jax.numpy as jnp
from jax import lax
from jax.experimental import pallas as pl
from jax.experimental.pallas import tpu as pltpu

F32 = jnp.float32
BF16 = jnp.bfloat16
HI = lax.Precision.HIGHEST

D_MODEL = 1024
NH = 4
HD = 64
GW = NH * HD
CONV_K = 31
GLA_LR = 16
GLA_TAU = 16.0
GLA_CHUNK = 16
RET_CHUNK = 128
N_EXPERTS = 8
EPS = 1e-6
N_MAIN = 13 * GW
TAIL_W = 128
IN_PAD = N_MAIN + TAIL_W
TAIL_BLK = N_MAIN // TAIL_W
(G_CA, G_CG, G_FQ, G_FK, G_FV, G_GQ, G_GK, G_GV, G_GO, G_RQ, G_RK, G_RV, G_RG) = range(13)
VMEM_LIMIT = 56 * 1024 * 1024
MOE_ROWS = 128
NEG = -1e30


def _cp(sem, vmem=VMEM_LIMIT):
    return pltpu.CompilerParams(dimension_semantics=sem, vmem_limit_bytes=vmem)


def _silu(x):
    return x * jax.nn.sigmoid(x)


def _log_sigmoid(x):
    return jnp.minimum(x, 0.0) - jnp.log1p(jnp.exp(-jnp.abs(x)))


def _dot(a, b):
    return jnp.dot(a, b, preferred_element_type=F32)


def _split3(x):
    hi = x.astype(BF16)
    r1 = x - hi.astype(F32)
    mid = r1.astype(BF16)
    lo = (r1 - mid.astype(F32)).astype(BF16)
    return hi, mid, lo


def _dot_sel(a, b, data):
    if data == "a":
        sel = b.astype(BF16)
        return sum(_dot(p, sel) for p in _split3(a))
    sel = a.astype(BF16)
    return sum(_dot(sel, p) for p in _split3(b))


def _head_ids(shape, dim):
    return lax.shift_right_logical(lax.broadcasted_iota(jnp.int32, shape, dim), int(math.log2(HD)))


def _dot_nt(a, b, **kw):
    return lax.dot_general(a, b, (((1,), (1,)), ((), ())), preferred_element_type=F32, **kw)


def _dot_tn(a, b, **kw):
    return lax.dot_general(a, b, (((0,), (0,)), ((), ())), preferred_element_type=F32, **kw)


def _wdot(a, w, nt=False):
    if w.dtype == BF16:
        a, kw = a.astype(BF16), {}
    else:
        kw = {"precision": HI}
    return _dot_nt(a, w, **kw) if nt else jnp.dot(a, w, preferred_element_type=F32, **kw)


def _modnorm(x, g, sc, sh):
    y = x * lax.rsqrt(jnp.mean(x * x, axis=-1, keepdims=True) + EPS)
    return (y * g) * (1.0 + sc) + sh


def _head_rms(x, hsum, g):
    ms = _dot_sel(x * x, hsum, "a") * (1.0 / HD)
    return x * lax.rsqrt(ms + EPS) * g


def _np_hsum():
    i = np.arange(GW) // HD
    return (i[:, None] == i[None, :]).astype(np.float32)


def _ada_kernel(c_ref, w_ref, b_ref, o_ref):
    c = c_ref[...]
    o_ref[...] = _wdot(_silu(c), w_ref[...]) + b_ref[...]


def ada_mod(c_all, w_ada, b_ada):
    depth, d, n = w_ada.shape
    r = c_all.shape[0]
    tn = 1536
    return pl.pallas_call(
        _ada_kernel,
        grid=(depth, n // tn),
        in_specs=[pl.BlockSpec((r, d), lambda l, j: (0, 0)),
                  pl.BlockSpec((None, d, tn), lambda l, j: (l, 0, j)),
                  pl.BlockSpec((None, 1, tn), lambda l, j: (l, 0, j))],
        out_specs=pl.BlockSpec((None, r, tn), lambda l, j: (l, 0, j)),
        out_shape=jax.ShapeDtypeStruct((depth, r, n), F32),
        compiler_params=_cp(("arbitrary", "arbitrary")),
        name="ada_mod",
    )(c_all, w_ada, b_ada.reshape(depth, 1, n))


def _in_proj_kernel(x_ref, sc_ref, sh_ref, g_ref, w_ref, z_ref):
    h = _modnorm(x_ref[...], g_ref[...], sc_ref[...], sh_ref[...])
    z_ref[...] = _wdot(h, w_ref[...], nt=True)


def _mod_spec(mod, tm, rows_per_group):
    r = mod.shape[1]
    tiles = max(rows_per_group // tm, 1)
    return pl.BlockSpec((None, r, mod.shape[2]), lambda i, *_: (i // tiles, 0, 0))


def in_proj(x, sc, sh, g, w, tm, rows_per_group):
    m, d = x.shape
    n = w.shape[0]
    return pl.pallas_call(
        _in_proj_kernel,
        grid=(m // tm,),
        in_specs=[pl.BlockSpec((tm, d), lambda i: (i, 0)),
                  _mod_spec(sc, tm, rows_per_group), _mod_spec(sh, tm, rows_per_group),
                  pl.BlockSpec((1, d), lambda i: (0, 0)),
                  pl.BlockSpec((n, d), lambda i: (0, 0))],
        out_specs=pl.BlockSpec((tm, n), lambda i: (i, 0)),
        out_shape=jax.ShapeDtypeStruct((m, n), F32),
        compiler_params=_cp(("arbitrary",)),
        name="in_proj",
    )(x, sc, sh, g, w)


def _out_proj_kernel(a_ref, b_ref, c_ref, d_ref, w_ref, x_ref, g1_ref, o_ref):
    cat = jnp.concatenate([a_ref[...], b_ref[...], c_ref[...], d_ref[...]], axis=1)
    o_ref[...] = x_ref[...] + g1_ref[...] * _dot(cat, w_ref[...])


def out_proj(parts, w, x, g1, tm, rows_per_group):
    m, d = x.shape
    return pl.pallas_call(
        _out_proj_kernel,
        grid=(m // tm,),
        in_specs=[pl.BlockSpec((tm, GW), lambda i: (i, 0))] * 4 + [
            pl.BlockSpec((4 * GW, d), lambda i: (0, 0)),
            pl.BlockSpec((tm, d), lambda i: (i, 0)),
            _mod_spec(g1, tm, rows_per_group)],
        out_specs=pl.BlockSpec((tm, d), lambda i: (i, 0)),
        out_shape=jax.ShapeDtypeStruct((m, d), F32),
        compiler_params=_cp(("arbitrary",)),
        name="out_proj",
    )(*parts, w, x, g1)


def _ffn_kernel(x_ref, sc_ref, sh_ref, g_ref, gate_ref, w1_ref, w3_ref, w2_ref, o_ref, h_scr, acc_scr):
    j = pl.program_id(1)

    @pl.when(j == 0)
    def _():
        h_scr[...] = _modnorm(x_ref[...], g_ref[...], sc_ref[...], sh_ref[...]).astype(h_scr.dtype)
        acc_scr[...] = jnp.zeros_like(acc_scr)

    h = h_scr[...]
    acc_scr[...] += _wdot(_silu(_wdot(h, w1_ref[...])) * _wdot(h, w3_ref[...]), w2_ref[...])

    @pl.when(j == pl.num_programs(1) - 1)
    def _():
        o_ref[...] = x_ref[...] + gate_ref[...] * acc_scr[...]


def ffn_dense(x, sc, sh, g, gate, w1, w3, w2, tm, tf, rows_per_group):
    m, d = x.shape
    ff = w1.shape[1]
    return pl.pallas_call(
        _ffn_kernel,
        grid=(m // tm, ff // tf),
        in_specs=[pl.BlockSpec((tm, d), lambda i, j: (i, 0)),
                  _mod_spec(sc, tm, rows_per_group), _mod_spec(sh, tm, rows_per_group),
                  pl.BlockSpec((1, d), lambda i, j: (0, 0)),
                  _mod_spec(gate, tm, rows_per_group),
                  pl.BlockSpec((d, tf), lambda i, j: (0, j)),
                  pl.BlockSpec((d, tf), lambda i, j: (0, j)),
                  pl.BlockSpec((tf, d), lambda i, j: (j, 0))],
        out_specs=pl.BlockSpec((tm, d), lambda i, j: (i, 0)),
        out_shape=jax.ShapeDtypeStruct((m, d), F32),
        scratch_shapes=[pltpu.VMEM((tm, d), w1.dtype), pltpu.VMEM((tm, d), F32)],
        compiler_params=_cp(("arbitrary", "arbitrary")),
        name="ffn_dense",
    )(x, sc, sh, g, gate, w1, w3, w2)


def _route_kernel(n_valid, x_ref, sc_ref, sh_ref, g_ref, rt_ref, tri_ref, *rest):
    if len(rest) > 4:
        (w1_ref, w3_ref, w2_ref), rest = rest[:3], rest[3:]
        for src, dst in zip((w1_ref, w3_ref, w2_ref), rest[4:]):
            dst[...] = src[...].astype(BF16)
    h_ref, rank_ref, gate_ref, cnt_ref = rest[:4]
    pl.when(pl.program_id(1) == 0)(functools.partial(
        _route_tile, n_valid, x_ref, sc_ref, sh_ref, g_ref, rt_ref, tri_ref, h_ref, rank_ref, gate_ref, cnt_ref))


def _route_tile(n_valid, x_ref, sc_ref, sh_ref, g_ref, rt_ref, tri_ref, h_ref, rank_ref, gate_ref, cnt_ref):
    t = x_ref.shape[0]
    h = _modnorm(x_ref[...], g_ref[...], sc_ref[...], sh_ref[...])
    h_ref[...] = h.astype(BF16)
    logits = jnp.dot(h, rt_ref[...], preferred_element_type=F32, precision=HI).T[:N_EXPERTS]
    e_iota = lax.broadcasted_iota(jnp.int32, logits.shape, 0).astype(F32)
    m1 = jnp.max(logits, axis=0, keepdims=True)
    i1 = jnp.min(jnp.where(logits == m1, e_iota, float(N_EXPERTS)), axis=0, keepdims=True)
    sel1 = e_iota == i1
    rest = jnp.where(sel1, -jnp.inf, logits)
    m2 = jnp.max(rest, axis=0, keepdims=True)
    i2 = jnp.min(jnp.where(rest == m2, e_iota, float(N_EXPERTS)), axis=0, keepdims=True)
    sel2 = e_iota == i2
    e2 = jnp.exp(m2 - m1)
    den = 1.0 + e2
    gate = jnp.where(sel1, 1.0 / den, 0.0) + jnp.where(sel2, e2 / den, 0.0)
    tok = pl.program_id(0) * t + lax.broadcasted_iota(jnp.int32, logits.shape, 1)
    sel = jnp.logical_and(jnp.logical_or(sel1, sel2), tok < n_valid)
    incl = _dot(jnp.where(sel, 1.0, 0.0).astype(BF16), tri_ref[...])
    rank_ref[...] = jnp.where(sel, incl - 1.0, -1.0)
    gate_ref[...] = gate
    cnt_ref[...] = jnp.broadcast_to(incl[:, t - 1:t], cnt_ref.shape)


def _moe_kernel(cnt_ref, h_ref, rank_ref, gate_ref, x_ref, g2_ref, w1_ref, w3_ref, w2_ref, o_ref, xs_scr, y_scr):
    i, e, j = pl.program_id(0), pl.program_id(1), pl.program_id(2)
    nff = pl.num_programs(2)
    t = h_ref.shape[0]
    r = min(MOE_ROWS, t)
    cnt = cnt_ref[i * N_EXPERTS + e]
    nb = (cnt + (r - 1)) // r
    row = lax.broadcasted_iota(jnp.int32, (r, t), 0).astype(F32)

    @pl.when(jnp.logical_and(e == 0, j == 0))
    def _():
        o_ref[...] = jnp.zeros_like(o_ref)
        y_scr[...] = jnp.zeros_like(y_scr)

    @pl.when(j == 0)
    def _():
        def gather(s, c):
            r0 = pl.multiple_of(s * r, r)
            onehot = jnp.where(rank_ref[...] == row + (s * r).astype(F32), 1.0, 0.0).astype(BF16)
            xs_scr[pl.ds(r0, r), :] = _dot(onehot, h_ref[...]).astype(BF16)
            y_scr[pl.ds(r0, r), :] = jnp.zeros((r, y_scr.shape[1]), F32)
            return c
        lax.fori_loop(0, nb, gather, 0)

    def ffn(s, c, rows=r):
        r0 = pl.multiple_of(s * r, r)
        xs = xs_scr[pl.ds(r0, rows), :]
        act = (_silu(_dot(xs, w1_ref[...])) * _dot(xs, w3_ref[...])).astype(BF16)
        y_scr[pl.ds(r0, rows), :] += _dot(act, w2_ref[...])
        return c

    n_full = cnt // r
    rem = cnt - n_full * r
    lax.fori_loop(0, n_full, ffn, 0)
    @pl.when(rem > 0)
    def _():
        ffn(n_full, 0)

    @pl.when(j == nff - 1)
    def _():
        rp = min(max(r, 256), t)
        rowp = lax.broadcasted_iota(jnp.int32, (rp, t), 0).astype(F32)

        def scatter(s, c):
            r0 = pl.multiple_of(s * rp, rp)
            hit = rank_ref[...] == rowp + (s * rp).astype(F32)
            wgt = jnp.where(hit, gate_ref[...], 0.0).astype(BF16)
            o_ref[...] += _dot_tn(wgt, y_scr[pl.ds(r0, rp), :].astype(BF16))
            return c
        lax.fori_loop(0, (nb * r + rp - 1) // rp, scatter, 0)

    @pl.when(jnp.logical_and(e == N_EXPERTS - 1, j == nff - 1))
    def _():
        o_ref[...] = x_ref[...] + g2_ref[...] * o_ref[...]


def moe_ffn(x, sc, sh, g, gate2, router_t, w1, w3, w2, t, tf, rows_per_group, n_valid):
    m, d = x.shape
    nt = m // t
    ne, _, ff = w1.shape
    assert ne == N_EXPERTS
    tri = jnp.asarray(np.triu(np.ones((t, t), np.float32)), BF16)
    cast = w1.dtype == F32
    in_specs = [pl.BlockSpec((t, d), lambda i, s: (i, 0)),
                _mod_spec(sc, t, rows_per_group), _mod_spec(sh, t, rows_per_group),
                pl.BlockSpec((1, d), lambda i, s: (0, 0)),
                pl.BlockSpec((d, 128), lambda i, s: (0, 0)),
                pl.BlockSpec((t, t), lambda i, s: (0, 0))]
    out_specs = [pl.BlockSpec((t, d), lambda i, s: (i, 0)),
                 pl.BlockSpec((None, ne, t), lambda i, s: (i, 0, 0)),
                 pl.BlockSpec((None, ne, t), lambda i, s: (i, 0, 0)),
                 pl.BlockSpec((None, ne, 128), lambda i, s: (i, 0, 0))]
    out_shape = [jax.ShapeDtypeStruct((m, d), BF16),
                 jax.ShapeDtypeStruct((nt, ne, t), F32),
                 jax.ShapeDtypeStruct((nt, ne, t), F32),
                 jax.ShapeDtypeStruct((nt, ne, 128), F32)]
    weights = ()
    if cast:
        assert d % (16 * nt) == 0 and ff % (16 * nt) == 0
        slabs = [pl.BlockSpec((None, d // nt, ff), lambda i, s: (s, i, 0)),
                 pl.BlockSpec((None, d // nt, ff), lambda i, s: (s, i, 0)),
                 pl.BlockSpec((None, ff // nt, d), lambda i, s: (s, i, 0))]
        in_specs += slabs
        out_specs += slabs
        out_shape += [jax.ShapeDtypeStruct(w.shape, BF16) for w in (w1, w3, w2)]
        weights = (w1, w3, w2)
    res = pl.pallas_call(
        functools.partial(_route_kernel, n_valid),
        grid=(nt, ne if cast else 1),
        in_specs=in_specs,
        out_specs=out_specs,
        out_shape=out_shape,
        compiler_params=_cp(("arbitrary", "arbitrary")),
        name="moe_route",
    )(x, sc, sh, g, router_t, tri, *weights)
    h, rank, gate, cnt = res[:4]
    if cast:
        w1, w3, w2 = res[4:]
    nblk = cnt[:, :, 0].astype(jnp.int32).reshape(nt * ne)
    rank = rank.reshape(nt, ne, 1, t)
    gate = gate.reshape(nt, ne, 1, t)
    out = pl.pallas_call(
        _moe_kernel,
        grid_spec=pltpu.PrefetchScalarGridSpec(
            num_scalar_prefetch=1,
            grid=(nt, ne, ff // tf),
            in_specs=[pl.BlockSpec((t, d), lambda i, e, j, nb: (i, 0)),
                      pl.BlockSpec((None, None, 1, t), lambda i, e, j, nb: (i, e, 0, 0)),
                      pl.BlockSpec((None, None, 1, t), lambda i, e, j, nb: (i, e, 0, 0)),
                      pl.BlockSpec((t, d), lambda i, e, j, nb: (i, 0)),
                      _mod_spec(gate2, t, rows_per_group),
                      pl.BlockSpec((None, d, tf), lambda i, e, j, nb: (e, 0, j)),
                      pl.BlockSpec((None, d, tf), lambda i, e, j, nb: (e, 0, j)),
                      pl.BlockSpec((None, tf, d), lambda i, e, j, nb: (e, j, 0))],
            out_specs=pl.BlockSpec((t, d), lambda i, e, j, nb: (i, 0)),
            scratch_shapes=[pltpu.VMEM((t, d), BF16), pltpu.VMEM((t, d), F32)]),
        out_shape=jax.ShapeDtypeStruct((m, d), F32),
        compiler_params=_cp(("arbitrary", "arbitrary", "arbitrary")),
        name="moe_ffn",
    )(nblk, h, rank, gate, x, gate2, w1, w3, w2)
    return out, (w1, w3, w2)


def _conv_kernel(a_ref, g_ref, w_ref, b_ref, lg_ref, lb_ref, o_ref, cn_ref, ext_scr, sh_scr):
    tc = a_ref.shape[0]
    halo = 32
    off = halo - (CONV_K - 1)

    @pl.when(pl.program_id(1) == 0)
    def _():
        ext_scr[0:halo, :] = jnp.zeros((halo, GW), F32)

    ext_scr[halo:halo + tc, :] = a_ref[...] * jax.nn.sigmoid(g_ref[...])
    for s in range(1, 8):
        sh_scr[s - 1] = ext_scr[s:s + tc + halo - 8, :]
    rc = 64
    for c in range(tc // rc):
        acc = jnp.zeros((rc, GW), F32)
        for k in range(CONV_K):
            s, r0 = (k + off) % 8, c * rc + (k + off) // 8 * 8
            rows = ext_scr[r0:r0 + rc, :] if s == 0 else sh_scr[s - 1, r0:r0 + rc, :]
            acc = acc + w_ref[k:k + 1, :] * rows
        y = acc + b_ref[...]
        mu = jnp.mean(y, axis=-1, keepdims=True)
        yc = y - mu
        var = jnp.mean(yc * yc, axis=-1, keepdims=True)
        o_ref[c * rc:(c + 1) * rc, :] = _silu(yc * lax.rsqrt(var + EPS) * lg_ref[...] + lb_ref[...]).astype(o_ref.dtype)
    cn_ref[...] = ext_scr[tc + off:tc + halo, :]
    ext_scr[0:halo, :] = ext_scr[tc:tc + halo, :]


def conv_prompt(z, nb, l, conv_w, conv_b, ln_g, ln_b, tc):
    m = z.shape[0]
    nl = l // tc
    row = lambda v: v.reshape(1, GW)
    return pl.pallas_call(
        _conv_kernel,
        grid=(nb, nl),
        in_specs=[pl.BlockSpec((tc, GW), lambda b, i: (b * nl + i, G_CA)),
                  pl.BlockSpec((tc, GW), lambda b, i: (b * nl + i, G_CG)),
                  pl.BlockSpec((CONV_K, GW), lambda b, i: (0, 0)),
                  pl.BlockSpec((1, GW), lambda b, i: (0, 0)),
                  pl.BlockSpec((1, GW), lambda b, i: (0, 0)),
                  pl.BlockSpec((1, GW), lambda b, i: (0, 0))],
        out_specs=[pl.BlockSpec((tc, GW), lambda b, i: (b * nl + i, 0)),
                   pl.BlockSpec((None, CONV_K - 1, GW), lambda b, i: (b, 0, 0))],
        out_shape=[jax.ShapeDtypeStruct((m, GW), BF16),
                   jax.ShapeDtypeStruct((nb, CONV_K - 1, GW), F32)],
        scratch_shapes=[pltpu.VMEM((tc + 32, GW), F32), pltpu.VMEM((7, tc + 24, GW), F32)],
        compiler_params=_cp(("arbitrary", "arbitrary")),
        name="conv_prompt",
    )(z, z, conv_w, row(conv_b), row(ln_g), row(ln_b))


def _fox_prep_kernel(q_ref, k_ref, v_ref, t_ref, gq_ref, gk_ref, fb_ref, hs_ref, tri_ref,
                     k_all, v_all, lf_all, qb_ref, kb_ref, vb_ref, kt_ref, vt_ref, lft_ref, fc_ref, carry_scr):
    del k_all, v_all, lf_all

    @pl.when(pl.program_id(1) == 0)
    def _():
        carry_scr[...] = jnp.zeros_like(carry_scr)

    hs = hs_ref[...]
    qn = _head_rms(q_ref[...], hs, gq_ref[...])
    kn = _head_rms(k_ref[...], hs, gk_ref[...])
    v = v_ref[...]
    vt = v.T
    qb_ref[...] = (qn * (HD ** -0.5)).astype(BF16)
    kb_ref[...] = kn.astype(BF16)
    vb_ref[...] = vt.astype(BF16)
    kt_ref[...] = kn.T
    vt_ref[...] = vt
    lf = _log_sigmoid(t_ref[...] + fb_ref[...])
    lft_ref[...] = lf.T[:NH]
    cum = _dot_sel(tri_ref[...], lf, "b") + carry_scr[...]
    fc_ref[...] = cum
    carry_scr[...] = cum[cum.shape[0] - 1:, :]


def _fox_flash_kernel(q_ref, k_ref, vt_ref, fc_ref, o_ref, ot_scr):
    tq = q_ref.shape[0]
    tk = tq
    qi = pl.program_id(1)
    q = q_ref[...]
    head = _head_ids((1, GW), 1)
    kpos = lax.broadcasted_iota(jnp.int32, (tk, tq), 0)
    qpos = lax.broadcasted_iota(jnp.int32, (tk, tq), 1)
    qhs = [jnp.where(head == h, q, jnp.zeros_like(q)) for h in range(NH)]

    def step(jb, carry, diagonal):
        k0 = pl.multiple_of(jb * tk, tk)
        kb = k_ref[pl.ds(k0, tk), :]
        fcb = fc_ref[pl.ds(k0, tk), :]
        out = []
        for h in range(NH):
            m, l, acc = carry[h]
            s = _dot_nt(kb, qhs[h]) - fcb[:, h:h + 1]
            if diagonal:
                s = jnp.where(kpos <= qpos, s, NEG)
            m_new = jnp.maximum(m, jnp.max(s, axis=0, keepdims=True))
            alpha = jnp.exp(m - m_new)
            p = jnp.exp(s - m_new)
            l = alpha * l + jnp.sum(p, axis=0, keepdims=True)
            acc = alpha * acc + _dot(vt_ref[jb, h * HD:(h + 1) * HD, :], p.astype(BF16))
            out.append((m_new, l, acc))
        return tuple(out)

    init = tuple((jnp.full((1, tq), NEG, F32), jnp.zeros((1, tq), F32), jnp.zeros((HD, tq), F32)) for _ in range(NH))
    carry = lax.fori_loop(0, qi, functools.partial(step, diagonal=False), init)
    carry = step(qi, carry, True)
    for h in range(NH):
        _, l, acc = carry[h]
        ot_scr[h * HD:(h + 1) * HD, :] = acc * (1.0 / l)
    o_ref[...] = ot_scr[...].T.astype(o_ref.dtype)


def fox_prompt(z, nb, l, gq, gk, fb, tq, li, k_all, v_all, lf_all):
    m = z.shape[0]
    tp = tq
    nl = l // tp
    hs = jnp.asarray(_np_hsum())
    tri = jnp.asarray(np.tril(np.ones((tp, tp), np.float32)))
    row = lambda v: jnp.tile(v, NH).reshape(1, GW)
    fbp = jnp.zeros((1, TAIL_W), F32).at[0, :NH].set(fb)
    blk = lambda g: pl.BlockSpec((tp, GW), lambda b, i: (b * nl + i, g))
    oblk = lambda w: pl.BlockSpec((tp, w), lambda b, i: (b * nl + i, 0))
    anyspec = pl.BlockSpec(memory_space=pl.ANY)
    stacked = lambda r: pl.BlockSpec((None, None, r, tp), lambda b, i: (b, li, 0, i))
    qb, kb, vb, k_all, v_all, lf_all, fc = pl.pallas_call(
        _fox_prep_kernel,
        grid=(nb, nl),
        in_specs=[blk(G_FQ), blk(G_FK), blk(G_FV),
                  pl.BlockSpec((tp, TAIL_W), lambda b, i: (b * nl + i, TAIL_BLK)),
                  pl.BlockSpec((1, GW), lambda b, i: (0, 0)),
                  pl.BlockSpec((1, GW), lambda b, i: (0, 0)),
                  pl.BlockSpec((1, TAIL_W), lambda b, i: (0, 0)),
                  pl.BlockSpec((GW, GW), lambda b, i: (0, 0)),
                  pl.BlockSpec((tp, tp), lambda b, i: (0, 0)),
                  anyspec, anyspec, anyspec],
        out_specs=[oblk(GW)] * 2 + [pl.BlockSpec((None, None, GW, tp), lambda b, i: (b, i, 0, 0)),
                                    stacked(GW), stacked(GW), stacked(NH), oblk(TAIL_W)],
        out_shape=[jax.ShapeDtypeStruct((m, GW), BF16)] * 2 + [jax.ShapeDtypeStruct((nb, nl, GW, tp), BF16),
                   jax.ShapeDtypeStruct(k_all.shape, F32), jax.ShapeDtypeStruct(v_all.shape, F32),
                   jax.ShapeDtypeStruct(lf_all.shape, F32), jax.ShapeDtypeStruct((m, TAIL_W), F32)],
        input_output_aliases={9: 3, 10: 4, 11: 5},
        scratch_shapes=[pltpu.VMEM((1, TAIL_W), F32)],
        compiler_params=_cp(("arbitrary", "arbitrary")),
        name="fox_prep",
    )(z, z, z, z, row(gq), row(gk), fbp, hs, tri, k_all, v_all, lf_all)
    nq = l // tq
    o = pl.pallas_call(
        _fox_flash_kernel,
        grid=(nb, nq),
        in_specs=[pl.BlockSpec((tq, GW), lambda b, i: (b * nq + i, 0)),
                  pl.BlockSpec((l, GW), lambda b, i: (b, 0)),
                  pl.BlockSpec((None, nq, GW, tq), lambda b, i: (b, 0, 0, 0)),
                  pl.BlockSpec((l, TAIL_W), lambda b, i: (b, 0))],
        out_specs=pl.BlockSpec((tq, GW), lambda b, i: (b * nq + i, 0)),
        out_shape=jax.ShapeDtypeStruct((m, GW), BF16),
        scratch_shapes=[pltpu.VMEM((GW, tq), F32)],
        compiler_params=_cp(("arbitrary", "arbitrary")),
        name="fox_flash",
    )(qb, kb, vb, fc)
    return o, k_all, v_all, lf_all


GLA_SAFE_LOG = -80.0


def _gla_kernel(q_ref, k_ref, v_ref, t_ref, go_ref, w2_ref, bg_ref, gn_ref, hs_ref, cs_ref, bd_ref,
                o_ref, st_ref, st_scr, x_scr, o_scr):
    tg = q_ref.shape[0]
    c = GLA_CHUNK

    @pl.when(pl.program_id(1) == 0)
    def _():
        st_scr[...] = jnp.zeros_like(st_scr)

    hs = hs_ref[...]
    bd = bd_ref[...]
    glog = _log_sigmoid(_dot(t_ref[...].astype(BF16), w2_ref[...]) + bg_ref[...]) * (1.0 / GLA_TAU)
    q = q_ref[...] * (HD ** -0.5)
    k = k_ref[...]
    v = v_ref[...]
    vb = v.astype(BF16)
    lower = lax.broadcasted_iota(jnp.int32, (tg, tg), 0) >= lax.broadcasted_iota(jnp.int32, (tg, tg), 1)
    bfull = _dot_sel(jnp.where(lower, 1.0, 0.0), glog, "b")
    safe = jnp.min(bfull) >= GLA_SAFE_LOG

    @pl.when(safe)
    def _():
        head = _head_ids((1, GW), 1)
        btot = bfull[tg - 1:tg]
        qe = (q * jnp.exp(bfull)).astype(BF16)
        kinv = (k * jnp.exp(-bfull)).astype(BF16)
        kend = (k * jnp.exp(btot - bfull)).astype(BF16)
        o = _dot_nt(qe, st_scr[...].astype(BF16))
        for h in range(NH):
            s = _dot_nt(jnp.where(head == h, qe, jnp.zeros_like(qe)), kinv)
            o = o + jnp.where(head == h, _dot(jnp.where(lower, s, 0.0).astype(BF16), vb), 0.0)
        o_scr[...] = o
        st_scr[...] = st_scr[...] * jnp.exp(btot) + _dot_tn(vb, kend) * bd

    @pl.when(jnp.logical_not(safe))
    def _():
        hs_b = hs.astype(BF16)
        cs = cs_ref[...]
        bcum = _dot_sel(jnp.where(lower, cs, 0.0), glog, "b")
        blast = _dot_sel(cs, glog, "b")
        qe = (q * jnp.exp(bcum)).astype(BF16)
        ke = (k * jnp.exp(blast - bcum)).astype(BF16)
        ii = lax.broadcasted_iota(jnp.int32, (c, GW), 0)
        for n in range(tg // c):
            r0 = n * c
            bc = bcum[r0:r0 + c]
            qc = q[r0:r0 + c]
            kc = k[r0:r0 + c]
            vc = v[r0:r0 + c]
            for j in range(c):
                ex = jnp.exp(jnp.where(ii >= j, bc - bc[j:j + 1], -jnp.inf))
                x_scr[j * c:(j + 1) * c, :] = (qc * ex * kc[j:j + 1]).astype(BF16)
            att = _dot(x_scr[...], hs_b)
            o = _dot_nt(qe[r0:r0 + c], st_scr[...].astype(BF16))
            for j in range(c):
                o = o + att[j * c:(j + 1) * c] * vc[j:j + 1]
            o_scr[r0:r0 + c, :] = o
            kv = _dot_tn(vb[r0:r0 + c], ke[r0:r0 + c])
            st_scr[...] = st_scr[...] * jnp.exp(blast[r0:r0 + 1]) + kv * bd

    o_ref[...] = (_head_rms(o_scr[...], hs, gn_ref[...]) * _silu(go_ref[...])).astype(o_ref.dtype)
    st_ref[...] = st_scr[...]


def gla_prompt(z, nb, l, w_gate2, b_gate, norm_g, tg):
    m = z.shape[0]
    nl = l // tg
    hs = jnp.asarray(_np_hsum())
    ch = np.arange(tg) // GLA_CHUNK
    cs = jnp.asarray((ch[:, None] == ch[None, :]).astype(np.float32))
    w2p = jnp.zeros((TAIL_W, GW), F32).at[NH:NH + GLA_LR].set(w_gate2).astype(BF16)
    blk = lambda g: pl.BlockSpec((tg, GW), lambda b, i: (b * nl + i, g))
    full = lambda r, c: pl.BlockSpec((r, c), lambda b, i: (0, 0))
    return pl.pallas_call(
        _gla_kernel,
        grid=(nb, nl),
        in_specs=[blk(G_GQ), blk(G_GK), blk(G_GV),
                  pl.BlockSpec((tg, TAIL_W), lambda b, i: (b * nl + i, TAIL_BLK)),
                  blk(G_GO), full(TAIL_W, GW), full(1, GW), full(1, GW), full(GW, GW), full(tg, tg), full(GW, GW)],
        out_specs=[pl.BlockSpec((tg, GW), lambda b, i: (b * nl + i, 0)),
                   pl.BlockSpec((None, GW, GW), lambda b, i: (b, 0, 0))],
        out_shape=[jax.ShapeDtypeStruct((m, GW), BF16), jax.ShapeDtypeStruct((nb, GW, GW), F32)],
        scratch_shapes=[pltpu.VMEM((GW, GW), F32), pltpu.VMEM((GLA_CHUNK * GLA_CHUNK, GW), BF16),
                        pltpu.VMEM((tg, GW), F32)],
        compiler_params=_cp(("arbitrary", "arbitrary")),
        name="gla_prompt",
    )(z, z, z, z, z, w2p, b_gate.reshape(1, GW), norm_g.reshape(1, GW), hs, cs, hs)


def _ret_lg_row():
    lg = np.log(1.0 - np.exp2(-5.0 - np.arange(NH, dtype=np.float32))).astype(np.float32)
    return np.repeat(lg, HD).reshape(1, GW)


def _ret_decay_mask(c):
    lg = _ret_lg_row()[0, ::HD]
    i = np.arange(c, dtype=np.float32)
    rel = i[:, None] - i[None, :]
    return np.where(rel >= 0, np.exp(np.maximum(rel, 0.0)[None] * lg[:, None, None]), 0.0).astype(np.float32)


def _rope(x, cos, sin_signed):
    first = (lax.broadcasted_iota(jnp.int32, (1, GW), 1) & (HD - 1)) < (HD // 2)
    swapped = jnp.where(first, pltpu.roll(x, GW - HD // 2, axis=1), pltpu.roll(x, HD // 2, axis=1))
    return x * cos + swapped * sin_signed


def _ret_kernel(q_ref, k_ref, v_ref, g_ref, cos_ref, sin_ref, lg_ref, gn_ref, hs_ref, dm_ref, o_ref, st_ref, st_scr):
    tr = dm_ref.shape[1]

    @pl.when(pl.program_id(1) == 0)
    def _():
        st_scr[...] = jnp.zeros_like(st_scr)

    hs = hs_ref[...]
    lg = lg_ref[...]
    head = _head_ids((1, GW), 1)
    ri = lax.broadcasted_iota(jnp.int32, (tr, 1), 0).astype(F32)
    dq = jnp.exp((ri + 1.0) * lg)
    dk = jnp.exp((tr - 1.0 - ri) * lg)
    ds = jnp.exp(tr * lg)
    for n in range(q_ref.shape[0] // tr):
        rows = slice(n * tr, (n + 1) * tr)
        cos = cos_ref[rows, :]
        sin = sin_ref[rows, :]
        q = _rope(q_ref[rows, :], cos, sin)
        k = _rope(k_ref[rows, :], cos, sin) * (HD ** -0.5)
        qb = q.astype(BF16)
        kb = k.astype(BF16)
        vb = v_ref[rows, :].astype(BF16)
        o = _dot_nt((q * dq).astype(BF16), st_scr[...].astype(BF16))
        for h in range(NH):
            att = _dot_nt(jnp.where(head == h, qb, jnp.zeros_like(qb)), kb) * dm_ref[h]
            o = o + jnp.where(head == h, _dot(att.astype(BF16), vb), 0.0)
        st_scr[...] = st_scr[...] * ds + _dot_tn(vb, (k * dk).astype(BF16)) * hs
        o_ref[rows, :] = (_head_rms(o, hs, gn_ref[...]) * _silu(g_ref[rows, :])).astype(o_ref.dtype)
    st_ref[...] = st_scr[...]


def _rope_tables(pos):
    half = HD // 2
    inv = 10000.0 ** (-jnp.arange(half, dtype=F32) / half)
    ang = pos[:, None] * inv[None, :]
    cos = jnp.cos(ang)
    sin = jnp.sin(ang)
    cos_t = jnp.tile(jnp.concatenate([cos, cos], axis=1), (1, NH))
    sin_t = jnp.tile(jnp.concatenate([-sin, sin], axis=1), (1, NH))
    return cos_t, sin_t


def ret_prompt(z, nb, l, norm_g, tt):
    m = z.shape[0]
    nl = l // tt
    tr = RET_CHUNK
    hs = jnp.asarray(_np_hsum())
    cos_t, sin_t = _rope_tables(jnp.arange(l, dtype=F32))
    blk = lambda g: pl.BlockSpec((tt, GW), lambda b, i: (b * nl + i, g))
    full = lambda r, c: pl.BlockSpec((r, c), lambda b, i: (0, 0))
    tab = pl.BlockSpec((tt, GW), lambda b, i: (i, 0))
    return pl.pallas_call(
        _ret_kernel,
        grid=(nb, nl),
        in_specs=[blk(G_RQ), blk(G_RK), blk(G_RV), blk(G_RG), tab, tab, full(1, GW), full(1, GW), full(GW, GW),
                  pl.BlockSpec((NH, tr, tr), lambda b, i: (0, 0, 0))],
        out_specs=[pl.BlockSpec((tt, GW), lambda b, i: (b * nl + i, 0)),
                   pl.BlockSpec((None, GW, GW), lambda b, i: (b, 0, 0))],
        out_shape=[jax.ShapeDtypeStruct((m, GW), BF16), jax.ShapeDtypeStruct((nb, GW, GW), F32)],
        scratch_shapes=[pltpu.VMEM((GW, GW), F32)],
        compiler_params=_cp(("arbitrary", "arbitrary")),
        name="ret_prompt",
    )(z, z, z, z, cos_t, sin_t, jnp.asarray(_ret_lg_row()), norm_g.reshape(1, GW), hs, jnp.asarray(_ret_decay_mask(tr)))


def _state_from_blockdiag(st):
    nb = st.shape[0]
    s5 = st.reshape(nb, NH, HD, NH, HD)
    diag = jnp.stack([s5[:, h, :, h, :] for h in range(NH)], axis=1)
    return diag.transpose(0, 1, 3, 2)


def _sample_rows_kernel(z_ref, buf_ref, cw_ref, cb_ref, lg_ref, lb_ref, gq_ref, gk_ref, fb_ref, w2_ref, bg_ref,
                        cos_ref, sin_ref, rlg_ref, hs_ref,
                        oa_ref, cn_ref, fq_ref, fk_ref, lf_ref, gq_o, ga_o, rq_o, rk_o, ra_o):
    grp = lambda g: z_ref[:, g * GW:(g + 1) * GW]
    tail = z_ref[:, N_MAIN:N_MAIN + TAIL_W]
    hs = hs_ref[...]
    u = grp(G_CA) * jax.nn.sigmoid(grp(G_CG))
    y = cw_ref[CONV_K - 1:CONV_K, :] * u + cb_ref[...]
    for k in range(CONV_K - 1):
        y = y + cw_ref[k:k + 1, :] * buf_ref[k]
    mu = jnp.mean(y, axis=-1, keepdims=True)
    yc = y - mu
    var = jnp.mean(yc * yc, axis=-1, keepdims=True)
    oa_ref[...] = _silu(yc * lax.rsqrt(var + EPS) * lg_ref[...] + lb_ref[...]).astype(oa_ref.dtype)
    for k in range(CONV_K - 2):
        cn_ref[k] = buf_ref[k + 1]
    cn_ref[CONV_K - 2] = u
    fq_ref[...] = _head_rms(grp(G_FQ), hs, gq_ref[...]) * (HD ** -0.5)
    fk_ref[...] = _head_rms(grp(G_FK), hs, gk_ref[...])
    lf_ref[...] = _log_sigmoid(tail + fb_ref[...])
    glog = _log_sigmoid(_wdot(tail, w2_ref[...]) + bg_ref[...]) * (1.0 / GLA_TAU)
    gq_o[...] = grp(G_GQ) * (HD ** -0.5)
    ga_o[...] = jnp.exp(glog)
    rq_o[...] = _rope(grp(G_RQ), cos_ref[...], sin_ref[...])
    rk_o[...] = _rope(grp(G_RK), cos_ref[...], sin_ref[...]) * (HD ** -0.5)
    ra_o[...] = jnp.broadcast_to(jnp.exp(rlg_ref[...]), ra_o.shape)


def _rec_step_kernel(q_ref, k_ref, a_ref, v_ref, s_ref, o_ref, sn_ref):
    for b in range(q_ref.shape[0]):
        for h in range(NH):
            q = q_ref[b, h]
            k = k_ref[b, h]
            a = a_ref[b, h]
            v = v_ref[b, h]
            s = s_ref[b, h]
            qk = jnp.sum(q * k, axis=0, keepdims=True)
            o_ref[b, h] = qk * v + jnp.sum((q * a) * s, axis=0, keepdims=True)
            sn_ref[b, h] = a * s + k * v


REC_SEQS = 8


def rec_step(q, k, a, v, state, li):
    nb = q.shape[0]
    bt = math.gcd(nb, REC_SEQS)
    col = lambda x: x.reshape(nb, NH, HD, 1)
    cspec = pl.BlockSpec((bt, NH, HD, 1), lambda b: (b, 0, 0, 0))
    rspec = pl.BlockSpec((bt, NH, 1, HD), lambda b: (b, 0, 0, 0))
    o, sn = pl.pallas_call(
        _rec_step_kernel,
        grid=(nb // bt,),
        in_specs=[cspec, cspec, cspec, rspec,
                  pl.BlockSpec((None, bt, NH, HD, HD), lambda b: (li, b, 0, 0, 0))],
        out_specs=[rspec, pl.BlockSpec((bt, NH, HD, HD), lambda b: (b, 0, 0, 0))],
        out_shape=[jax.ShapeDtypeStruct((nb, NH, 1, HD), F32), jax.ShapeDtypeStruct((nb, NH, HD, HD), F32)],
        compiler_params=_cp(("arbitrary",)),
        name="rec_step",
    )(col(q), col(k), col(a), v.reshape(nb, NH, 1, HD), state)
    return o.reshape(nb, GW), sn


FOX_PAGES = 64


def _fox_bias_kernel(pt_ref, lfn_ref, lf_ref, ts_ref, pre_ref, o_ref, lf_scr):
    b = pl.program_id(0)
    n_pages = o_ref.shape[1]
    pg = o_ref.shape[2]
    for p in range(n_pages):
        page = lf_ref[pt_ref[b * n_pages + p]]
        for h in range(NH):
            lf_scr[h * n_pages + p:h * n_pages + p + 1, :] = page[h:h + 1, :]
    both = _dot_sel(lf_scr[...], ts_ref[...], "a")
    suf = both[:, :pg]
    tot = both[:, pg:]
    later = _dot_sel(pre_ref[...], tot, "b")
    for h in range(NH):
        r0, r1 = h * n_pages, (h + 1) * n_pages
        o_ref[h] = lfn_ref[h:h + 1, :] + later[r0:r1] + suf[r0:r1]


def _fox_sample_kernel(pt_ref, q_ref, kn_ref, vn_ref, bias_ref, *rest):
    g_n = bias_ref.shape[1]
    k_refs = rest[:g_n]
    v_refs = rest[g_n:2 * g_n]
    o_ref, m_scr, l_scr, acc_scr, s_scr = rest[2 * g_n:]
    c = pl.program_id(1)
    pg = k_refs[0].shape[1]

    @pl.when(c == 0)
    def _():
        m_scr[...] = jnp.full(m_scr.shape, NEG, F32)
        l_scr[...] = jnp.zeros_like(l_scr)
        acc_scr[...] = jnp.zeros_like(acc_scr)

    qb = jnp.broadcast_to(q_ref[...], (GW, pg))
    for g in range(g_n):
        prod = k_refs[g][...] * qb
        for h in range(NH):
            s_scr[h * g_n + g:h * g_n + g + 1, :] = jnp.sum(prod[h * HD:(h + 1) * HD], axis=0, keepdims=True)
    for h in range(NH):
        s_h = s_scr[h * g_n:(h + 1) * g_n, :] + bias_ref[h]
        m_old = m_scr[h:h + 1, :]
        m_new = jnp.maximum(m_old, jnp.max(jnp.max(s_h, axis=0, keepdims=True), axis=1, keepdims=True))
        alpha = jnp.exp(m_old - m_new)
        p = jnp.exp(s_h - m_new)
        m_scr[h:h + 1, :] = m_new
        l_scr[h:h + 1, :] = l_scr[h:h + 1, :] * alpha + jnp.sum(p, axis=0, keepdims=True)
        acc = acc_scr[h * HD:(h + 1) * HD, :] * alpha
        for g in range(g_n):
            acc = acc + p[g:g + 1, :] * v_refs[g][h * HD:(h + 1) * HD, :]
        acc_scr[h * HD:(h + 1) * HD, :] = acc

    @pl.when(c == pl.num_programs(1) - 1)
    def _():
        prod = q_ref[...] * kn_ref[...]
        for h in range(NH):
            s_self = jnp.sum(prod[h * HD:(h + 1) * HD], axis=0, keepdims=True)
            m_h = m_scr[h:h + 1, 0:1]
            m_fin = jnp.maximum(m_h, s_self)
            a_h = jnp.exp(m_h - m_fin)
            p_self = jnp.exp(s_self - m_fin)
            l_tot = jnp.sum(l_scr[h:h + 1, :], axis=1, keepdims=True) * a_h + p_self
            num = jnp.sum(acc_scr[h * HD:(h + 1) * HD, :], axis=1, keepdims=True) * a_h \
                + p_self * vn_ref[h * HD:(h + 1) * HD, :]
            o_ref[h * HD:(h + 1) * HD, :] = num / l_tot


def fox_sample(q, k_new, v_new, lf_new, cache_kt, cache_vt, cache_lf, page_table, li):
    nb, n_pages = page_table.shape
    n_phys = cache_lf.shape[0]
    pg = cache_kt.shape[-1]
    g_n = math.gcd(FOX_PAGES, n_pages)
    nc = n_pages // g_n
    col = lambda x: x.reshape(nb, GW, 1)
    pt = page_table.reshape(-1)
    lfn = jnp.broadcast_to(lf_new[:, :, None], (nb, NH, pg))
    t = np.arange(pg)
    ts = jnp.asarray(np.concatenate([(t[:, None] > t[None, :]).astype(np.float32), np.ones((pg, pg), np.float32)], axis=1))
    r = np.arange(NH * n_pages)
    later = jnp.asarray(((r[:, None] // n_pages == r[None, :] // n_pages) & (r[None, :] > r[:, None])).astype(np.float32))
    bias = pl.pallas_call(
        _fox_bias_kernel,
        grid_spec=pltpu.PrefetchScalarGridSpec(
            num_scalar_prefetch=1,
            grid=(nb,),
            in_specs=[pl.BlockSpec((None, NH, pg), lambda b, pt: (b, 0, 0)),
                      pl.BlockSpec((n_phys, None, NH, pg), lambda b, pt: (0, li, 0, 0)),
                      pl.BlockSpec((pg, 2 * pg), lambda b, pt: (0, 0)),
                      pl.BlockSpec((NH * n_pages, NH * n_pages), lambda b, pt: (0, 0))],
            out_specs=pl.BlockSpec((None, NH, n_pages, pg), lambda b, pt: (b, 0, 0, 0)),
            scratch_shapes=[pltpu.VMEM((NH * n_pages, pg), F32)]),
        out_shape=jax.ShapeDtypeStruct((nb, NH, n_pages, pg), F32),
        compiler_params=_cp(("arbitrary",)),
        name="fox_bias",
    )(pt, lfn, cache_lf, ts, later)

    def page(g):
        return lambda b, c, pt: (pt[b * n_pages + c * g_n + g], li, 0, 0)

    cspec = pl.BlockSpec((None, GW, 1), lambda b, c, pt: (b, 0, 0))
    in_specs = [cspec, cspec, cspec, pl.BlockSpec((None, NH, g_n, pg), lambda b, c, pt: (b, 0, c, 0))]
    in_specs += [pl.BlockSpec((None, None, GW, pg), page(g)) for g in range(g_n)]
    in_specs += [pl.BlockSpec((None, None, GW, pg), page(g)) for g in range(g_n)]
    o = pl.pallas_call(
        _fox_sample_kernel,
        grid_spec=pltpu.PrefetchScalarGridSpec(
            num_scalar_prefetch=1,
            grid=(nb, nc),
            in_specs=in_specs,
            out_specs=pl.BlockSpec((None, GW, 1), lambda b, c, pt: (b, 0, 0)),
            scratch_shapes=[pltpu.VMEM((NH, pg), F32), pltpu.VMEM((NH, pg), F32),
                            pltpu.VMEM((GW, pg), F32), pltpu.VMEM((NH * g_n, pg), F32)]),
        out_shape=jax.ShapeDtypeStruct((nb, GW, 1), F32),
        compiler_params=_cp(("arbitrary", "arbitrary")),
        name="fox_sample",
    )(pt, col(q), col(k_new), col(v_new), bias, *([cache_kt] * g_n), *([cache_vt] * g_n))
    return o.reshape(nb, GW)


def _sample_out_kernel(oa_ref, ob_ref, oc_ref, od_ref, go_ref, rg_ref, gng_ref, rng_ref, hs_ref, w_ref, x_ref, g1_ref, o_ref):
    hs = hs_ref[...]
    oc = _head_rms(oc_ref[...], hs, gng_ref[...]) * _silu(go_ref[...])
    od = _head_rms(od_ref[...], hs, rng_ref[...]) * _silu(rg_ref[...])
    cat = jnp.concatenate([oa_ref[...], ob_ref[...], oc, od], axis=1)
    o_ref[...] = x_ref[...] + g1_ref[...] * _wdot(cat, w_ref[...])


def _prep_w_in(w_in, li):
    wt = w_in.transpose(2, 0, 1)[:, li, :]
    o_ff = 5 * GW
    o_lr = o_ff + NH + 3 * GW
    return jnp.concatenate([wt[:o_ff], wt[o_ff + NH:o_lr], wt[o_lr + GLA_LR:],
                            wt[o_ff:o_ff + NH], wt[o_lr:o_lr + GLA_LR],
                            jnp.zeros((TAIL_W - NH - GLA_LR, wt.shape[1]), wt.dtype)], axis=0)


def _ffn_apply(li, x, sc, sh, g, gate, fw, tm, rows_per_group, n_valid):
    if li % 2 == 0:
        w1, w3, w2 = fw
        return ffn_dense(x, sc, sh, g, gate, w1, w3, w2, tm, w1.shape[1] // 2, rows_per_group), fw
    router_t, w1, w3, w2 = fw
    out, wb = moe_ffn(x, sc, sh, g, gate, router_t, w1, w3, w2, tm, w1.shape[2] // 2, rows_per_group, n_valid)
    return out, (router_t,) + wb


def kernel(x_prompt, x_sample, c_prompt, c_sample, state_conv, cache_k, cache_v, cache_logf, state_gla, state_ret, page_table, w_in, w_out, conv_w, conv_b, conv_ln_g, conv_ln_b, fox_qn_g, fox_kn_g, fox_fb, gla_w_gate2, gla_b_gate, gla_norm_g, ret_norm_g, norm1_g, norm2_g, w_ada, b_ada, ffn_w1, ffn_w3, ffn_w2, moe_router, moe_w1, moe_w3, moe_w2):
    nbp, l, d = x_prompt.shape
    nbs = x_sample.shape[0]
    depth = w_in.shape[0]
    n_pages, pg = page_table.shape[1], cache_k.shape[2]
    p_len = n_pages * pg
    mp = nbp * l
    ms = -(-nbs // 8) * 8
    ms_moe = 128

    mod = ada_mod(jnp.concatenate([c_prompt, c_sample], axis=0), w_ada, b_ada)
    mod = mod.reshape(depth, nbp + nbs, 6, d)
    cache_kt = cache_k.transpose(0, 1, 3, 4, 2).reshape(cache_k.shape[0], depth, GW, pg)
    cache_vt = cache_v.transpose(0, 1, 3, 4, 2).reshape(cache_v.shape[0], depth, GW, pg)
    cache_lf = cache_logf.transpose(0, 1, 3, 2)
    conv_state = state_conv.transpose(0, 2, 1, 3)
    cos_s, sin_s = _rope_tables(jnp.full((1,), p_len, F32))
    hs = jnp.asarray(_np_hsum())
    rlg = jnp.asarray(_ret_lg_row())

    xp = x_prompt.reshape(mp, d)
    xs = jnp.zeros((ms, d), F32).at[:nbs].set(x_sample.reshape(nbs, d))
    outs_p, outs_s = [], []
    k_all = jnp.zeros((nbp, depth, GW, l), F32)
    v_all = jnp.zeros((nbp, depth, GW, l), F32)
    lf_all = jnp.zeros((nbp, depth, NH, l), F32)
    for li in range(depth):
        mp_l = [mod[li, :nbp, i].reshape(nbp, 1, d) for i in range(6)]
        ms_l = [jnp.zeros((1, ms, d), F32).at[0, :nbs].set(mod[li, nbp:, i]) for i in range(6)]
        n1 = norm1_g[li].reshape(1, d)
        n2 = norm2_g[li].reshape(1, d)
        w_in_f = _prep_w_in(w_in, li)
        w_in_b = w_in_f.astype(BF16)
        w_out_f = w_out[li]
        w_out_b = w_out_f.astype(BF16)
        if li % 2 == 0:
            fw_s = (ffn_w1[li // 2], ffn_w3[li // 2], ffn_w2[li // 2])
            fw = tuple(w.astype(BF16) for w in fw_s)
        else:
            fw = (jnp.pad(moe_router[li // 2], ((0, 0), (0, 128 - N_EXPERTS))), moe_w1[li // 2], moe_w3[li // 2],
                  moe_w2[li // 2])
            fw_s = None

        sh1, sc1, g1, sh2, sc2, g2 = mp_l
        z = in_proj(xp, sc1, sh1, n1, w_in_b, 512, l)
        out_a, conv_new = conv_prompt(z, nbp, l, conv_w[li], conv_b[li], conv_ln_g[li], conv_ln_b[li], 512)
        out_b, k_all, v_all, lf_all = fox_prompt(z, nbp, l, fox_qn_g[li], fox_kn_g[li], fox_fb[li], 512, li,
                                                 k_all, v_all, lf_all)
        out_c, gla_st = gla_prompt(z, nbp, l, gla_w_gate2[li], gla_b_gate[li], gla_norm_g[li], 256)
        out_d, ret_st = ret_prompt(z, nbp, l, ret_norm_g[li], 512)
        xp = out_proj((out_a, out_b, out_c, out_d), w_out_b, xp, g1, 512, l)
        xp, fw_used = _ffn_apply(li, xp, sc2, sh2, n2, g2, fw, 512 if li % 2 == 0 else 1024, l, mp)
        if fw_s is None:
            fw_s = fw_used
        outs_p.append((conv_new, None, None, None, _state_from_blockdiag(gla_st), _state_from_blockdiag(ret_st)))

        sh1, sc1, g1, sh2, sc2, g2 = ms_l
        zs = in_proj(xs, sc1, sh1, n1, w_in_f, ms, ms)[:nbs]
        row = lambda v_: v_.reshape(1, GW)
        w2p = jnp.zeros((TAIL_W, GW), F32).at[NH:NH + GLA_LR].set(gla_w_gate2[li])
        fbp = jnp.zeros((1, TAIL_W), F32).at[0, :NH].set(fox_fb[li])
        r32 = jax.ShapeDtypeStruct((nbs, GW), F32)
        (oa, cn, fq, fk, lfs, gq, ga, rq, rk, ra) = pl.pallas_call(
            _sample_rows_kernel,
            out_shape=[r32, jax.ShapeDtypeStruct((CONV_K - 1, nbs, GW), F32), r32, r32,
                       jax.ShapeDtypeStruct((nbs, TAIL_W), F32), r32, r32, r32, r32, r32],
            compiler_params=pltpu.CompilerParams(vmem_limit_bytes=VMEM_LIMIT),
            name="sample_rows",
        )(zs, conv_state[li], conv_w[li], row(conv_b[li]), row(conv_ln_g[li]), row(conv_ln_b[li]),
          row(jnp.tile(fox_qn_g[li], NH)), row(jnp.tile(fox_kn_g[li], NH)), fbp, w2p, row(gla_b_gate[li]),
          cos_s, sin_s, rlg, hs)
        grp = lambda g_: zs[:, g_ * GW:(g_ + 1) * GW]
        f_v, g_k, g_v, r_v = grp(G_FV), grp(G_GK), grp(G_GV), grp(G_RV)
        ob = fox_sample(fq, fk, f_v, lfs[:, :NH], cache_kt, cache_vt, cache_lf, page_table, li)
        oc, gla_new = rec_step(gq, g_k, ga, g_v, state_gla, li)
        od, ret_new = rec_step(rq, rk, ra, r_v, state_ret, li)
        pad = lambda v_: jnp.zeros((ms, v_.shape[1]), F32).at[:nbs].set(v_)
        xs = pl.pallas_call(
            _sample_out_kernel,
            out_shape=jax.ShapeDtypeStruct((ms, d), F32),
            compiler_params=pltpu.CompilerParams(vmem_limit_bytes=VMEM_LIMIT),
            name="sample_out",
        )(pad(oa), pad(ob), pad(oc), pad(od), pad(grp(G_GO)), pad(grp(G_RG)), row(gla_norm_g[li]), row(ret_norm_g[li]),
          hs, w_out_f, xs, g1[0])
        if li % 2 == 0:
            xs, _ = _ffn_apply(li, xs, sc2, sh2, n2, g2, fw_s, ms, ms, nbs)
        else:
            wide = lambda a: jnp.zeros(a.shape[:-2] + (ms_moe, d), F32).at[..., :ms, :].set(a)
            xs = _ffn_apply(li, wide(xs), wide(sc2), wide(sh2), n2, wide(g2), fw_s, ms_moe, ms_moe, nbs)[0][:ms]
        outs_s.append((cn.transpose(1, 0, 2), fk.reshape(nbs, 1, NH, HD), f_v.reshape(nbs, 1, NH, HD),
                       lfs[:, :NH].reshape(nbs, 1, NH), gla_new, ret_new))

    st = lambda lst, i, ax: jnp.stack([s[i] for s in lst], axis=ax)
    heads_last = lambda a: a.reshape(nbp, depth, NH, HD, l).transpose(0, 1, 4, 2, 3)
    return (xp.reshape(nbp, l, d), xs[:nbs].reshape(nbs, 1, d),
            st(outs_p, 0, 0), st(outs_s, 0, 0),
            heads_last(k_all), st(outs_s, 1, 1),
            heads_last(v_all), st(outs_s, 2, 1),
            lf_all.transpose(0, 1, 3, 2), st(outs_s, 3, 1),
            st(outs_p, 4, 0), st(outs_s, 4, 0),
            st(outs_p, 5, 0), st(outs_s, 5, 0))
```

```python
import functools
import math

import numpy as np
import jax
import jax.numpy as jnp
from jax import lax
from jax.experimental import pallas as pl
from jax.experimental.pallas import tpu as pltpu

F32 = jnp.float32
BF16 = jnp.bfloat16
HI = lax.Precision.HIGHEST

D_MODEL = 1024
NH = 4
HD = 64
GW = NH * HD
CONV_K = 31
GLA_LR = 16
GLA_TAU = 16.0
GLA_CHUNK = 16
RET_CHUNK = 128
N_EXPERTS = 8
EPS = 1e-6
N_MAIN = 13 * GW
TAIL_W = 128
IN_PAD = N_MAIN + TAIL_W
TAIL_BLK = N_MAIN // TAIL_W
(G_CA, G_CG, G_FQ, G_FK, G_FV, G_GQ, G_GK, G_GV, G_GO, G_RQ, G_RK, G_RV, G_RG) = range(13)
VMEM_LIMIT = 56 * 1024 * 1024
MOE_ROWS = 128
NEG = -1e30
LOG2E = math.log2(math.e)


def _cp(sem, vmem=VMEM_LIMIT):
    return pltpu.CompilerParams(dimension_semantics=sem, vmem_limit_bytes=vmem)


def _silu(x):
    return x * jax.nn.sigmoid(x)


def _log_sigmoid(x):
    return jnp.minimum(x, 0.0) - jnp.log1p(jnp.exp(-jnp.abs(x)))


def _dot(a, b):
    return jnp.dot(a, b, preferred_element_type=F32)


def _split3(x):
    hi = x.astype(BF16)
    r1 = x - hi.astype(F32)
    mid = r1.astype(BF16)
    lo = (r1 - mid.astype(F32)).astype(BF16)
    return hi, mid, lo


def _dot_sel(a, b, data):
    if data == "a":
        sel = b.astype(BF16)
        return sum(_dot(p, sel) for p in _split3(a))
    sel = a.astype(BF16)
    return sum(_dot(sel, p) for p in _split3(b))


def _head_ids(shape, dim):
    return lax.shift_right_logical(lax.broadcasted_iota(jnp.int32, shape, dim), int(math.log2(HD)))


def _dot_nt(a, b, **kw):
    return lax.dot_general(a, b, (((1,), (1,)), ((), ())), preferred_element_type=F32, **kw)


def _dot_tn(a, b, **kw):
    return lax.dot_general(a, b, (((0,), (0,)), ((), ())), preferred_element_type=F32, **kw)


def _wdot(a, w, nt=False):
    if w.dtype == BF16:
        a, kw = a.astype(BF16), {}
    else:
        kw = {"precision": HI}
    return _dot_nt(a, w, **kw) if nt else jnp.dot(a, w, preferred_element_type=F32, **kw)


def _modnorm(x, g, sc, sh):
    y = x * lax.rsqrt(jnp.mean(x * x, axis=-1, keepdims=True) + EPS)
    return (y * g) * (1.0 + sc) + sh


def _head_rms(x, hsum, g):
    ms = _dot_sel(x * x, hsum, "a") * (1.0 / HD)
    return x * lax.rsqrt(ms + EPS) * g


def _np_hsum():
    i = np.arange(GW) // HD
    return (i[:, None] == i[None, :]).astype(np.float32)


def _ada_kernel(c_ref, w_ref, b_ref, o_ref):
    c = c_ref[...]
    o_ref[...] = _wdot(_silu(c), w_ref[...]) + b_ref[...]


def ada_mod(c_all, w_ada, b_ada):
    depth, d, n = w_ada.shape
    r = c_all.shape[0]
    tn = 1536
    return pl.pallas_call(
        _ada_kernel,
        grid=(depth, n // tn),
        in_specs=[pl.BlockSpec((r, d), lambda l, j: (0, 0)),
                  pl.BlockSpec((None, d, tn), lambda l, j: (l, 0, j)),
                  pl.BlockSpec((None, 1, tn), lambda l, j: (l, 0, j))],
        out_specs=pl.BlockSpec((None, r, tn), lambda l, j: (l, 0, j)),
        out_shape=jax.ShapeDtypeStruct((depth, r, n), F32),
        compiler_params=_cp(("arbitrary", "arbitrary")),
        name="ada_mod",
    )(c_all, w_ada, b_ada.reshape(depth, 1, n))


def _in_proj_kernel(x_ref, sc_ref, sh_ref, g_ref, w_ref, z_ref):
    h = _modnorm(x_ref[...], g_ref[...], sc_ref[...], sh_ref[...])
    z_ref[...] = _wdot(h, w_ref[...], nt=True)


def _mod_spec(mod, tm, rows_per_group):
    r = mod.shape[1]
    tiles = max(rows_per_group // tm, 1)
    return pl.BlockSpec((None, r, mod.shape[2]), lambda i, *_: (i // tiles, 0, 0))


def in_proj(x, sc, sh, g, w, tm, rows_per_group):
    m, d = x.shape
    n = w.shape[0]
    return pl.pallas_call(
        _in_proj_kernel,
        grid=(m // tm,),
        in_specs=[pl.BlockSpec((tm, d), lambda i: (i, 0)),
                  _mod_spec(sc, tm, rows_per_group), _mod_spec(sh, tm, rows_per_group),
                  pl.BlockSpec((1, d), lambda i: (0, 0)),
                  pl.BlockSpec((n, d), lambda i: (0, 0))],
        out_specs=pl.BlockSpec((tm, n), lambda i: (i, 0)),
        out_shape=jax.ShapeDtypeStruct((m, n), F32),
        compiler_params=_cp(("arbitrary",)),
        name="in_proj",
    )(x, sc, sh, g, w)


def _out_proj_kernel(a_ref, b_ref, c_ref, d_ref, w_ref, x_ref, g1_ref, o_ref):
    cat = jnp.concatenate([a_ref[...], b_ref[...], c_ref[...], d_ref[...]], axis=1)
    o_ref[...] = x_ref[...] + g1_ref[...] * _dot(cat, w_ref[...])


def out_proj(parts, w, x, g1, tm, rows_per_group):
    m, d = x.shape
    return pl.pallas_call(
        _out_proj_kernel,
        grid=(m // tm,),
        in_specs=[pl.BlockSpec((tm, GW), lambda i: (i, 0))] * 4 + [
            pl.BlockSpec((4 * GW, d), lambda i: (0, 0)),
            pl.BlockSpec((tm, d), lambda i: (i, 0)),
            _mod_spec(g1, tm, rows_per_group)],
        out_specs=pl.BlockSpec((tm, d), lambda i: (i, 0)),
        out_shape=jax.ShapeDtypeStruct((m, d), F32),
        compiler_params=_cp(("arbitrary",)),
        name="out_proj",
    )(*parts, w, x, g1)


def _ffn_kernel(x_ref, sc_ref, sh_ref, g_ref, gate_ref, w1_ref, w3_ref, w2_ref, o_ref, h_scr, acc_scr):
    j = pl.program_id(1)

    @pl.when(j == 0)
    def _():
        h_scr[...] = _modnorm(x_ref[...], g_ref[...], sc_ref[...], sh_ref[...]).astype(h_scr.dtype)
        acc_scr[...] = jnp.zeros_like(acc_scr)

    h = h_scr[...]
    acc_scr[...] += _wdot(_silu(_wdot(h, w1_ref[...])) * _wdot(h, w3_ref[...]), w2_ref[...])

    @pl.when(j == pl.num_programs(1) - 1)
    def _():
        o_ref[...] = x_ref[...] + gate_ref[...] * acc_scr[...]


def ffn_dense(x, sc, sh, g, gate, w1, w3, w2, tm, tf, rows_per_group):
    m, d = x.shape
    ff = w1.shape[1]
    return pl.pallas_call(
        _ffn_kernel,
        grid=(m // tm, ff // tf),
        in_specs=[pl.BlockSpec((tm, d), lambda i, j: (i, 0)),
                  _mod_spec(sc, tm, rows_per_group), _mod_spec(sh, tm, rows_per_group),
                  pl.BlockSpec((1, d), lambda i, j: (0, 0)),
                  _mod_spec(gate, tm, rows_per_group),
                  pl.BlockSpec((d, tf), lambda i, j: (0, j)),
                  pl.BlockSpec((d, tf), lambda i, j: (0, j)),
                  pl.BlockSpec((tf, d), lambda i, j: (j, 0))],
        out_specs=pl.BlockSpec((tm, d), lambda i, j: (i, 0)),
        out_shape=jax.ShapeDtypeStruct((m, d), F32),
        scratch_shapes=[pltpu.VMEM((tm, d), w1.dtype), pltpu.VMEM((tm, d), F32)],
        compiler_params=_cp(("arbitrary", "arbitrary")),
        name="ffn_dense",
    )(x, sc, sh, g, gate, w1, w3, w2)


def _route_kernel(n_valid, x_ref, sc_ref, sh_ref, g_ref, rt_ref, tri_ref, h_ref, rank_ref, gate_ref, cnt_ref):
    t = x_ref.shape[0]
    h = _modnorm(x_ref[...], g_ref[...], sc_ref[...], sh_ref[...])
    h_ref[...] = h.astype(BF16)
    logits = jnp.dot(h, rt_ref[...], preferred_element_type=F32, precision=HI).T[:N_EXPERTS]
    e_iota = lax.broadcasted_iota(jnp.int32, logits.shape, 0).astype(F32)
    m1 = jnp.max(logits, axis=0, keepdims=True)
    i1 = jnp.min(jnp.where(logits == m1, e_iota, float(N_EXPERTS)), axis=0, keepdims=True)
    sel1 = e_iota == i1
    rest = jnp.where(sel1, -jnp.inf, logits)
    m2 = jnp.max(rest, axis=0, keepdims=True)
    i2 = jnp.min(jnp.where(rest == m2, e_iota, float(N_EXPERTS)), axis=0, keepdims=True)
    sel2 = e_iota == i2
    e2 = jnp.exp(m2 - m1)
    den = 1.0 + e2
    gate = jnp.where(sel1, 1.0 / den, 0.0) + jnp.where(sel2, e2 / den, 0.0)
    tok = pl.program_id(0) * t + lax.broadcasted_iota(jnp.int32, logits.shape, 1)
    sel = jnp.logical_and(jnp.logical_or(sel1, sel2), tok < n_valid)
    incl = _dot(jnp.where(sel, 1.0, 0.0).astype(BF16), tri_ref[...])
    rank_ref[...] = jnp.where(sel, incl - 1.0, -1.0)
    gate_ref[...] = gate
    cnt_ref[...] = jnp.broadcast_to(incl[:, t - 1:t], cnt_ref.shape)


def _moe_kernel(cnt_ref, h_ref, rank_ref, gate_ref, x_ref, g2_ref, w1_ref, w3_ref, w2_ref, o_ref, xs_scr, y_scr):
    i, e, j = pl.program_id(0), pl.program_id(1), pl.program_id(2)
    nff = pl.num_programs(2)
    t = h_ref.shape[0]
    r = min(MOE_ROWS, t)
    cnt = cnt_ref[i * N_EXPERTS + e]
    nb = (cnt + (r - 1)) // r
    row = lax.broadcasted_iota(jnp.int32, (r, t), 0).astype(F32)

    @pl.when(jnp.logical_and(e == 0, j == 0))
    def _():
        o_ref[...] = jnp.zeros_like(o_ref)
        y_scr[...] = jnp.zeros_like(y_scr)

    @pl.when(j == 0)
    def _():
        def gather(s, c):
            r0 = pl.multiple_of(s * r, r)
            onehot = jnp.where(rank_ref[...] == row + (s * r).astype(F32), 1.0, 0.0).astype(BF16)
            xs_scr[pl.ds(r0, r), :] = _dot(onehot, h_ref[...]).astype(BF16)
            y_scr[pl.ds(r0, r), :] = jnp.zeros((r, y_scr.shape[1]), F32)
            return c
        lax.fori_loop(0, nb, gather, 0)

    def ffn(s, c, rows=r):
        r0 = pl.multiple_of(s * r, r)
        xs = xs_scr[pl.ds(r0, rows), :]
        act = (_silu(_dot(xs, w1_ref[...])) * _dot(xs, w3_ref[...])).astype(BF16)
        y_scr[pl.ds(r0, rows), :] += _dot(act, w2_ref[...])
        return c

    n_full = cnt // r
    rem = cnt - n_full * r
    lax.fori_loop(0, n_full, ffn, 0)
    @pl.when(rem > 0)
    def _():
        ffn(n_full, 0)

    @pl.when(j == nff - 1)
    def _():
        rp = min(max(r, 256), t)
        rowp = lax.broadcasted_iota(jnp.int32, (rp, t), 0).astype(F32)

        def scatter(s, c):
            r0 = pl.multiple_of(s * rp, rp)
            hit = rank_ref[...] == rowp + (s * rp).astype(F32)
            wgt = jnp.where(hit, gate_ref[...], 0.0).astype(BF16)
            o_ref[...] += _dot_tn(wgt, y_scr[pl.ds(r0, rp), :].astype(BF16))
            return c
        lax.fori_loop(0, (nb * r + rp - 1) // rp, scatter, 0)

    @pl.when(jnp.logical_and(e == N_EXPERTS - 1, j == nff - 1))
    def _():
        o_ref[...] = x_ref[...] + g2_ref[...] * o_ref[...]


def moe_ffn(x, sc, sh, g, gate2, router_t, w1, w3, w2, t, tf, rows_per_group, n_valid):
    m, d = x.shape
    nt = m // t
    ne, _, ff = w1.shape
    assert ne == N_EXPERTS
    tri = jnp.asarray(np.triu(np.ones((t, t), np.float32)), BF16)
    h, rank, gate, cnt = pl.pallas_call(
        functools.partial(_route_kernel, n_valid),
        grid=(nt,),
        in_specs=[pl.BlockSpec((t, d), lambda i: (i, 0)),
                  _mod_spec(sc, t, rows_per_group), _mod_spec(sh, t, rows_per_group),
                  pl.BlockSpec((1, d), lambda i: (0, 0)),
                  pl.BlockSpec((d, 128), lambda i: (0, 0)),
                  pl.BlockSpec((t, t), lambda i: (0, 0))],
        out_specs=[pl.BlockSpec((t, d), lambda i: (i, 0)),
                   pl.BlockSpec((None, ne, t), lambda i: (i, 0, 0)),
                   pl.BlockSpec((None, ne, t), lambda i: (i, 0, 0)),
                   pl.BlockSpec((None, ne, 128), lambda i: (i, 0, 0))],
        out_shape=[jax.ShapeDtypeStruct((m, d), BF16),
                   jax.ShapeDtypeStruct((nt, ne, t), F32),
                   jax.ShapeDtypeStruct((nt, ne, t), F32),
                   jax.ShapeDtypeStruct((nt, ne, 128), F32)],
        compiler_params=_cp(("arbitrary",)),
        name="moe_route",
    )(x, sc, sh, g, router_t, tri)
    nblk = cnt[:, :, 0].astype(jnp.int32).reshape(nt * ne)
    rank = rank.reshape(nt, ne, 1, t)
    gate = gate.reshape(nt, ne, 1, t)
    out = pl.pallas_call(
        _moe_kernel,
        grid_spec=pltpu.PrefetchScalarGridSpec(
            num_scalar_prefetch=1,
            grid=(nt, ne, ff // tf),
            in_specs=[pl.BlockSpec((t, d), lambda i, e, j, nb: (i, 0)),
                      pl.BlockSpec((None, None, 1, t), lambda i, e, j, nb: (i, e, 0, 0)),
                      pl.BlockSpec((None, None, 1, t), lambda i, e, j, nb: (i, e, 0, 0)),
                      pl.BlockSpec((t, d), lambda i, e, j, nb: (i, 0)),
                      _mod_spec(gate2, t, rows_per_group),
                      pl.BlockSpec((None, d, tf), lambda i, e, j, nb: (e, 0, j)),
                      pl.BlockSpec((None, d, tf), lambda i, e, j, nb: (e, 0, j)),
                      pl.BlockSpec((None, tf, d), lambda i, e, j, nb: (e, j, 0))],
            out_specs=pl.BlockSpec((t, d), lambda i, e, j, nb: (i, 0)),
            scratch_shapes=[pltpu.VMEM((t, d), BF16), pltpu.VMEM((t, d), F32)]),
        out_shape=jax.ShapeDtypeStruct((m, d), F32),
        compiler_params=_cp(("arbitrary", "arbitrary", "arbitrary")),
        name="moe_ffn",
    )(nblk, h, rank, gate, x, gate2, w1, w3, w2)
    return out


def _conv_kernel(a_ref, g_ref, w_ref, b_ref, lg_ref, lb_ref, o_ref, cn_ref, ext_scr, sh_scr):
    tc = a_ref.shape[0]
    halo = 32
    off = halo - (CONV_K - 1)

    @pl.when(pl.program_id(1) == 0)
    def _():
        ext_scr[0:halo, :] = jnp.zeros((halo, GW), F32)

    ext_scr[halo:halo + tc, :] = a_ref[...] * jax.nn.sigmoid(g_ref[...])
    for s in range(1, 8):
        sh_scr[s - 1] = ext_scr[s:s + tc + halo - 8, :]
    rc = 64
    for c in range(tc // rc):
        acc = jnp.zeros((rc, GW), F32)
        for k in range(CONV_K):
            s, r0 = (k + off) % 8, c * rc + (k + off) // 8 * 8
            rows = ext_scr[r0:r0 + rc, :] if s == 0 else sh_scr[s - 1, r0:r0 + rc, :]
            acc = acc + w_ref[k:k + 1, :] * rows
        y = acc + b_ref[...]
        mu = jnp.mean(y, axis=-1, keepdims=True)
        yc = y - mu
        var = jnp.mean(yc * yc, axis=-1, keepdims=True)
        o_ref[c * rc:(c + 1) * rc, :] = _silu(yc * lax.rsqrt(var + EPS) * lg_ref[...] + lb_ref[...]).astype(o_ref.dtype)
    cn_ref[...] = ext_scr[tc + off:tc + halo, :]
    ext_scr[0:halo, :] = ext_scr[tc:tc + halo, :]


def conv_prompt(z, nb, l, conv_w, conv_b, ln_g, ln_b, tc):
    m = z.shape[0]
    nl = l // tc
    row = lambda v: v.reshape(1, GW)
    return pl.pallas_call(
        _conv_kernel,
        grid=(nb, nl),
        in_specs=[pl.BlockSpec((tc, GW), lambda b, i: (b * nl + i, G_CA)),
                  pl.BlockSpec((tc, GW), lambda b, i: (b * nl + i, G_CG)),
                  pl.BlockSpec((CONV_K, GW), lambda b, i: (0, 0)),
                  pl.BlockSpec((1, GW), lambda b, i: (0, 0)),
                  pl.BlockSpec((1, GW), lambda b, i: (0, 0)),
                  pl.BlockSpec((1, GW), lambda b, i: (0, 0))],
        out_specs=[pl.BlockSpec((tc, GW), lambda b, i: (b * nl + i, 0)),
                   pl.BlockSpec((None, CONV_K - 1, GW), lambda b, i: (b, 0, 0))],
        out_shape=[jax.ShapeDtypeStruct((m, GW), BF16),
                   jax.ShapeDtypeStruct((nb, CONV_K - 1, GW), F32)],
        scratch_shapes=[pltpu.VMEM((tc + 32, GW), F32), pltpu.VMEM((7, tc + 24, GW), F32)],
        compiler_params=_cp(("arbitrary", "arbitrary")),
        name="conv_prompt",
    )(z, z, conv_w, row(conv_b), row(ln_g), row(ln_b))


def _fox_prep_kernel(q_ref, k_ref, v_ref, t_ref, gq_ref, gk_ref, fb_ref, hs_ref, tri_ref,
                     k_all, v_all, lf_all, qb_ref, kb_ref, vb_ref, kt_ref, vt_ref, lft_ref, fc_ref, carry_scr):
    del k_all, v_all, lf_all

    @pl.when(pl.program_id(1) == 0)
    def _():
        carry_scr[...] = jnp.zeros_like(carry_scr)

    hs = hs_ref[...]
    qn = _head_rms(q_ref[...], hs, gq_ref[...])
    kn = _head_rms(k_ref[...], hs, gk_ref[...])
    v = v_ref[...]
    vt = v.T
    qb_ref[...] = (qn * (HD ** -0.5 * LOG2E)).astype(BF16)
    kb_ref[...] = kn.astype(BF16)
    vb_ref[...] = vt.astype(BF16)
    kt_ref[...] = kn.T
    vt_ref[...] = vt
    lf = _log_sigmoid(t_ref[...] + fb_ref[...])
    lft_ref[...] = lf.T[:NH]
    cum = _dot_sel(tri_ref[...], lf, "b") + carry_scr[...]
    fc_ref[...] = cum * LOG2E
    carry_scr[...] = cum[cum.shape[0] - 1:, :]


def _fox_flash_kernel(q_ref, k_ref, vt_ref, fc_ref, o_ref, ot_scr):
    tq = q_ref.shape[0]
    tk = tq
    qi = pl.program_id(1)
    q = q_ref[...]
    head = _head_ids((1, GW), 1)
    kpos = lax.broadcasted_iota(jnp.int32, (tk, tq), 0)
    qpos = lax.broadcasted_iota(jnp.int32, (tk, tq), 1)
    qhs = [jnp.where(head == h, q, jnp.zeros_like(q)) for h in range(NH)]

    def step(jb, carry, diagonal):
        k0 = pl.multiple_of(jb * tk, tk)
        kb = k_ref[pl.ds(k0, tk), :]
        fcb = fc_ref[pl.ds(k0, tk), :]
        out = []
        for h in range(NH):
            m, l, acc = carry[h]
            s = _dot_nt(kb, qhs[h]) - fcb[:, h:h + 1]
            if diagonal:
                s = jnp.where(kpos <= qpos, s, NEG)
            m_new = jnp.maximum(m, jnp.max(s, axis=0, keepdims=True))
            alpha = jnp.exp2(m - m_new)
            p = jnp.exp2(s - m_new)
            l = alpha * l + jnp.sum(p, axis=0, keepdims=True)
            acc = alpha * acc + _dot(vt_ref[jb, h * HD:(h + 1) * HD, :], p.astype(BF16))
            out.append((m_new, l, acc))
        return tuple(out)

    init = tuple((jnp.full((1, tq), NEG, F32), jnp.zeros((1, tq), F32), jnp.zeros((HD, tq), F32)) for _ in range(NH))
    carry = lax.fori_loop(0, qi, functools.partial(step, diagonal=False), init)
    carry = step(qi, carry, True)
    for h in range(NH):
        _, l, acc = carry[h]
        ot_scr[h * HD:(h + 1) * HD, :] = acc * (1.0 / l)
    o_ref[...] = ot_scr[...].T.astype(o_ref.dtype)


def fox_prompt(z, nb, l, gq, gk, fb, tq, li, k_all, v_all, lf_all):
    m = z.shape[0]
    tp = tq
    nl = l // tp
    hs = jnp.asarray(_np_hsum())
    tri = jnp.asarray(np.tril(np.ones((tp, tp), np.float32)))
    row = lambda v: jnp.tile(v, NH).reshape(1, GW)
    fbp = jnp.zeros((1, TAIL_W), F32).at[0, :NH].set(fb)
    blk = lambda g: pl.BlockSpec((tp, GW), lambda b, i: (b * nl + i, g))
    oblk = lambda w: pl.BlockSpec((tp, w), lambda b, i: (b * nl + i, 0))
    anyspec = pl.BlockSpec(memory_space=pl.ANY)
    stacked = lambda r: pl.BlockSpec((None, None, r, tp), lambda b, i: (b, li, 0, i))
    qb, kb, vb, k_all, v_all, lf_all, fc = pl.pallas_call(
        _fox_prep_kernel,
        grid=(nb, nl),
        in_specs=[blk(G_FQ), blk(G_FK), blk(G_FV),
                  pl.BlockSpec((tp, TAIL_W), lambda b, i: (b * nl + i, TAIL_BLK)),
                  pl.BlockSpec((1, GW), lambda b, i: (0, 0)),
                  pl.BlockSpec((1, GW), lambda b, i: (0, 0)),
                  pl.BlockSpec((1, TAIL_W), lambda b, i: (0, 0)),
                  pl.BlockSpec((GW, GW), lambda b, i: (0, 0)),
                  pl.BlockSpec((tp, tp), lambda b, i: (0, 0)),
                  anyspec, anyspec, anyspec],
        out_specs=[oblk(GW)] * 2 + [pl.BlockSpec((None, None, GW, tp), lambda b, i: (b, i, 0, 0)),
                                    stacked(GW), stacked(GW), stacked(NH), oblk(TAIL_W)],
        out_shape=[jax.ShapeDtypeStruct((m, GW), BF16)] * 2 + [jax.ShapeDtypeStruct((nb, nl, GW, tp), BF16),
                   jax.ShapeDtypeStruct(k_all.shape, F32), jax.ShapeDtypeStruct(v_all.shape, F32),
                   jax.ShapeDtypeStruct(lf_all.shape, F32), jax.ShapeDtypeStruct((m, TAIL_W), F32)],
        input_output_aliases={9: 3, 10: 4, 11: 5},
        scratch_shapes=[pltpu.VMEM((1, TAIL_W), F32)],
        compiler_params=_cp(("arbitrary", "arbitrary")),
        name="fox_prep",
    )(z, z, z, z, row(gq), row(gk), fbp, hs, tri, k_all, v_all, lf_all)
    nq = l // tq
    o = pl.pallas_call(
        _fox_flash_kernel,
        grid=(nb, nq),
        in_specs=[pl.BlockSpec((tq, GW), lambda b, i: (b * nq + i, 0)),
                  pl.BlockSpec((l, GW), lambda b, i: (b, 0)),
                  pl.BlockSpec((None, nq, GW, tq), lambda b, i: (b, 0, 0, 0)),
                  pl.BlockSpec((l, TAIL_W), lambda b, i: (b, 0))],
        out_specs=pl.BlockSpec((tq, GW), lambda b, i: (b * nq + i, 0)),
        out_shape=jax.ShapeDtypeStruct((m, GW), BF16),
        scratch_shapes=[pltpu.VMEM((GW, tq), F32)],
        compiler_params=_cp(("arbitrary", "arbitrary")),
        name="fox_flash",
    )(qb, kb, vb, fc)
    return o, k_all, v_all, lf_all


GLA_SAFE_LOG = -80.0


def _gla_kernel(q_ref, k_ref, v_ref, t_ref, go_ref, w2_ref, bg_ref, gn_ref, hs_ref, cs_ref, bd_ref,
                o_ref, st_ref, st_scr, x_scr, o_scr):
    tg = q_ref.shape[0]
    c = GLA_CHUNK

    @pl.when(pl.program_id(1) == 0)
    def _():
        st_scr[...] = jnp.zeros_like(st_scr)

    hs = hs_ref[...]
    bd = bd_ref[...]
    glog = _log_sigmoid(_dot(t_ref[...].astype(BF16), w2_ref[...]) + bg_ref[...]) * (1.0 / GLA_TAU)
    q = q_ref[...] * (HD ** -0.5)
    k = k_ref[...]
    v = v_ref[...]
    vb = v.astype(BF16)
    lower = lax.broadcasted_iota(jnp.int32, (tg, tg), 0) >= lax.broadcasted_iota(jnp.int32, (tg, tg), 1)
    bfull = _dot_sel(jnp.where(lower, 1.0, 0.0), glog, "b")
    safe = jnp.min(bfull) >= GLA_SAFE_LOG

    @pl.when(safe)
    def _():
        head = _head_ids((1, GW), 1)
        btot = bfull[tg - 1:tg]
        qe = (q * jnp.exp(bfull)).astype(BF16)
        kinv = (k * jnp.exp(-bfull)).astype(BF16)
        kend = (k * jnp.exp(btot - bfull)).astype(BF16)
        o = _dot_nt(qe, st_scr[...].astype(BF16))
        for h in range(NH):
            s = _dot_nt(jnp.where(head == h, qe, jnp.zeros_like(qe)), kinv)
            o = o + jnp.where(head == h, _dot(jnp.where(lower, s, 0.0).astype(BF16), vb), 0.0)
        o_scr[...] = o
        st_scr[...] = st_scr[...] * jnp.exp(btot) + _dot_tn(vb, kend) * bd

    @pl.when(jnp.logical_not(safe))
    def _():
        hs_b = hs.astype(BF16)
        cs = cs_ref[...]
        bcum = _dot_sel(jnp.where(lower, cs, 0.0), glog, "b")
        blast = _dot_sel(cs, glog, "b")
        qe = (q * jnp.exp(bcum)).astype(BF16)
        ke = (k * jnp.exp(blast - bcum)).astype(BF16)
        ii = lax.broadcasted_iota(jnp.int32, (c, GW), 0)
        for n in range(tg // c):
            r0 = n * c
            bc = bcum[r0:r0 + c]
            qc = q[r0:r0 + c]
            kc = k[r0:r0 + c]
            vc = v[r0:r0 + c]
            for j in range(c):
                ex = jnp.exp(jnp.where(ii >= j, bc - bc[j:j + 1], -jnp.inf))
                x_scr[j * c:(j + 1) * c, :] = (qc * ex * kc[j:j + 1]).astype(BF16)
            att = _dot(x_scr[...], hs_b)
            o = _dot_nt(qe[r0:r0 + c], st_scr[...].astype(BF16))
            for j in range(c):
                o = o + att[j * c:(j + 1) * c] * vc[j:j + 1]
            o_scr[r0:r0 + c, :] = o
            kv = _dot_tn(vb[r0:r0 + c], ke[r0:r0 + c])
            st_scr[...] = st_scr[...] * jnp.exp(blast[r0:r0 + 1]) + kv * bd

    o_ref[...] = (_head_rms(o_scr[...], hs, gn_ref[...]) * _silu(go_ref[...])).astype(o_ref.dtype)
    st_ref[...] = st_scr[...]


def gla_prompt(z, nb, l, w_gate2, b_gate, norm_g, tg):
    m = z.shape[0]
    nl = l // tg
    hs = jnp.asarray(_np_hsum())
    ch = np.arange(tg) // GLA_CHUNK
    cs = jnp.asarray((ch[:, None] == ch[None, :]).astype(np.float32))
    w2p = jnp.zeros((TAIL_W, GW), F32).at[NH:NH + GLA_LR].set(w_gate2).astype(BF16)
    blk = lambda g: pl.BlockSpec((tg, GW), lambda b, i: (b * nl + i, g))
    full = lambda r, c: pl.BlockSpec((r, c), lambda b, i: (0, 0))
    return pl.pallas_call(
        _gla_kernel,
        grid=(nb, nl),
        in_specs=[blk(G_GQ), blk(G_GK), blk(G_GV),
                  pl.BlockSpec((tg, TAIL_W), lambda b, i: (b * nl + i, TAIL_BLK)),
                  blk(G_GO), full(TAIL_W, GW), full(1, GW), full(1, GW), full(GW, GW), full(tg, tg), full(GW, GW)],
        out_specs=[pl.BlockSpec((tg, GW), lambda b, i: (b * nl + i, 0)),
                   pl.BlockSpec((None, GW, GW), lambda b, i: (b, 0, 0))],
        out_shape=[jax.ShapeDtypeStruct((m, GW), BF16), jax.ShapeDtypeStruct((nb, GW, GW), F32)],
        scratch_shapes=[pltpu.VMEM((GW, GW), F32), pltpu.VMEM((GLA_CHUNK * GLA_CHUNK, GW), BF16),
                        pltpu.VMEM((tg, GW), F32)],
        compiler_params=_cp(("arbitrary", "arbitrary")),
        name="gla_prompt",
    )(z, z, z, z, z, w2p, b_gate.reshape(1, GW), norm_g.reshape(1, GW), hs, cs, hs)


def _ret_lg_row():
    lg = np.log(1.0 - np.exp2(-5.0 - np.arange(NH, dtype=np.float32))).astype(np.float32)
    return np.repeat(lg, HD).reshape(1, GW)


def _ret_decay_mask(c):
    lg = _ret_lg_row()[0, ::HD]
    i = np.arange(c, dtype=np.float32)
    rel = i[:, None] - i[None, :]
    return np.where(rel >= 0, np.exp(np.maximum(rel, 0.0)[None] * lg[:, None, None]), 0.0).astype(np.float32)


def _rope(x, cos, sin_signed):
    first = (lax.broadcasted_iota(jnp.int32, (1, GW), 1) & (HD - 1)) < (HD // 2)
    swapped = jnp.where(first, pltpu.roll(x, GW - HD // 2, axis=1), pltpu.roll(x, HD // 2, axis=1))
    return x * cos + swapped * sin_signed


def _ret_kernel(q_ref, k_ref, v_ref, g_ref, cos_ref, sin_ref, lg_ref, gn_ref, hs_ref, dm_ref, o_ref, st_ref, st_scr):
    tr = dm_ref.shape[1]

    @pl.when(pl.program_id(1) == 0)
    def _():
        st_scr[...] = jnp.zeros_like(st_scr)

    hs = hs_ref[...]
    lg = lg_ref[...]
    head = _head_ids((1, GW), 1)
    ri = lax.broadcasted_iota(jnp.int32, (tr, 1), 0).astype(F32)
    dq = jnp.exp((ri + 1.0) * lg)
    dk = jnp.exp((tr - 1.0 - ri) * lg)
    ds = jnp.exp(tr * lg)
    for n in range(q_ref.shape[0] // tr):
        rows = slice(n * tr, (n + 1) * tr)
        cos = cos_ref[rows, :]
        sin = sin_ref[rows, :]
        q = _rope(q_ref[rows, :], cos, sin)
        k = _rope(k_ref[rows, :], cos, sin) * (HD ** -0.5)
        qb = q.astype(BF16)
        kb = k.astype(BF16)
        vb = v_ref[rows, :].astype(BF16)
        o = _dot_nt((q * dq).astype(BF16), st_scr[...].astype(BF16))
        for h in range(NH):
            att = _dot_nt(jnp.where(head == h, qb, jnp.zeros_like(qb)), kb) * dm_ref[h]
            o = o + jnp.where(head == h, _dot(att.astype(BF16), vb), 0.0)
        st_scr[...] = st_scr[...] * ds + _dot_tn(vb, (k * dk).astype(BF16)) * hs
        o_ref[rows, :] = (_head_rms(o, hs, gn_ref[...]) * _silu(g_ref[rows, :])).astype(o_ref.dtype)
    st_ref[...] = st_scr[...]


def _rope_tables(pos):
    half = HD // 2
    inv = 10000.0 ** (-jnp.arange(half, dtype=F32) / half)
    ang = pos[:, None] * inv[None, :]
    cos = jnp.cos(ang)
    sin = jnp.sin(ang)
    cos_t = jnp.tile(jnp.concatenate([cos, cos], axis=1), (1, NH))
    sin_t = jnp.tile(jnp.concatenate([-sin, sin], axis=1), (1, NH))
    return cos_t, sin_t


def ret_prompt(z, nb, l, norm_g, tt):
    m = z.shape[0]
    nl = l // tt
    tr = RET_CHUNK
    hs = jnp.asarray(_np_hsum())
    cos_t, sin_t = _rope_tables(jnp.arange(l, dtype=F32))
    blk = lambda g: pl.BlockSpec((tt, GW), lambda b, i: (b * nl + i, g))
    full = lambda r, c: pl.BlockSpec((r, c), lambda b, i: (0, 0))
    tab = pl.BlockSpec((tt, GW), lambda b, i: (i, 0))
    return pl.pallas_call(
        _ret_kernel,
        grid=(nb, nl),
        in_specs=[blk(G_RQ), blk(G_RK), blk(G_RV), blk(G_RG), tab, tab, full(1, GW), full(1, GW), full(GW, GW),
                  pl.BlockSpec((NH, tr, tr), lambda b, i: (0, 0, 0))],
        out_specs=[pl.BlockSpec((tt, GW), lambda b, i: (b * nl + i, 0)),
                   pl.BlockSpec((None, GW, GW), lambda b, i: (b, 0, 0))],
        out_shape=[jax.ShapeDtypeStruct((m, GW), BF16), jax.ShapeDtypeStruct((nb, GW, GW), F32)],
        scratch_shapes=[pltpu.VMEM((GW, GW), F32)],
        compiler_params=_cp(("arbitrary", "arbitrary")),
        name="ret_prompt",
    )(z, z, z, z, cos_t, sin_t, jnp.asarray(_ret_lg_row()), norm_g.reshape(1, GW), hs, jnp.asarray(_ret_decay_mask(tr)))


def _state_from_blockdiag(st):
    nb = st.shape[0]
    s5 = st.reshape(nb, NH, HD, NH, HD)
    diag = jnp.stack([s5[:, h, :, h, :] for h in range(NH)], axis=1)
    return diag.transpose(0, 1, 3, 2)


def _sample_rows_kernel(z_ref, buf_ref, cw_ref, cb_ref, lg_ref, lb_ref, gq_ref, gk_ref, fb_ref, w2_ref, bg_ref,
                        cos_ref, sin_ref, rlg_ref, hs_ref,
                        oa_ref, cn_ref, lf_ref, col_ref):
    grp = lambda g: z_ref[:, g * GW:(g + 1) * GW]
    tail = z_ref[:, N_MAIN:N_MAIN + TAIL_W]
    hs = hs_ref[...]
    u = grp(G_CA) * jax.nn.sigmoid(grp(G_CG))
    y = cw_ref[CONV_K - 1:CONV_K, :] * u + cb_ref[...]
    for k in range(CONV_K - 1):
        y = y + cw_ref[k:k + 1, :] * buf_ref[k]
    mu = jnp.mean(y, axis=-1, keepdims=True)
    yc = y - mu
    var = jnp.mean(yc * yc, axis=-1, keepdims=True)
    oa_ref[...] = _silu(yc * lax.rsqrt(var + EPS) * lg_ref[...] + lb_ref[...]).astype(oa_ref.dtype)
    for k in range(CONV_K - 2):
        cn_ref[k] = buf_ref[k + 1]
    cn_ref[CONV_K - 2] = u
    col_ref[0] = _head_rms(grp(G_FQ), hs, gq_ref[...]) * (HD ** -0.5)
    col_ref[1] = _head_rms(grp(G_FK), hs, gk_ref[...])
    col_ref[2] = grp(G_FV)
    lf_ref[...] = _log_sigmoid(tail + fb_ref[...])
    glog = _log_sigmoid(_wdot(tail, w2_ref[...]) + bg_ref[...]) * (1.0 / GLA_TAU)
    col_ref[3] = grp(G_GQ) * (HD ** -0.5)
    col_ref[4] = grp(G_GK)
    col_ref[5] = jnp.exp(glog)
    col_ref[6] = _rope(grp(G_RQ), cos_ref[...], sin_ref[...])
    col_ref[7] = _rope(grp(G_RK), cos_ref[...], sin_ref[...]) * (HD ** -0.5)
    col_ref[8] = jnp.broadcast_to(jnp.exp(rlg_ref[...]), col_ref.shape[1:])


def _rec_step_kernel(c_ref, v_ref, s0_ref, s1_ref, o_ref, sn0_ref, sn1_ref):
    for r, (s_ref, sn_ref) in enumerate(((s0_ref, sn0_ref), (s1_ref, sn1_ref))):
        for b in range(v_ref.shape[1]):
            for h in range(NH):
                q = c_ref[3 * r, b, h]
                k = c_ref[3 * r + 1, b, h]
                a = c_ref[3 * r + 2, b, h]
                v = v_ref[r, b, h]
                s = s_ref[b, h]
                qk = jnp.sum(q * k, axis=0, keepdims=True)
                o_ref[r, b, h] = qk * v + jnp.sum((q * a) * s, axis=0, keepdims=True)
                sn_ref[b, h] = a * s + k * v


REC_SEQS = 8


def rec_step(cols, v, state_gla, state_ret, li):
    nb = cols.shape[1]
    bt = math.gcd(nb, REC_SEQS)
    sspec = pl.BlockSpec((None, bt, NH, HD, HD), lambda b: (li, b, 0, 0, 0))
    ospec = pl.BlockSpec((bt, NH, HD, HD), lambda b: (b, 0, 0, 0))
    st = jax.ShapeDtypeStruct((nb, NH, HD, HD), F32)
    o, sn0, sn1 = pl.pallas_call(
        _rec_step_kernel,
        grid=(nb // bt,),
        in_specs=[pl.BlockSpec((6, bt, NH, HD, 1), lambda b: (0, b, 0, 0, 0)),
                  pl.BlockSpec((2, bt, NH, 1, HD), lambda b: (0, b, 0, 0, 0)), sspec, sspec],
        out_specs=[pl.BlockSpec((2, bt, NH, 1, HD), lambda b: (0, b, 0, 0, 0)), ospec, ospec],
        out_shape=[jax.ShapeDtypeStruct((2, nb, NH, 1, HD), F32), st, st],
        compiler_params=_cp(("arbitrary",)),
        name="rec_step",
    )(cols.reshape(6, nb, NH, HD, 1), v.reshape(2, nb, NH, 1, HD), state_gla, state_ret)
    o = o.reshape(2, nb, GW)
    return o[0], sn0, o[1], sn1


FOX_PAGES = 64


def _fox_bias_kernel(pt_ref, lfn_ref, lf_ref, ts_ref, pre_ref, o_ref, lf_scr):
    b = pl.program_id(0)
    n_pages = o_ref.shape[1]
    pg = o_ref.shape[2]
    for p in range(n_pages):
        page = lf_ref[pt_ref[b * n_pages + p]]
        for h in range(NH):
            lf_scr[h * n_pages + p:h * n_pages + p + 1, :] = page[h:h + 1, :]
    both = _dot_sel(lf_scr[...], ts_ref[...], "a")
    suf = both[:, :pg]
    tot = both[:, pg:]
    later = _dot_sel(pre_ref[...], tot, "b")
    for h in range(NH):
        r0, r1 = h * n_pages, (h + 1) * n_pages
        o_ref[h] = lfn_ref[h:h + 1, :] + later[r0:r1] + suf[r0:r1]


def _fox_sample_kernel(pt_ref, q_ref, kn_ref, vn_ref, bias_ref, *rest):
    g_n = bias_ref.shape[1]
    k_refs = rest[:g_n]
    v_refs = rest[g_n:2 * g_n]
    o_ref, m_scr, l_scr, acc_scr, s_scr = rest[2 * g_n:]
    c = pl.program_id(1)
    pg = k_refs[0].shape[1]

    @pl.when(c == 0)
    def _():
        m_scr[...] = jnp.full(m_scr.shape, NEG, F32)
        l_scr[...] = jnp.zeros_like(l_scr)
        acc_scr[...] = jnp.zeros_like(acc_scr)

    qb = jnp.broadcast_to(q_ref[...], (GW, pg))
    for g in range(g_n):
        prod = k_refs[g][...] * qb
        for h in range(NH):
            s_scr[h * g_n + g:h * g_n + g + 1, :] = jnp.sum(prod[h * HD:(h + 1) * HD], axis=0, keepdims=True)
    for h in range(NH):
        s_h = s_scr[h * g_n:(h + 1) * g_n, :] + bias_ref[h]
        m_old = m_scr[h:h + 1, :]
        m_new = jnp.maximum(m_old, jnp.max(jnp.max(s_h, axis=0, keepdims=True), axis=1, keepdims=True))
        alpha = jnp.exp(m_old - m_new)
        p = jnp.exp(s_h - m_new)
        m_scr[h:h + 1, :] = m_new
        l_scr[h:h + 1, :] = l_scr[h:h + 1, :] * alpha + jnp.sum(p, axis=0, keepdims=True)
        acc = acc_scr[h * HD:(h + 1) * HD, :] * alpha
        for g in range(g_n):
            acc = acc + p[g:g + 1, :] * v_refs[g][h * HD:(h + 1) * HD, :]
        acc_scr[h * HD:(h + 1) * HD, :] = acc

    @pl.when(c == pl.num_programs(1) - 1)
    def _():
        prod = q_ref[...] * kn_ref[...]
        for h in range(NH):
            s_self = jnp.sum(prod[h * HD:(h + 1) * HD], axis=0, keepdims=True)
            m_h = m_scr[h:h + 1, 0:1]
            m_fin = jnp.maximum(m_h, s_self)
            a_h = jnp.exp(m_h - m_fin)
            p_self = jnp.exp(s_self - m_fin)
            l_tot = jnp.sum(l_scr[h:h + 1, :], axis=1, keepdims=True) * a_h + p_self
            num = jnp.sum(acc_scr[h * HD:(h + 1) * HD, :], axis=1, keepdims=True) * a_h \
                + p_self * vn_ref[h * HD:(h + 1) * HD, :]
            o_ref[h * HD:(h + 1) * HD, :] = num / l_tot


def fox_sample(qkv, lf_new, cache_kt, cache_vt, cache_lf, page_table, li):
    nb, n_pages = page_table.shape
    n_phys = cache_lf.shape[0]
    pg = cache_kt.shape[-1]
    g_n = math.gcd(FOX_PAGES, n_pages)
    nc = n_pages // g_n
    qkv = qkv.reshape(3, nb, GW, 1)
    pt = page_table.reshape(-1)
    lfn = jnp.broadcast_to(lf_new[:, :, None], (nb, NH, pg))
    t = np.arange(pg)
    ts = jnp.asarray(np.concatenate([(t[:, None] > t[None, :]).astype(np.float32), np.ones((pg, pg), np.float32)], axis=1))
    r = np.arange(NH * n_pages)
    later = jnp.asarray(((r[:, None] // n_pages == r[None, :] // n_pages) & (r[None, :] > r[:, None])).astype(np.float32))
    bias = pl.pallas_call(
        _fox_bias_kernel,
        grid_spec=pltpu.PrefetchScalarGridSpec(
            num_scalar_prefetch=1,
            grid=(nb,),
            in_specs=[pl.BlockSpec((None, NH, pg), lambda b, pt: (b, 0, 0)),
                      pl.BlockSpec((n_phys, None, NH, pg), lambda b, pt: (0, li, 0, 0)),
                      pl.BlockSpec((pg, 2 * pg), lambda b, pt: (0, 0)),
                      pl.BlockSpec((NH * n_pages, NH * n_pages), lambda b, pt: (0, 0))],
            out_specs=pl.BlockSpec((None, NH, n_pages, pg), lambda b, pt: (b, 0, 0, 0)),
            scratch_shapes=[pltpu.VMEM((NH * n_pages, pg), F32)]),
        out_shape=jax.ShapeDtypeStruct((nb, NH, n_pages, pg), F32),
        compiler_params=_cp(("arbitrary",)),
        name="fox_bias",
    )(pt, lfn, cache_lf, ts, later)

    def page(g):
        return lambda b, c, pt: (pt[b * n_pages + c * g_n + g], li, 0, 0)

    cspec = lambda r: pl.BlockSpec((None, None, GW, 1), lambda b, c, pt: (r, b, 0, 0))
    in_specs = [cspec(0), cspec(1), cspec(2), pl.BlockSpec((None, NH, g_n, pg), lambda b, c, pt: (b, 0, c, 0))]
    in_specs += [pl.BlockSpec((None, None, GW, pg), page(g)) for g in range(g_n)]
    in_specs += [pl.BlockSpec((None, None, GW, pg), page(g)) for g in range(g_n)]
    o = pl.pallas_call(
        _fox_sample_kernel,
        grid_spec=pltpu.PrefetchScalarGridSpec(
            num_scalar_prefetch=1,
            grid=(nb, nc),
            in_specs=in_specs,
            out_specs=pl.BlockSpec((None, GW, 1), lambda b, c, pt: (b, 0, 0)),
            scratch_shapes=[pltpu.VMEM((NH, pg), F32), pltpu.VMEM((NH, pg), F32),
                            pltpu.VMEM((GW, pg), F32), pltpu.VMEM((NH * g_n, pg), F32)]),
        out_shape=jax.ShapeDtypeStruct((nb, GW, 1), F32),
        compiler_params=_cp(("arbitrary", "arbitrary")),
        name="fox_sample",
    )(pt, qkv, qkv, qkv, bias, *([cache_kt] * g_n), *([cache_vt] * g_n))
    return o.reshape(nb, GW)


def _sample_out_kernel(oa_ref, ob_ref, oc_ref, od_ref, go_ref, rg_ref, gng_ref, rng_ref, hs_ref, w_ref, x_ref, g1_ref, o_ref):
    hs = hs_ref[...]
    oc = _head_rms(oc_ref[...], hs, gng_ref[...]) * _silu(go_ref[...])
    od = _head_rms(od_ref[...], hs, rng_ref[...]) * _silu(rg_ref[...])
    cat = jnp.concatenate([oa_ref[...], ob_ref[...], oc, od], axis=1)
    o_ref[...] = x_ref[...] + g1_ref[...] * _wdot(cat, w_ref[...])


def _prep_w_in(w_in, li):
    wt = w_in.transpose(2, 0, 1)[:, li, :]
    o_ff = 5 * GW
    o_lr = o_ff + NH + 3 * GW
    return jnp.concatenate([wt[:o_ff], wt[o_ff + NH:o_lr], wt[o_lr + GLA_LR:],
                            wt[o_ff:o_ff + NH], wt[o_lr:o_lr + GLA_LR],
                            jnp.zeros((TAIL_W - NH - GLA_LR, wt.shape[1]), wt.dtype)], axis=0)


def _ffn_apply(li, x, sc, sh, g, gate, fw, tm, rows_per_group, n_valid):
    if li % 2 == 0:
        w1, w3, w2 = fw
        return ffn_dense(x, sc, sh, g, gate, w1, w3, w2, tm, w1.shape[1] // 2, rows_per_group)
    router_t, w1, w3, w2 = fw
    return moe_ffn(x, sc, sh, g, gate, router_t, w1, w3, w2, tm, w1.shape[2] // 2, rows_per_group, n_valid)


def kernel(x_prompt, x_sample, c_prompt, c_sample, state_conv, cache_k, cache_v, cache_logf, state_gla, state_ret, page_table, w_in, w_out, conv_w, conv_b, conv_ln_g, conv_ln_b, fox_qn_g, fox_kn_g, fox_fb, gla_w_gate2, gla_b_gate, gla_norm_g, ret_norm_g, norm1_g, norm2_g, w_ada, b_ada, ffn_w1, ffn_w3, ffn_w2, moe_router, moe_w1, moe_w3, moe_w2):
    nbp, l, d = x_prompt.shape
    nbs = x_sample.shape[0]
    depth = w_in.shape[0]
    n_pages, pg = page_table.shape[1], cache_k.shape[2]
    p_len = n_pages * pg
    mp = nbp * l
    ms = -(-nbs // 8) * 8
    ms_moe = 128

    mod = ada_mod(jnp.concatenate([c_prompt, c_sample], axis=0), w_ada, b_ada)
    mod = mod.reshape(depth, nbp + nbs, 6, d)
    cache_kt = cache_k.transpose(0, 1, 3, 4, 2).reshape(cache_k.shape[0], depth, GW, pg)
    cache_vt = cache_v.transpose(0, 1, 3, 4, 2).reshape(cache_v.shape[0], depth, GW, pg)
    cache_lf = cache_logf.transpose(0, 1, 3, 2)
    conv_state = state_conv.transpose(0, 2, 1, 3)
    cos_s, sin_s = _rope_tables(jnp.full((1,), p_len, F32))
    hs = jnp.asarray(_np_hsum())
    rlg = jnp.asarray(_ret_lg_row())

    xp = x_prompt.reshape(mp, d)
    xs = jnp.zeros((ms, d), F32).at[:nbs].set(x_sample.reshape(nbs, d))
    outs_p, outs_s = [], []
    k_all = jnp.zeros((nbp, depth, GW, l), F32)
    v_all = jnp.zeros((nbp, depth, GW, l), F32)
    lf_all = jnp.zeros((nbp, depth, NH, l), F32)
    for li in range(depth):
        mp_l = [mod[li, :nbp, i].reshape(nbp, 1, d) for i in range(6)]
        ms_l = [jnp.zeros((1, ms, d), F32).at[0, :nbs].set(mod[li, nbp:, i]) for i in range(6)]
        n1 = norm1_g[li].reshape(1, d)
        n2 = norm2_g[li].reshape(1, d)
        w_in_f = _prep_w_in(w_in, li)
        w_in_b = w_in_f.astype(BF16)
        w_out_f = w_out[li]
        w_out_b = w_out_f.astype(BF16)
        if li % 2 == 0:
            fw_s = (ffn_w1[li // 2], ffn_w3[li // 2], ffn_w2[li // 2])
            fw = tuple(w.astype(BF16) for w in fw_s)
        else:
            fw = (jnp.pad(moe_router[li // 2], ((0, 0), (0, 128 - N_EXPERTS))), moe_w1[li // 2].astype(BF16),
                  moe_w3[li // 2].astype(BF16), moe_w2[li // 2].astype(BF16))
            fw_s = fw

        sh1, sc1, g1, sh2, sc2, g2 = mp_l
        z = in_proj(xp, sc1, sh1, n1, w_in_b, 512, l)
        out_a, conv_new = conv_prompt(z, nbp, l, conv_w[li], conv_b[li], conv_ln_g[li], conv_ln_b[li], 512)
        out_b, k_all, v_all, lf_all = fox_prompt(z, nbp, l, fox_qn_g[li], fox_kn_g[li], fox_fb[li], 512, li,
                                                 k_all, v_all, lf_all)
        out_c, gla_st = gla_prompt(z, nbp, l, gla_w_gate2[li], gla_b_gate[li], gla_norm_g[li], 256)
        out_d, ret_st = ret_prompt(z, nbp, l, ret_norm_g[li], 512)
        xp = out_proj((out_a, out_b, out_c, out_d), w_out_b, xp, g1, 512, l)
        xp = _ffn_apply(li, xp, sc2, sh2, n2, g2, fw, 512 if li % 2 == 0 else 1024, l, mp)
        outs_p.append((conv_new, None, None, None, _state_from_blockdiag(gla_st), _state_from_blockdiag(ret_st)))

        sh1, sc1, g1, sh2, sc2, g2 = ms_l
        zs = in_proj(xs, sc1, sh1, n1, w_in_f, ms, ms)[:nbs]
        row = lambda v_: v_.reshape(1, GW)
        w2p = jnp.zeros((TAIL_W, GW), F32).at[NH:NH + GLA_LR].set(gla_w_gate2[li])
        fbp = jnp.zeros((1, TAIL_W), F32).at[0, :NH].set(fox_fb[li])
        r32 = jax.ShapeDtypeStruct((nbs, GW), F32)
        oa, cn, lfs, cols = pl.pallas_call(
            _sample_rows_kernel,
            out_shape=[r32, jax.ShapeDtypeStruct((CONV_K - 1, nbs, GW), F32),
                       jax.ShapeDtypeStruct((nbs, TAIL_W), F32), jax.ShapeDtypeStruct((9, nbs, GW), F32)],
            compiler_params=pltpu.CompilerParams(vmem_limit_bytes=VMEM_LIMIT),
            name="sample_rows",
        )(zs, conv_state[li], conv_w[li], row(conv_b[li]), row(conv_ln_g[li]), row(conv_ln_b[li]),
          row(jnp.tile(fox_qn_g[li], NH)), row(jnp.tile(fox_kn_g[li], NH)), fbp, w2p, row(gla_b_gate[li]),
          cos_s, sin_s, rlg, hs)
        grp = lambda g_: zs[:, g_ * GW:(g_ + 1) * GW]
        fk, f_v = cols[1], cols[2]
        ob = fox_sample(cols[:3], lfs[:, :NH], cache_kt, cache_vt, cache_lf, page_table, li)
        oc, gla_new, od, ret_new = rec_step(cols[3:], jnp.stack([grp(G_GV), grp(G_RV)]), state_gla, state_ret, li)
        pad = lambda v_: jnp.zeros((ms, v_.shape[1]), F32).at[:nbs].set(v_)
        xs = pl.pallas_call(
            _sample_out_kernel,
            out_shape=jax.ShapeDtypeStruct((ms, d), F32),
            compiler_params=pltpu.CompilerParams(vmem_limit_bytes=VMEM_LIMIT),
            name="sample_out",
        )(pad(oa), pad(ob), pad(oc), pad(od), pad(grp(G_GO)), pad(grp(G_RG)), row(gla_norm_g[li]), row(ret_norm_g[li]),
          hs, w_out_f, xs, g1[0])
        if li % 2 == 0:
            xs = _ffn_apply(li, xs, sc2, sh2, n2, g2, fw_s, ms, ms, nbs)
        else:
            wide = lambda a: jnp.zeros(a.shape[:-2] + (ms_moe, d), F32).at[..., :ms, :].set(a)
            xs = _ffn_apply(li, wide(xs), wide(sc2), wide(sh2), n2, wide(g2), fw_s, ms_moe, ms_moe, nbs)[:ms]
        outs_s.append((cn.transpose(1, 0, 2), fk.reshape(nbs, 1, NH, HD), f_v.reshape(nbs, 1, NH, HD),
                       lfs[:, :NH].reshape(nbs, 1, NH), gla_new, ret_new))

    st = lambda lst, i, ax: jnp.stack([s[i] for s in lst], axis=ax)
    heads_last = lambda a: a.reshape(nbp, depth, NH, HD, l).transpose(0, 1, 4, 2, 3)
    return (xp.reshape(nbp, l, d), xs[:nbs].reshape(nbs, 1, d),
            st(outs_p, 0, 0), st(outs_s, 0, 0),
            heads_last(k_all), st(outs_s, 1, 1),
            heads_last(v_all), st(outs_s, 2, 1),
            lf_all.transpose(0, 1, 3, 2), st(outs_s, 3, 1),
            st(outs_p, 4, 0), st(outs_s, 4, 0),
            st(outs_p, 5, 0), st(outs_s, 5, 0))
```

```python
import functools
import math

import numpy as np
import jax
import jax.numpy as jnp
from jax import lax
from jax.experimental import pallas as pl
from jax.experimental.pallas import tpu as pltpu

F32 = jnp.float32
BF16 = jnp.bfloat16
HI = lax.Precision.HIGHEST

D_MODEL = 1024
NH = 4
HD = 64
GW = NH * HD
CONV_K = 31
GLA_LR = 16
GLA_TAU = 16.0
GLA_CHUNK = 16
RET_CHUNK = 128
N_EXPERTS = 8
EPS = 1e-6
N_MAIN = 13 * GW
TAIL_W = 128
IN_PAD = N_MAIN + TAIL_W
TAIL_BLK = N_MAIN // TAIL_W
(G_CA, G_CG, G_FQ, G_FK, G_FV, G_GQ, G_GK, G_GV, G_GO, G_RQ, G_RK, G_RV, G_RG) = range(13)
VMEM_LIMIT = 56 * 1024 * 1024
MOE_ROWS = 128
NEG = -1e30
LOG2E = math.log2(math.e)


def _cp(sem, vmem=VMEM_LIMIT):
    return pltpu.CompilerParams(dimension_semantics=sem, vmem_limit_bytes=vmem)


def _silu(x):
    return x * jax.nn.sigmoid(x)


def _log_sigmoid(x):
    return jnp.minimum(x, 0.0) - jnp.log1p(jnp.exp(-jnp.abs(x)))


def _dot(a, b):
    return jnp.dot(a, b, preferred_element_type=F32)


def _split3(x):
    hi = x.astype(BF16)
    r1 = x - hi.astype(F32)
    mid = r1.astype(BF16)
    lo = (r1 - mid.astype(F32)).astype(BF16)
    return hi, mid, lo


def _dot_sel(a, b, data):
    if data == "a":
        sel = b.astype(BF16)
        return sum(_dot(p, sel) for p in _split3(a))
    sel = a.astype(BF16)
    return sum(_dot(sel, p) for p in _split3(b))


def _head_ids(shape, dim):
    return lax.shift_right_logical(lax.broadcasted_iota(jnp.int32, shape, dim), int(math.log2(HD)))


def _dot_nt(a, b, **kw):
    return lax.dot_general(a, b, (((1,), (1,)), ((), ())), preferred_element_type=F32, **kw)


def _dot_tn(a, b, **kw):
    return lax.dot_general(a, b, (((0,), (0,)), ((), ())), preferred_element_type=F32, **kw)


def _wdot(a, w, nt=False):
    if w.dtype == BF16:
        a, kw = a.astype(BF16), {}
    else:
        kw = {"precision": HI}
    return _dot_nt(a, w, **kw) if nt else jnp.dot(a, w, preferred_element_type=F32, **kw)


def _modnorm(x, g, sc, sh):
    y = x * lax.rsqrt(jnp.mean(x * x, axis=-1, keepdims=True) + EPS)
    return (y * g) * (1.0 + sc) + sh


def _head_rms(x, hsum, g):
    ms = _dot_sel(x * x, hsum, "a") * (1.0 / HD)
    return x * lax.rsqrt(ms + EPS) * g


def _np_hsum():
    i = np.arange(GW) // HD
    return (i[:, None] == i[None, :]).astype(np.float32)


def _ada_kernel(c_ref, w_ref, b_ref, o_ref):
    c = c_ref[...]
    o_ref[...] = _wdot(_silu(c), w_ref[...]) + b_ref[...]


def ada_mod(c_all, w_ada, b_ada):
    depth, d, n = w_ada.shape
    r = c_all.shape[0]
    tn = 1536
    return pl.pallas_call(
        _ada_kernel,
        grid=(depth, n // tn),
        in_specs=[pl.BlockSpec((r, d), lambda l, j: (0, 0)),
                  pl.BlockSpec((None, d, tn), lambda l, j: (l, 0, j)),
                  pl.BlockSpec((None, 1, tn), lambda l, j: (l, 0, j))],
        out_specs=pl.BlockSpec((None, r, tn), lambda l, j: (l, 0, j)),
        out_shape=jax.ShapeDtypeStruct((depth, r, n), F32),
        compiler_params=_cp(("arbitrary", "arbitrary")),
        name="ada_mod",
    )(c_all, w_ada, b_ada.reshape(depth, 1, n))


def _in_proj_kernel(x_ref, sc_ref, sh_ref, g_ref, w_ref, z_ref):
    h = _modnorm(x_ref[...], g_ref[...], sc_ref[...], sh_ref[...])
    z_ref[...] = _wdot(h, w_ref[...], nt=True)


MOD_SH1, MOD_SC1, MOD_G1, MOD_SH2, MOD_SC2, MOD_G2 = range(6)


def _mod_spec(mod, li, k, tm, rows_per_group):
    if mod.ndim == 5:
        tiles = rows_per_group // tm
        return pl.BlockSpec((None, None, None, 1, mod.shape[-1]), lambda i, *_: (li, k, i // tiles, 0, 0))
    return pl.BlockSpec((None, None, tm, mod.shape[-1]), lambda i, *_: (li, k, 0, 0))


def in_proj(x, mod, li, g, w, tm, rows_per_group):
    m, d = x.shape
    n = w.shape[0]
    return pl.pallas_call(
        _in_proj_kernel,
        grid=(m // tm,),
        in_specs=[pl.BlockSpec((tm, d), lambda i: (i, 0)),
                  _mod_spec(mod, li, MOD_SC1, tm, rows_per_group), _mod_spec(mod, li, MOD_SH1, tm, rows_per_group),
                  pl.BlockSpec((1, d), lambda i: (0, 0)),
                  pl.BlockSpec((n, d), lambda i: (0, 0))],
        out_specs=pl.BlockSpec((tm, n), lambda i: (i, 0)),
        out_shape=jax.ShapeDtypeStruct((m, n), F32),
        compiler_params=_cp(("arbitrary",)),
        name="in_proj",
    )(x, mod, mod, g, w)


def _block_input(x_ref, mix_refs):
    x = x_ref[...]
    if mix_refs:
        *parts, w_ref, g1_ref = mix_refs
        x = x + g1_ref[...] * _dot(jnp.concatenate([p[...] for p in parts], axis=1), w_ref[...])
    return x


def _mix_operands(mix, mod, li, tm, rows_per_group):
    if mix is None:
        return [], []
    parts, w_out = mix
    specs = [pl.BlockSpec((tm, GW), lambda i, *_: (i, 0))] * 4 + [
        pl.BlockSpec(w_out.shape, lambda i, *_: (0, 0)), _mod_spec(mod, li, MOD_G1, tm, rows_per_group)]
    return specs, [*parts, w_out, mod]


def _ffn_kernel(n_mix, x_ref, *rest):
    mix_refs, rest = rest[:n_mix], rest[n_mix:]
    sc_ref, sh_ref, g_ref, gate_ref, w1_ref, w3_ref, w2_ref, o_ref, h_scr, acc_scr, x1_scr = rest
    j = pl.program_id(1)

    @pl.when(j == 0)
    def _():
        x1 = _block_input(x_ref, mix_refs)
        x1_scr[...] = x1
        h_scr[...] = _modnorm(x1, g_ref[...], sc_ref[...], sh_ref[...]).astype(h_scr.dtype)
        acc_scr[...] = jnp.zeros_like(acc_scr)

    h = h_scr[...]
    acc_scr[...] += _wdot(_silu(_wdot(h, w1_ref[...])) * _wdot(h, w3_ref[...]), w2_ref[...])

    @pl.when(j == pl.num_programs(1) - 1)
    def _():
        o_ref[...] = x1_scr[...] + gate_ref[...] * acc_scr[...]


def ffn_dense(x, mix, mod, li, g, w1, w3, w2, tm, tf, rows_per_group):
    m, d = x.shape
    ff = w1.shape[1]
    mix_specs, mix_args = _mix_operands(mix, mod, li, tm, rows_per_group)
    spec = lambda k: _mod_spec(mod, li, k, tm, rows_per_group)
    return pl.pallas_call(
        functools.partial(_ffn_kernel, len(mix_args)),
        grid=(m // tm, ff // tf),
        in_specs=[pl.BlockSpec((tm, d), lambda i, j: (i, 0))] + mix_specs + [
            spec(MOD_SC2), spec(MOD_SH2),
            pl.BlockSpec((1, d), lambda i, j: (0, 0)),
            spec(MOD_G2),
            pl.BlockSpec((d, tf), lambda i, j: (0, j)),
            pl.BlockSpec((d, tf), lambda i, j: (0, j)),
            pl.BlockSpec((tf, d), lambda i, j: (j, 0))],
        out_specs=pl.BlockSpec((tm, d), lambda i, j: (i, 0)),
        out_shape=jax.ShapeDtypeStruct((m, d), F32),
        scratch_shapes=[pltpu.VMEM((tm, d), w1.dtype), pltpu.VMEM((tm, d), F32), pltpu.VMEM((tm, d), F32)],
        compiler_params=_cp(("arbitrary", "arbitrary")),
        name="ffn_dense",
    )(x, *mix_args, mod, mod, g, mod, w1, w3, w2)


def _route_kernel(n_valid, n_mix, x_ref, *rest):
    mix_refs, rest = rest[:n_mix], rest[n_mix:]
    sc_ref, sh_ref, g_ref, rt_ref, tri_ref, x1_ref, h_ref, rank_ref, gate_ref, cnt_ref = rest
    t = x_ref.shape[0]
    x1 = _block_input(x_ref, mix_refs)
    x1_ref[...] = x1
    h = _modnorm(x1, g_ref[...], sc_ref[...], sh_ref[...])
    h_ref[...] = h.astype(BF16)
    logits = jnp.dot(h, rt_ref[...], preferred_element_type=F32, precision=HI).T[:N_EXPERTS]
    e_iota = lax.broadcasted_iota(jnp.int32, logits.shape, 0).astype(F32)
    m1 = jnp.max(logits, axis=0, keepdims=True)
    i1 = jnp.min(jnp.where(logits == m1, e_iota, float(N_EXPERTS)), axis=0, keepdims=True)
    sel1 = e_iota == i1
    rest = jnp.where(sel1, -jnp.inf, logits)
    m2 = jnp.max(rest, axis=0, keepdims=True)
    i2 = jnp.min(jnp.where(rest == m2, e_iota, float(N_EXPERTS)), axis=0, keepdims=True)
    sel2 = e_iota == i2
    e2 = jnp.exp(m2 - m1)
    den = 1.0 + e2
    gate = jnp.where(sel1, 1.0 / den, 0.0) + jnp.where(sel2, e2 / den, 0.0)
    tok = pl.program_id(0) * t + lax.broadcasted_iota(jnp.int32, logits.shape, 1)
    sel = jnp.logical_and(jnp.logical_or(sel1, sel2), tok < n_valid)
    incl = _dot(jnp.where(sel, 1.0, 0.0).astype(BF16), tri_ref[...])
    rank_ref[...] = jnp.where(sel, incl - 1.0, -1.0)
    gate_ref[...] = gate
    cnt_ref[...] = jnp.broadcast_to(incl[:, t - 1:t], cnt_ref.shape)


def _moe_kernel(cnt_ref, h_ref, rank_ref, gate_ref, x_ref, g2_ref, w1_ref, w3_ref, w2_ref, o_ref, xs_scr, y_scr):
    i, e, j = pl.program_id(0), pl.program_id(1), pl.program_id(2)
    nff = pl.num_programs(2)
    t = h_ref.shape[0]
    r = min(MOE_ROWS, t)
    cnt = cnt_ref[i * N_EXPERTS + e]
    nb = (cnt + (r - 1)) // r
    row = lax.broadcasted_iota(jnp.int32, (r, t), 0).astype(F32)

    @pl.when(jnp.logical_and(e == 0, j == 0))
    def _():
        o_ref[...] = jnp.zeros_like(o_ref)
        y_scr[...] = jnp.zeros_like(y_scr)

    @pl.when(j == 0)
    def _():
        def gather(s, c):
            r0 = pl.multiple_of(s * r, r)
            onehot = jnp.where(rank_ref[...] == row + (s * r).astype(F32), 1.0, 0.0).astype(BF16)
            xs_scr[pl.ds(r0, r), :] = _dot(onehot, h_ref[...]).astype(BF16)
            y_scr[pl.ds(r0, r), :] = jnp.zeros((r, y_scr.shape[1]), F32)
            return c
        lax.fori_loop(0, nb, gather, 0)

    def ffn(s, c, rows=r):
        r0 = pl.multiple_of(s * r, r)
        xs = xs_scr[pl.ds(r0, rows), :]
        act = (_silu(_dot(xs, w1_ref[...])) * _dot(xs, w3_ref[...])).astype(BF16)
        y_scr[pl.ds(r0, rows), :] += _dot(act, w2_ref[...])
        return c

    n_full = cnt // r
    rem = cnt - n_full * r
    lax.fori_loop(0, n_full, ffn, 0)
    @pl.when(rem > 0)
    def _():
        ffn(n_full, 0)

    @pl.when(j == nff - 1)
    def _():
        rp = min(max(r, 256), t)
        rowp = lax.broadcasted_iota(jnp.int32, (rp, t), 0).astype(F32)

        def scatter(s, c):
            r0 = pl.multiple_of(s * rp, rp)
            hit = rank_ref[...] == rowp + (s * rp).astype(F32)
            wgt = jnp.where(hit, gate_ref[...], 0.0).astype(BF16)
            o_ref[...] += _dot_tn(wgt, y_scr[pl.ds(r0, rp), :].astype(BF16))
            return c
        lax.fori_loop(0, (nb * r + rp - 1) // rp, scatter, 0)

    @pl.when(jnp.logical_and(e == N_EXPERTS - 1, j == nff - 1))
    def _():
        o_ref[...] = x_ref[...] + g2_ref[...] * o_ref[...]


def moe_ffn(x, mix, mod, li, g, router_t, w1, w3, w2, t, tf, rows_per_group, n_valid):
    m, d = x.shape
    nt = m // t
    ne, _, ff = w1.shape
    assert ne == N_EXPERTS
    tri = jnp.asarray(np.triu(np.ones((t, t), np.float32)), BF16)
    mix_specs, mix_args = _mix_operands(mix, mod, li, t, rows_per_group)
    x, h, rank, gate, cnt = pl.pallas_call(
        functools.partial(_route_kernel, n_valid, len(mix_args)),
        grid=(nt,),
        in_specs=[pl.BlockSpec((t, d), lambda i: (i, 0))] + mix_specs + [
            _mod_spec(mod, li, MOD_SC2, t, rows_per_group), _mod_spec(mod, li, MOD_SH2, t, rows_per_group),
            pl.BlockSpec((1, d), lambda i: (0, 0)),
            pl.BlockSpec((d, 128), lambda i: (0, 0)),
            pl.BlockSpec((t, t), lambda i: (0, 0))],
        out_specs=[pl.BlockSpec((t, d), lambda i: (i, 0)),
                   pl.BlockSpec((t, d), lambda i: (i, 0)),
                   pl.BlockSpec((None, ne, t), lambda i: (i, 0, 0)),
                   pl.BlockSpec((None, ne, t), lambda i: (i, 0, 0)),
                   pl.BlockSpec((None, ne, 128), lambda i: (i, 0, 0))],
        out_shape=[jax.ShapeDtypeStruct((m, d), F32),
                   jax.ShapeDtypeStruct((m, d), BF16),
                   jax.ShapeDtypeStruct((nt, ne, t), F32),
                   jax.ShapeDtypeStruct((nt, ne, t), F32),
                   jax.ShapeDtypeStruct((nt, ne, 128), F32)],
        compiler_params=_cp(("arbitrary",)),
        name="moe_route",
    )(x, *mix_args, mod, mod, g, router_t, tri)
    nblk = cnt[:, :, 0].astype(jnp.int32).reshape(nt * ne)
    rank = rank.reshape(nt, ne, 1, t)
    gate = gate.reshape(nt, ne, 1, t)
    out = pl.pallas_call(
        _moe_kernel,
        grid_spec=pltpu.PrefetchScalarGridSpec(
            num_scalar_prefetch=1,
            grid=(nt, ne, ff // tf),
            in_specs=[pl.BlockSpec((t, d), lambda i, e, j, nb: (i, 0)),
                      pl.BlockSpec((None, None, 1, t), lambda i, e, j, nb: (i, e, 0, 0)),
                      pl.BlockSpec((None, None, 1, t), lambda i, e, j, nb: (i, e, 0, 0)),
                      pl.BlockSpec((t, d), lambda i, e, j, nb: (i, 0)),
                      _mod_spec(mod, li, MOD_G2, t, rows_per_group),
                      pl.BlockSpec((None, d, tf), lambda i, e, j, nb: (e, 0, j)),
                      pl.BlockSpec((None, d, tf), lambda i, e, j, nb: (e, 0, j)),
                      pl.BlockSpec((None, tf, d), lambda i, e, j, nb: (e, j, 0))],
            out_specs=pl.BlockSpec((t, d), lambda i, e, j, nb: (i, 0)),
            scratch_shapes=[pltpu.VMEM((t, d), BF16), pltpu.VMEM((t, d), F32)]),
        out_shape=jax.ShapeDtypeStruct((m, d), F32),
        compiler_params=_cp(("arbitrary", "arbitrary", "arbitrary")),
        name="moe_ffn",
    )(nblk, h, rank, gate, x, mod, w1, w3, w2)
    return out


def _conv_kernel(a_ref, g_ref, w_ref, b_ref, lg_ref, lb_ref, o_ref, cn_ref, ext_scr, sh_scr):
    tc = a_ref.shape[0]
    halo = 32
    off = halo - (CONV_K - 1)

    @pl.when(pl.program_id(1) == 0)
    def _():
        ext_scr[0:halo, :] = jnp.zeros((halo, GW), F32)

    ext_scr[halo:halo + tc, :] = a_ref[...] * jax.nn.sigmoid(g_ref[...])
    for s in range(1, 8):
        sh_scr[s - 1] = ext_scr[s:s + tc + halo - 8, :]
    rc = 64
    for c in range(tc // rc):
        acc = jnp.zeros((rc, GW), F32)
        for k in range(CONV_K):
            s, r0 = (k + off) % 8, c * rc + (k + off) // 8 * 8
            rows = ext_scr[r0:r0 + rc, :] if s == 0 else sh_scr[s - 1, r0:r0 + rc, :]
            acc = acc + w_ref[k:k + 1, :] * rows
        y = acc + b_ref[...]
        mu = jnp.mean(y, axis=-1, keepdims=True)
        yc = y - mu
        var = jnp.mean(yc * yc, axis=-1, keepdims=True)
        o_ref[c * rc:(c + 1) * rc, :] = _silu(yc * lax.rsqrt(var + EPS) * lg_ref[...] + lb_ref[...]).astype(o_ref.dtype)
    cn_ref[...] = ext_scr[tc + off:tc + halo, :]
    ext_scr[0:halo, :] = ext_scr[tc:tc + halo, :]


def conv_prompt(z, nb, l, conv_w, conv_b, ln_g, ln_b, tc):
    m = z.shape[0]
    nl = l // tc
    row = lambda v: v.reshape(1, GW)
    return pl.pallas_call(
        _conv_kernel,
        grid=(nb, nl),
        in_specs=[pl.BlockSpec((tc, GW), lambda b, i: (b * nl + i, G_CA)),
                  pl.BlockSpec((tc, GW), lambda b, i: (b * nl + i, G_CG)),
                  pl.BlockSpec((CONV_K, GW), lambda b, i: (0, 0)),
                  pl.BlockSpec((1, GW), lambda b, i: (0, 0)),
                  pl.BlockSpec((1, GW), lambda b, i: (0, 0)),
                  pl.BlockSpec((1, GW), lambda b, i: (0, 0))],
        out_specs=[pl.BlockSpec((tc, GW), lambda b, i: (b * nl + i, 0)),
                   pl.BlockSpec((None, CONV_K - 1, GW), lambda b, i: (b, 0, 0))],
        out_shape=[jax.ShapeDtypeStruct((m, GW), BF16),
                   jax.ShapeDtypeStruct((nb, CONV_K - 1, GW), F32)],
        scratch_shapes=[pltpu.VMEM((tc + 32, GW), F32), pltpu.VMEM((7, tc + 24, GW), F32)],
        compiler_params=_cp(("arbitrary", "arbitrary")),
        name="conv_prompt",
    )(z, z, conv_w, row(conv_b), row(ln_g), row(ln_b))


def _fox_prep_kernel(q_ref, k_ref, v_ref, t_ref, gq_ref, gk_ref, fb_ref, hs_ref, tri_ref,
                     k_all, v_all, lf_all, qb_ref, kb_ref, vb_ref, kt_ref, vt_ref, lft_ref, fc_ref, carry_scr):
    del k_all, v_all, lf_all

    @pl.when(pl.program_id(1) == 0)
    def _():
        carry_scr[...] = jnp.zeros_like(carry_scr)

    hs = hs_ref[...]
    qn = _head_rms(q_ref[...], hs, gq_ref[...])
    kn = _head_rms(k_ref[...], hs, gk_ref[...])
    v = v_ref[...]
    vt = v.T
    qb_ref[...] = (qn * (HD ** -0.5 * LOG2E)).astype(BF16)
    kb_ref[...] = kn.astype(BF16)
    vb_ref[...] = vt.astype(BF16)
    kt_ref[...] = kn.T
    vt_ref[...] = vt
    lf = _log_sigmoid(t_ref[...] + fb_ref[...])
    lft_ref[...] = lf.T[:NH]
    cum = _dot_sel(tri_ref[...], lf, "b") + carry_scr[...]
    fc_ref[...] = cum * LOG2E
    carry_scr[...] = cum[cum.shape[0] - 1:, :]


def _fox_flash_kernel(q_ref, k_ref, vt_ref, fc_ref, o_ref, ot_scr):
    tq = q_ref.shape[0]
    tk = tq
    qi = pl.program_id(1)
    q = q_ref[...]
    head = _head_ids((1, GW), 1)
    kpos = lax.broadcasted_iota(jnp.int32, (tk, tq), 0)
    qpos = lax.broadcasted_iota(jnp.int32, (tk, tq), 1)
    qhs = [jnp.where(head == h, q, jnp.zeros_like(q)) for h in range(NH)]

    def step(jb, carry, diagonal):
        k0 = pl.multiple_of(jb * tk, tk)
        kb = k_ref[pl.ds(k0, tk), :]
        fcb = fc_ref[pl.ds(k0, tk), :]
        out = []
        for h in range(NH):
            m, l, acc = carry[h]
            s = _dot_nt(kb, qhs[h]) - fcb[:, h:h + 1]
            if diagonal:
                s = jnp.where(kpos <= qpos, s, NEG)
            m_new = jnp.maximum(m, jnp.max(s, axis=0, keepdims=True))
            alpha = jnp.exp2(m - m_new)
            p = jnp.exp2(s - m_new)
            l = alpha * l + jnp.sum(p, axis=0, keepdims=True)
            acc = alpha * acc + _dot(vt_ref[jb, h * HD:(h + 1) * HD, :], p.astype(BF16))
            out.append((m_new, l, acc))
        return tuple(out)

    init = tuple((jnp.full((1, tq), NEG, F32), jnp.zeros((1, tq), F32), jnp.zeros((HD, tq), F32)) for _ in range(NH))
    carry = lax.fori_loop(0, qi, functools.partial(step, diagonal=False), init)
    carry = step(qi, carry, True)
    for h in range(NH):
        _, l, acc = carry[h]
        ot_scr[h * HD:(h + 1) * HD, :] = acc * (1.0 / l)
    o_ref[...] = ot_scr[...].T.astype(o_ref.dtype)


def fox_prompt(z, nb, l, gq, gk, fb, tq, li, k_all, v_all, lf_all):
    m = z.shape[0]
    tp = tq
    nl = l // tp
    hs = jnp.asarray(_np_hsum())
    tri = jnp.asarray(np.tril(np.ones((tp, tp), np.float32)))
    row = lambda v: jnp.tile(v, NH).reshape(1, GW)
    fbp = jnp.zeros((1, TAIL_W), F32).at[0, :NH].set(fb)
    blk = lambda g: pl.BlockSpec((tp, GW), lambda b, i: (b * nl + i, g))
    oblk = lambda w: pl.BlockSpec((tp, w), lambda b, i: (b * nl + i, 0))
    anyspec = pl.BlockSpec(memory_space=pl.ANY)
    stacked = lambda r: pl.BlockSpec((None, None, r, tp), lambda b, i: (b, li, 0, i))
    qb, kb, vb, k_all, v_all, lf_all, fc = pl.pallas_call(
        _fox_prep_kernel,
        grid=(nb, nl),
        in_specs=[blk(G_FQ), blk(G_FK), blk(G_FV),
                  pl.BlockSpec((tp, TAIL_W), lambda b, i: (b * nl + i, TAIL_BLK)),
                  pl.BlockSpec((1, GW), lambda b, i: (0, 0)),
                  pl.BlockSpec((1, GW), lambda b, i: (0, 0)),
                  pl.BlockSpec((1, TAIL_W), lambda b, i: (0, 0)),
                  pl.BlockSpec((GW, GW), lambda b, i: (0, 0)),
                  pl.BlockSpec((tp, tp), lambda b, i: (0, 0)),
                  anyspec, anyspec, anyspec],
        out_specs=[oblk(GW)] * 2 + [pl.BlockSpec((None, None, GW, tp), lambda b, i: (b, i, 0, 0)),
                                    stacked(GW), stacked(GW), stacked(NH), oblk(TAIL_W)],
        out_shape=[jax.ShapeDtypeStruct((m, GW), BF16)] * 2 + [jax.ShapeDtypeStruct((nb, nl, GW, tp), BF16),
                   jax.ShapeDtypeStruct(k_all.shape, F32), jax.ShapeDtypeStruct(v_all.shape, F32),
                   jax.ShapeDtypeStruct(lf_all.shape, F32), jax.ShapeDtypeStruct((m, TAIL_W), F32)],
        input_output_aliases={9: 3, 10: 4, 11: 5},
        scratch_shapes=[pltpu.VMEM((1, TAIL_W), F32)],
        compiler_params=_cp(("arbitrary", "arbitrary")),
        name="fox_prep",
    )(z, z, z, z, row(gq), row(gk), fbp, hs, tri, k_all, v_all, lf_all)
    nq = l // tq
    o = pl.pallas_call(
        _fox_flash_kernel,
        grid=(nb, nq),
        in_specs=[pl.BlockSpec((tq, GW), lambda b, i: (b * nq + i, 0)),
                  pl.BlockSpec((l, GW), lambda b, i: (b, 0)),
                  pl.BlockSpec((None, nq, GW, tq), lambda b, i: (b, 0, 0, 0)),
                  pl.BlockSpec((l, TAIL_W), lambda b, i: (b, 0))],
        out_specs=pl.BlockSpec((tq, GW), lambda b, i: (b * nq + i, 0)),
        out_shape=jax.ShapeDtypeStruct((m, GW), BF16),
        scratch_shapes=[pltpu.VMEM((GW, tq), F32)],
        compiler_params=_cp(("arbitrary", "arbitrary")),
        name="fox_flash",
    )(qb, kb, vb, fc)
    return o, k_all, v_all, lf_all


GLA_SAFE_LOG = -80.0


def _gla_kernel(q_ref, k_ref, v_ref, t_ref, go_ref, w2_ref, bg_ref, gn_ref, hs_ref, cs_ref, bd_ref,
                o_ref, st_ref, st_scr, x_scr, o_scr):
    tg = q_ref.shape[0]
    c = GLA_CHUNK

    @pl.when(pl.program_id(1) == 0)
    def _():
        st_scr[...] = jnp.zeros_like(st_scr)

    hs = hs_ref[...]
    bd = bd_ref[...]
    glog = _log_sigmoid(_dot(t_ref[...].astype(BF16), w2_ref[...]) + bg_ref[...]) * (1.0 / GLA_TAU)
    q = q_ref[...] * (HD ** -0.5)
    k = k_ref[...]
    v = v_ref[...]
    vb = v.astype(BF16)
    lower = lax.broadcasted_iota(jnp.int32, (tg, tg), 0) >= lax.broadcasted_iota(jnp.int32, (tg, tg), 1)
    bfull = _dot_sel(jnp.where(lower, 1.0, 0.0), glog, "b")
    safe = jnp.min(bfull) >= GLA_SAFE_LOG

    @pl.when(safe)
    def _():
        head = _head_ids((1, GW), 1)
        btot = bfull[tg - 1:tg]
        qe = (q * jnp.exp(bfull)).astype(BF16)
        kinv = (k * jnp.exp(-bfull)).astype(BF16)
        kend = (k * jnp.exp(btot - bfull)).astype(BF16)
        o = _dot_nt(qe, st_scr[...].astype(BF16))
        for h in range(NH):
            s = _dot_nt(jnp.where(head == h, qe, jnp.zeros_like(qe)), kinv)
            o = o + jnp.where(head == h, _dot(jnp.where(lower, s, 0.0).astype(BF16), vb), 0.0)
        o_scr[...] = o
        st_scr[...] = st_scr[...] * jnp.exp(btot) + _dot_tn(vb, kend) * bd

    @pl.when(jnp.logical_not(safe))
    def _():
        hs_b = hs.astype(BF16)
        cs = cs_ref[...]
        bcum = _dot_sel(jnp.where(lower, cs, 0.0), glog, "b")
        blast = _dot_sel(cs, glog, "b")
        qe = (q * jnp.exp(bcum)).astype(BF16)
        ke = (k * jnp.exp(blast - bcum)).astype(BF16)
        ii = lax.broadcasted_iota(jnp.int32, (c, GW), 0)
        for n in range(tg // c):
            r0 = n * c
            bc = bcum[r0:r0 + c]
            qc = q[r0:r0 + c]
            kc = k[r0:r0 + c]
            vc = v[r0:r0 + c]
            for j in range(c):
                ex = jnp.exp(jnp.where(ii >= j, bc - bc[j:j + 1], -jnp.inf))
                x_scr[j * c:(j + 1) * c, :] = (qc * ex * kc[j:j + 1]).astype(BF16)
            att = _dot(x_scr[...], hs_b)
            o = _dot_nt(qe[r0:r0 + c], st_scr[...].astype(BF16))
            for j in range(c):
                o = o + att[j * c:(j + 1) * c] * vc[j:j + 1]
            o_scr[r0:r0 + c, :] = o
            kv = _dot_tn(vb[r0:r0 + c], ke[r0:r0 + c])
            st_scr[...] = st_scr[...] * jnp.exp(blast[r0:r0 + 1]) + kv * bd

    o_ref[...] = (_head_rms(o_scr[...], hs, gn_ref[...]) * _silu(go_ref[...])).astype(o_ref.dtype)
    st_ref[...] = st_scr[...]


def gla_prompt(z, nb, l, w_gate2, b_gate, norm_g, tg):
    m = z.shape[0]
    nl = l // tg
    hs = jnp.asarray(_np_hsum())
    ch = np.arange(tg) // GLA_CHUNK
    cs = jnp.asarray((ch[:, None] == ch[None, :]).astype(np.float32))
    w2p = jnp.zeros((TAIL_W, GW), F32).at[NH:NH + GLA_LR].set(w_gate2).astype(BF16)
    blk = lambda g: pl.BlockSpec((tg, GW), lambda b, i: (b * nl + i, g))
    full = lambda r, c: pl.BlockSpec((r, c), lambda b, i: (0, 0))
    return pl.pallas_call(
        _gla_kernel,
        grid=(nb, nl),
        in_specs=[blk(G_GQ), blk(G_GK), blk(G_GV),
                  pl.BlockSpec((tg, TAIL_W), lambda b, i: (b * nl + i, TAIL_BLK)),
                  blk(G_GO), full(TAIL_W, GW), full(1, GW), full(1, GW), full(GW, GW), full(tg, tg), full(GW, GW)],
        out_specs=[pl.BlockSpec((tg, GW), lambda b, i: (b * nl + i, 0)),
                   pl.BlockSpec((None, GW, GW), lambda b, i: (b, 0, 0))],
        out_shape=[jax.ShapeDtypeStruct((m, GW), BF16), jax.ShapeDtypeStruct((nb, GW, GW), F32)],
        scratch_shapes=[pltpu.VMEM((GW, GW), F32), pltpu.VMEM((GLA_CHUNK * GLA_CHUNK, GW), BF16),
                        pltpu.VMEM((tg, GW), F32)],
        compiler_params=_cp(("arbitrary", "arbitrary")),
        name="gla_prompt",
    )(z, z, z, z, z, w2p, b_gate.reshape(1, GW), norm_g.reshape(1, GW), hs, cs, hs)


def _ret_lg_row():
    lg = np.log(1.0 - np.exp2(-5.0 - np.arange(NH, dtype=np.float32))).astype(np.float32)
    return np.repeat(lg, HD).reshape(1, GW)


def _ret_decay_mask(c):
    lg = _ret_lg_row()[0, ::HD]
    i = np.arange(c, dtype=np.float32)
    rel = i[:, None] - i[None, :]
    return np.where(rel >= 0, np.exp(np.maximum(rel, 0.0)[None] * lg[:, None, None]), 0.0).astype(np.float32)


def _rope(x, cos, sin_signed):
    first = (lax.broadcasted_iota(jnp.int32, (1, GW), 1) & (HD - 1)) < (HD // 2)
    swapped = jnp.where(first, pltpu.roll(x, GW - HD // 2, axis=1), pltpu.roll(x, HD // 2, axis=1))
    return x * cos + swapped * sin_signed


def _ret_kernel(q_ref, k_ref, v_ref, g_ref, cos_ref, sin_ref, lg_ref, gn_ref, hs_ref, dm_ref, o_ref, st_ref, st_scr):
    tr = dm_ref.shape[1]

    @pl.when(pl.program_id(1) == 0)
    def _():
        st_scr[...] = jnp.zeros_like(st_scr)

    hs = hs_ref[...]
    lg = lg_ref[...]
    head = _head_ids((1, GW), 1)
    ri = lax.broadcasted_iota(jnp.int32, (tr, 1), 0).astype(F32)
    dq = jnp.exp((ri + 1.0) * lg)
    dk = jnp.exp((tr - 1.0 - ri) * lg)
    ds = jnp.exp(tr * lg)
    for n in range(q_ref.shape[0] // tr):
        rows = slice(n * tr, (n + 1) * tr)
        cos = cos_ref[rows, :]
        sin = sin_ref[rows, :]
        q = _rope(q_ref[rows, :], cos, sin)
        k = _rope(k_ref[rows, :], cos, sin) * (HD ** -0.5)
        qb = q.astype(BF16)
        kb = k.astype(BF16)
        vb = v_ref[rows, :].astype(BF16)
        o = _dot_nt((q * dq).astype(BF16), st_scr[...].astype(BF16))
        for h in range(NH):
            att = _dot_nt(jnp.where(head == h, qb, jnp.zeros_like(qb)), kb) * dm_ref[h]
            o = o + jnp.where(head == h, _dot(att.astype(BF16), vb), 0.0)
        st_scr[...] = st_scr[...] * ds + _dot_tn(vb, (k * dk).astype(BF16)) * hs
        o_ref[rows, :] = (_head_rms(o, hs, gn_ref[...]) * _silu(g_ref[rows, :])).astype(o_ref.dtype)
    st_ref[...] = st_scr[...]


def _rope_tables(pos):
    half = HD // 2
    inv = 10000.0 ** (-jnp.arange(half, dtype=F32) / half)
    ang = pos[:, None] * inv[None, :]
    cos = jnp.cos(ang)
    sin = jnp.sin(ang)
    cos_t = jnp.tile(jnp.concatenate([cos, cos], axis=1), (1, NH))
    sin_t = jnp.tile(jnp.concatenate([-sin, sin], axis=1), (1, NH))
    return cos_t, sin_t


def ret_prompt(z, nb, l, norm_g, tt):
    m = z.shape[0]
    nl = l // tt
    tr = RET_CHUNK
    hs = jnp.asarray(_np_hsum())
    cos_t, sin_t = _rope_tables(jnp.arange(l, dtype=F32))
    blk = lambda g: pl.BlockSpec((tt, GW), lambda b, i: (b * nl + i, g))
    full = lambda r, c: pl.BlockSpec((r, c), lambda b, i: (0, 0))
    tab = pl.BlockSpec((tt, GW), lambda b, i: (i, 0))
    return pl.pallas_call(
        _ret_kernel,
        grid=(nb, nl),
        in_specs=[blk(G_RQ), blk(G_RK), blk(G_RV), blk(G_RG), tab, tab, full(1, GW), full(1, GW), full(GW, GW),
                  pl.BlockSpec((NH, tr, tr), lambda b, i: (0, 0, 0))],
        out_specs=[pl.BlockSpec((tt, GW), lambda b, i: (b * nl + i, 0)),
                   pl.BlockSpec((None, GW, GW), lambda b, i: (b, 0, 0))],
        out_shape=[jax.ShapeDtypeStruct((m, GW), BF16), jax.ShapeDtypeStruct((nb, GW, GW), F32)],
        scratch_shapes=[pltpu.VMEM((GW, GW), F32)],
        compiler_params=_cp(("arbitrary", "arbitrary")),
        name="ret_prompt",
    )(z, z, z, z, cos_t, sin_t, jnp.asarray(_ret_lg_row()), norm_g.reshape(1, GW), hs, jnp.asarray(_ret_decay_mask(tr)))


def _state_from_blockdiag(st):
    nb = st.shape[0]
    s5 = st.reshape(nb, NH, HD, NH, HD)
    diag = jnp.stack([s5[:, h, :, h, :] for h in range(NH)], axis=1)
    return diag.transpose(0, 1, 3, 2)


def _sample_rows_kernel(z_ref, buf_ref, cw_ref, cb_ref, lg_ref, lb_ref, gq_ref, gk_ref, fb_ref, w2_ref, bg_ref,
                        cos_ref, sin_ref, rlg_ref, hs_ref,
                        oa_ref, cn_ref, lf_ref, col_ref):
    grp = lambda g: z_ref[:, g * GW:(g + 1) * GW]
    tail = z_ref[:, N_MAIN:N_MAIN + TAIL_W]
    hs = hs_ref[...]
    u = grp(G_CA) * jax.nn.sigmoid(grp(G_CG))
    y = cw_ref[CONV_K - 1:CONV_K, :] * u + cb_ref[...]
    for k in range(CONV_K - 1):
        y = y + cw_ref[k:k + 1, :] * buf_ref[k]
    mu = jnp.mean(y, axis=-1, keepdims=True)
    yc = y - mu
    var = jnp.mean(yc * yc, axis=-1, keepdims=True)
    oa_ref[...] = _silu(yc * lax.rsqrt(var + EPS) * lg_ref[...] + lb_ref[...]).astype(oa_ref.dtype)
    for k in range(CONV_K - 2):
        cn_ref[k] = buf_ref[k + 1]
    cn_ref[CONV_K - 2] = u
    col_ref[0] = _head_rms(grp(G_FQ), hs, gq_ref[...]) * (HD ** -0.5)
    col_ref[1] = _head_rms(grp(G_FK), hs, gk_ref[...])
    col_ref[2] = grp(G_FV)
    lf_ref[...] = _log_sigmoid(tail + fb_ref[...])
    glog = _log_sigmoid(_wdot(tail, w2_ref[...]) + bg_ref[...]) * (1.0 / GLA_TAU)
    col_ref[3] = grp(G_GQ) * (HD ** -0.5)
    col_ref[4] = grp(G_GK)
    col_ref[5] = jnp.exp(glog)
    col_ref[6] = _rope(grp(G_RQ), cos_ref[...], sin_ref[...])
    col_ref[7] = _rope(grp(G_RK), cos_ref[...], sin_ref[...]) * (HD ** -0.5)
    col_ref[8] = jnp.broadcast_to(jnp.exp(rlg_ref[...]), col_ref.shape[1:])


def _rec_step_kernel(c_ref, v_ref, s0_ref, s1_ref, o_ref, sn0_ref, sn1_ref):
    for r, (s_ref, sn_ref) in enumerate(((s0_ref, sn0_ref), (s1_ref, sn1_ref))):
        for b in range(v_ref.shape[1]):
            for h in range(NH):
                q = c_ref[3 * r, b, h]
                k = c_ref[3 * r + 1, b, h]
                a = c_ref[3 * r + 2, b, h]
                v = v_ref[r, b, h]
                s = s_ref[b, h]
                qk = jnp.sum(q * k, axis=0, keepdims=True)
                o_ref[r, b, h] = qk * v + jnp.sum((q * a) * s, axis=0, keepdims=True)
                sn_ref[b, h] = a * s + k * v


REC_SEQS = 8


def rec_step(cols, v, state_gla, state_ret, li):
    nb = cols.shape[1]
    bt = math.gcd(nb, REC_SEQS)
    sspec = pl.BlockSpec((None, bt, NH, HD, HD), lambda b: (li, b, 0, 0, 0))
    ospec = pl.BlockSpec((bt, NH, HD, HD), lambda b: (b, 0, 0, 0))
    st = jax.ShapeDtypeStruct((nb, NH, HD, HD), F32)
    o, sn0, sn1 = pl.pallas_call(
        _rec_step_kernel,
        grid=(nb // bt,),
        in_specs=[pl.BlockSpec((6, bt, NH, HD, 1), lambda b: (0, b, 0, 0, 0)),
                  pl.BlockSpec((2, bt, NH, 1, HD), lambda b: (0, b, 0, 0, 0)), sspec, sspec],
        out_specs=[pl.BlockSpec((2, bt, NH, 1, HD), lambda b: (0, b, 0, 0, 0)), ospec, ospec],
        out_shape=[jax.ShapeDtypeStruct((2, nb, NH, 1, HD), F32), st, st],
        compiler_params=_cp(("arbitrary",)),
        name="rec_step",
    )(cols.reshape(6, nb, NH, HD, 1), v.reshape(2, nb, NH, 1, HD), state_gla, state_ret)
    o = o.reshape(2, nb, GW)
    return o[0], sn0, o[1], sn1


FOX_PAGES = 64


def _fox_bias_kernel(pt_ref, lfn_ref, lf_ref, ts_ref, pre_ref, o_ref, lf_scr):
    b = pl.program_id(0)
    n_pages = o_ref.shape[1]
    pg = o_ref.shape[2]
    for p in range(n_pages):
        page = lf_ref[pt_ref[b * n_pages + p]]
        for h in range(NH):
            lf_scr[h * n_pages + p:h * n_pages + p + 1, :] = page[h:h + 1, :]
    both = _dot_sel(lf_scr[...], ts_ref[...], "a")
    suf = both[:, :pg]
    tot = both[:, pg:]
    later = _dot_sel(pre_ref[...], tot, "b")
    for h in range(NH):
        r0, r1 = h * n_pages, (h + 1) * n_pages
        o_ref[h] = lfn_ref[h:h + 1, :] + later[r0:r1] + suf[r0:r1]


def _fox_sample_kernel(pt_ref, q_ref, kn_ref, vn_ref, bias_ref, *rest):
    g_n = bias_ref.shape[1]
    k_refs = rest[:g_n]
    v_refs = rest[g_n:2 * g_n]
    o_ref, m_scr, l_scr, acc_scr, s_scr = rest[2 * g_n:]
    c = pl.program_id(1)
    pg = k_refs[0].shape[1]

    @pl.when(c == 0)
    def _():
        m_scr[...] = jnp.full(m_scr.shape, NEG, F32)
        l_scr[...] = jnp.zeros_like(l_scr)
        acc_scr[...] = jnp.zeros_like(acc_scr)

    qb = jnp.broadcast_to(q_ref[...], (GW, pg))
    for g in range(g_n):
        prod = k_refs[g][...] * qb
        for h in range(NH):
            s_scr[h * g_n + g:h * g_n + g + 1, :] = jnp.sum(prod[h * HD:(h + 1) * HD], axis=0, keepdims=True)
    for h in range(NH):
        s_h = s_scr[h * g_n:(h + 1) * g_n, :] + bias_ref[h]
        m_old = m_scr[h:h + 1, :]
        m_new = jnp.maximum(m_old, jnp.max(jnp.max(s_h, axis=0, keepdims=True), axis=1, keepdims=True))
        alpha = jnp.exp(m_old - m_new)
        p = jnp.exp(s_h - m_new)
        m_scr[h:h + 1, :] = m_new
        l_scr[h:h + 1, :] = l_scr[h:h + 1, :] * alpha + jnp.sum(p, axis=0, keepdims=True)
        acc = acc_scr[h * HD:(h + 1) * HD, :] * alpha
        for g in range(g_n):
            acc = acc + p[g:g + 1, :] * v_refs[g][h * HD:(h + 1) * HD, :]
        acc_scr[h * HD:(h + 1) * HD, :] = acc

    @pl.when(c == pl.num_programs(1) - 1)
    def _():
        prod = q_ref[...] * kn_ref[...]
        for h in range(NH):
            s_self = jnp.sum(prod[h * HD:(h + 1) * HD], axis=0, keepdims=True)
            m_h = m_scr[h:h + 1, 0:1]
            m_fin = jnp.maximum(m_h, s_self)
            a_h = jnp.exp(m_h - m_fin)
            p_self = jnp.exp(s_self - m_fin)
            l_tot = jnp.sum(l_scr[h:h + 1, :], axis=1, keepdims=True) * a_h + p_self
            num = jnp.sum(acc_scr[h * HD:(h + 1) * HD, :], axis=1, keepdims=True) * a_h \
                + p_self * vn_ref[h * HD:(h + 1) * HD, :]
            o_ref[h * HD:(h + 1) * HD, :] = num / l_tot


def fox_sample(qkv, lf_new, cache_kt, cache_vt, cache_lf, page_table, li):
    nb, n_pages = page_table.shape
    n_phys = cache_lf.shape[0]
    pg = cache_kt.shape[-1]
    g_n = math.gcd(FOX_PAGES, n_pages)
    nc = n_pages // g_n
    qkv = qkv.reshape(3, nb, GW, 1)
    pt = page_table.reshape(-1)
    lfn = jnp.broadcast_to(lf_new[:, :, None], (nb, NH, pg))
    t = np.arange(pg)
    ts = jnp.asarray(np.concatenate([(t[:, None] > t[None, :]).astype(np.float32), np.ones((pg, pg), np.float32)], axis=1))
    r = np.arange(NH * n_pages)
    later = jnp.asarray(((r[:, None] // n_pages == r[None, :] // n_pages) & (r[None, :] > r[:, None])).astype(np.float32))
    bias = pl.pallas_call(
        _fox_bias_kernel,
        grid_spec=pltpu.PrefetchScalarGridSpec(
            num_scalar_prefetch=1,
            grid=(nb,),
            in_specs=[pl.BlockSpec((None, NH, pg), lambda b, pt: (b, 0, 0)),
                      pl.BlockSpec((n_phys, None, NH, pg), lambda b, pt: (0, li, 0, 0)),
                      pl.BlockSpec((pg, 2 * pg), lambda b, pt: (0, 0)),
                      pl.BlockSpec((NH * n_pages, NH * n_pages), lambda b, pt: (0, 0))],
            out_specs=pl.BlockSpec((None, NH, n_pages, pg), lambda b, pt: (b, 0, 0, 0)),
            scratch_shapes=[pltpu.VMEM((NH * n_pages, pg), F32)]),
        out_shape=jax.ShapeDtypeStruct((nb, NH, n_pages, pg), F32),
        compiler_params=_cp(("arbitrary",)),
        name="fox_bias",
    )(pt, lfn, cache_lf, ts, later)

    def page(g):
        return lambda b, c, pt: (pt[b * n_pages + c * g_n + g], li, 0, 0)

    cspec = lambda r: pl.BlockSpec((None, None, GW, 1), lambda b, c, pt: (r, b, 0, 0))
    in_specs = [cspec(0), cspec(1), cspec(2), pl.BlockSpec((None, NH, g_n, pg), lambda b, c, pt: (b, 0, c, 0))]
    in_specs += [pl.BlockSpec((None, None, GW, pg), page(g)) for g in range(g_n)]
    in_specs += [pl.BlockSpec((None, None, GW, pg), page(g)) for g in range(g_n)]
    o = pl.pallas_call(
        _fox_sample_kernel,
        grid_spec=pltpu.PrefetchScalarGridSpec(
            num_scalar_prefetch=1,
            grid=(nb, nc),
            in_specs=in_specs,
            out_specs=pl.BlockSpec((None, GW, 1), lambda b, c, pt: (b, 0, 0)),
            scratch_shapes=[pltpu.VMEM((NH, pg), F32), pltpu.VMEM((NH, pg), F32),
                            pltpu.VMEM((GW, pg), F32), pltpu.VMEM((NH * g_n, pg), F32)]),
        out_shape=jax.ShapeDtypeStruct((nb, GW, 1), F32),
        compiler_params=_cp(("arbitrary", "arbitrary")),
        name="fox_sample",
    )(pt, qkv, qkv, qkv, bias, *([cache_kt] * g_n), *([cache_vt] * g_n))
    return o.reshape(nb, GW)


def _sample_out_kernel(oa_ref, ob_ref, oc_ref, od_ref, go_ref, rg_ref, gng_ref, rng_ref, hs_ref, w_ref, x_ref, g1_ref, o_ref):
    hs = hs_ref[...]
    oc = _head_rms(oc_ref[...], hs, gng_ref[...]) * _silu(go_ref[...])
    od = _head_rms(od_ref[...], hs, rng_ref[...]) * _silu(rg_ref[...])
    cat = jnp.concatenate([oa_ref[...], ob_ref[...], oc, od], axis=1)
    o_ref[...] = x_ref[...] + g1_ref[...] * _wdot(cat, w_ref[...])


def _prep_w_in(w_in, li):
    wt = w_in.transpose(2, 0, 1)[:, li, :]
    o_ff = 5 * GW
    o_lr = o_ff + NH + 3 * GW
    return jnp.concatenate([wt[:o_ff], wt[o_ff + NH:o_lr], wt[o_lr + GLA_LR:],
                            wt[o_ff:o_ff + NH], wt[o_lr:o_lr + GLA_LR],
                            jnp.zeros((TAIL_W - NH - GLA_LR, wt.shape[1]), wt.dtype)], axis=0)


def _ffn_apply(li, x, mix, mod, g, fw, tm, rows_per_group, n_valid):
    if li % 2 == 0:
        w1, w3, w2 = fw
        return ffn_dense(x, mix, mod, li, g, w1, w3, w2, tm, w1.shape[1] // 2, rows_per_group)
    router_t, w1, w3, w2 = fw
    return moe_ffn(x, mix, mod, li, g, router_t, w1, w3, w2, tm, w1.shape[2] // 2, rows_per_group, n_valid)


def kernel(x_prompt, x_sample, c_prompt, c_sample, state_conv, cache_k, cache_v, cache_logf, state_gla, state_ret, page_table, w_in, w_out, conv_w, conv_b, conv_ln_g, conv_ln_b, fox_qn_g, fox_kn_g, fox_fb, gla_w_gate2, gla_b_gate, gla_norm_g, ret_norm_g, norm1_g, norm2_g, w_ada, b_ada, ffn_w1, ffn_w3, ffn_w2, moe_router, moe_w1, moe_w3, moe_w2):
    nbp, l, d = x_prompt.shape
    nbs = x_sample.shape[0]
    depth = w_in.shape[0]
    n_pages, pg = page_table.shape[1], cache_k.shape[2]
    p_len = n_pages * pg
    mp = nbp * l
    ms = -(-nbs // 8) * 8
    ms_moe = 128

    mod = ada_mod(jnp.concatenate([c_prompt, c_sample], axis=0), w_ada, b_ada)
    mod = mod.reshape(depth, nbp + nbs, 6, d).transpose(0, 2, 1, 3)
    mod_p = mod[:, :, :nbp].reshape(depth, 6, nbp, 1, d)
    mod_s = jnp.zeros((depth, 6, ms_moe, d), F32).at[:, :, :nbs].set(mod[:, :, nbp:])
    cache_kt = cache_k.transpose(0, 1, 3, 4, 2).reshape(cache_k.shape[0], depth, GW, pg)
    cache_vt = cache_v.transpose(0, 1, 3, 4, 2).reshape(cache_v.shape[0], depth, GW, pg)
    cache_lf = cache_logf.transpose(0, 1, 3, 2)
    conv_state = state_conv.transpose(0, 2, 1, 3)
    cos_s, sin_s = _rope_tables(jnp.full((1,), p_len, F32))
    hs = jnp.asarray(_np_hsum())
    rlg = jnp.asarray(_ret_lg_row())

    xp = x_prompt.reshape(mp, d)
    xs = jnp.zeros((ms, d), F32).at[:nbs].set(x_sample.reshape(nbs, d))
    outs_p, outs_s = [], []
    k_all = jnp.zeros((nbp, depth, GW, l), F32)
    v_all = jnp.zeros((nbp, depth, GW, l), F32)
    lf_all = jnp.zeros((nbp, depth, NH, l), F32)
    for li in range(depth):
        n1 =norm1_g[li].reshape(1, d)
        n2 = norm2_g[li].reshape(1, d)
        w_in_f = _prep_w_in(w_in, li)
        w_in_b = w_in_f.astype(BF16)
        w_out_f = w_out[li]
        w_out_b = w_out_f.astype(BF16)
        if li % 2 == 0:
            fw_s = (ffn_w1[li // 2], ffn_w3[li // 2], ffn_w2[li // 2])
            fw = tuple(w.astype(BF16) for w in fw_s)
        else:
            fw = (jnp.pad(moe_router[li // 2], ((0, 0), (0, 128 - N_EXPERTS))), moe_w1[li // 2].astype(BF16),
                  moe_w3[li // 2].astype(BF16), moe_w2[li // 2].astype(BF16))
            fw_s = fw

        z = in_proj(xp, mod_p, li, n1, w_in_b, 512, l)
        out_a, conv_new = conv_prompt(z, nbp, l, conv_w[li], conv_b[li], conv_ln_g[li], conv_ln_b[li], 512)
        out_b, k_all, v_all, lf_all = fox_prompt(z, nbp, l, fox_qn_g[li], fox_kn_g[li], fox_fb[li], 512, li,
                                                 k_all, v_all, lf_all)
        out_c, gla_st = gla_prompt(z, nbp, l, gla_w_gate2[li], gla_b_gate[li], gla_norm_g[li], 256)
        out_d, ret_st = ret_prompt(z, nbp, l, ret_norm_g[li], 512)
        xp = _ffn_apply(li, xp, ((out_a, out_b, out_c, out_d), w_out_b), mod_p, n2, fw,
                        512 if li % 2 == 0 else 1024, l, mp)
        outs_p.append((conv_new, None, None, None, _state_from_blockdiag(gla_st), _state_from_blockdiag(ret_st)))

        zs = in_proj(xs, mod_s, li, n1, w_in_f, ms, ms)[:nbs]
        row = lambda v_: v_.reshape(1, GW)
        w2p = jnp.zeros((TAIL_W, GW), F32).at[NH:NH + GLA_LR].set(gla_w_gate2[li])
        fbp = jnp.zeros((1, TAIL_W), F32).at[0, :NH].set(fox_fb[li])
        r32 = jax.ShapeDtypeStruct((nbs, GW), F32)
        oa, cn, lfs, cols = pl.pallas_call(
            _sample_rows_kernel,
            out_shape=[r32, jax.ShapeDtypeStruct((CONV_K - 1, nbs, GW), F32),
                       jax.ShapeDtypeStruct((nbs, TAIL_W), F32), jax.ShapeDtypeStruct((9, nbs, GW), F32)],
            compiler_params=pltpu.CompilerParams(vmem_limit_bytes=VMEM_LIMIT),
            name="sample_rows",
        )(zs, conv_state[li], conv_w[li], row(conv_b[li]), row(conv_ln_g[li]), row(conv_ln_b[li]),
          row(jnp.tile(fox_qn_g[li], NH)), row(jnp.tile(fox_kn_g[li], NH)), fbp, w2p, row(gla_b_gate[li]),
          cos_s, sin_s, rlg, hs)
        grp = lambda g_: zs[:, g_ * GW:(g_ + 1) * GW]
        fk, f_v = cols[1], cols[2]
        ob = fox_sample(cols[:3], lfs[:, :NH], cache_kt, cache_vt, cache_lf, page_table, li)
        oc, gla_new, od, ret_new = rec_step(cols[3:], jnp.stack([grp(G_GV), grp(G_RV)]), state_gla, state_ret, li)
        pad = lambda v_: jnp.zeros((ms, v_.shape[1]), F32).at[:nbs].set(v_)
        xs = pl.pallas_call(
            _sample_out_kernel,
            out_shape=jax.ShapeDtypeStruct((ms, d), F32),
            compiler_params=pltpu.CompilerParams(vmem_limit_bytes=VMEM_LIMIT),
            name="sample_out",
        )(pad(oa), pad(ob), pad(oc), pad(od), pad(grp(G_GO)), pad(grp(G_RG)), row(gla_norm_g[li]), row(ret_norm_g[li]),
          hs, w_out_f, xs, mod_s[li, MOD_G1, :ms])
        if li % 2 == 0:
            xs = _ffn_apply(li, xs, None, mod_s, n2, fw_s, ms, ms, nbs)
        else:
            wide = jnp.zeros((ms_moe, d), F32).at[:ms].set(xs)
            xs = _ffn_apply(li, wide, None, mod_s, n2, fw_s, ms_moe, ms_moe, nbs)[:ms]
        outs_s.append((cn.transpose(1, 0, 2), fk.reshape(nbs, 1, NH, HD), f_v.reshape(nbs, 1, NH, HD),
                       lfs[:, :NH].reshape(nbs, 1, NH), gla_new, ret_new))

    st = lambda lst, i, ax: jnp.stack([s[i] for s in lst], axis=ax)
    heads_last = lambda a: a.reshape(nbp, depth, NH, HD, l).transpose(0, 1, 4, 2, 3)
    return (xp.reshape(nbp, l, d), xs[:nbs].reshape(nbs, 1, d),
            st(outs_p, 0, 0), st(outs_s, 0, 0),
            heads_last(k_all), st(outs_s, 1, 1),
            heads_last(v_all), st(outs_s, 2, 1),
            lf_all.transpose(0, 1, 3, 2), st(outs_s, 3, 1),
            st(outs_p, 4, 0), st(outs_s, 4, 0),
            st(outs_p, 5, 0), st(outs_s, 5, 0))
```

```python
import functools
import math

import numpy as np
import jax
import jax.numpy as jnp
from jax import lax
from jax.experimental import pallas as pl
from jax.experimental.pallas import tpu as pltpu

F32 = jnp.float32
BF16 = jnp.bfloat16
HI = lax.Precision.HIGHEST

D_MODEL = 1024
NH = 4
HD = 64
GW = NH * HD
CONV_K = 31
GLA_LR = 16
GLA_TAU = 16.0
GLA_CHUNK = 16
RET_CHUNK = 128
N_EXPERTS = 8
EPS = 1e-6
N_MAIN = 13 * GW
TAIL_W = 128
IN_PAD = N_MAIN + TAIL_W
TAIL_BLK = N_MAIN // TAIL_W
(G_CA, G_CG, G_FQ, G_FK, G_FV, G_GQ, G_GK, G_GV, G_GO, G_RQ, G_RK, G_RV, G_RG) = range(13)
VMEM_LIMIT = 56 * 1024 * 1024
MOE_ROWS = 128
NEG = -1e30
LOG2E = math.log2(math.e)


def _cp(sem, vmem=VMEM_LIMIT):
    return pltpu.CompilerParams(dimension_semantics=sem, vmem_limit_bytes=vmem)


def _silu(x):
    return x * jax.nn.sigmoid(x)


def _log_sigmoid(x):
    return jnp.minimum(x, 0.0) - jnp.log1p(jnp.exp(-jnp.abs(x)))


def _dot(a, b):
    return jnp.dot(a, b, preferred_element_type=F32)


def _split3(x):
    hi = x.astype(BF16)
    r1 = x - hi.astype(F32)
    mid = r1.astype(BF16)
    lo = (r1 - mid.astype(F32)).astype(BF16)
    return hi, mid, lo


def _dot_sel(a, b, data):
    if data == "a":
        sel = b.astype(BF16)
        return sum(_dot(p, sel) for p in _split3(a))
    sel = a.astype(BF16)
    return sum(_dot(sel, p) for p in _split3(b))


def _head_ids(shape, dim):
    return lax.shift_right_logical(lax.broadcasted_iota(jnp.int32, shape, dim), int(math.log2(HD)))


def _dot_nt(a, b, **kw):
    return lax.dot_general(a, b, (((1,), (1,)), ((), ())), preferred_element_type=F32, **kw)


def _dot_tn(a, b, **kw):
    return lax.dot_general(a, b, (((0,), (0,)), ((), ())), preferred_element_type=F32, **kw)


def _wdot(a, w, nt=False):
    if w.dtype == BF16:
        a, kw = a.astype(BF16), {}
    else:
        kw = {"precision": HI}
    return _dot_nt(a, w, **kw) if nt else jnp.dot(a, w, preferred_element_type=F32, **kw)


def _modnorm(x, g, sc, sh):
    y = x * lax.rsqrt(jnp.mean(x * x, axis=-1, keepdims=True) + EPS)
    return (y * g) * (1.0 + sc) + sh


def _head_rms(x, hsum, g):
    ms = _dot_sel(x * x, hsum, "a") * (1.0 / HD)
    return x * lax.rsqrt(ms + EPS) * g


def _np_hsum():
    i = np.arange(GW) // HD
    return (i[:, None] == i[None, :]).astype(np.float32)


def _ada_kernel(c_ref, w_ref, b_ref, o_ref):
    c = c_ref[...]
    o_ref[...] = _wdot(_silu(c), w_ref[...]) + b_ref[...]


def ada_mod(c_all, w_ada, b_ada):
    depth, d, n = w_ada.shape
    r = c_all.shape[0]
    tn = 1536
    return pl.pallas_call(
        _ada_kernel,
        grid=(depth, n // tn),
        in_specs=[pl.BlockSpec((r, d), lambda l, j: (0, 0)),
                  pl.BlockSpec((None, d, tn), lambda l, j: (l, 0, j)),
                  pl.BlockSpec((None, 1, tn), lambda l, j: (l, 0, j))],
        out_specs=pl.BlockSpec((None, r, tn), lambda l, j: (l, 0, j)),
        out_shape=jax.ShapeDtypeStruct((depth, r, n), F32),
        compiler_params=_cp(("arbitrary", "arbitrary")),
        name="ada_mod",
    )(c_all, w_ada, b_ada.reshape(depth, 1, n))


def _in_proj_kernel(x_ref, sc_ref, sh_ref, g_ref, w_ref, z_ref):
    h = _modnorm(x_ref[...], g_ref[...], sc_ref[...], sh_ref[...])
    z_ref[...] = _wdot(h, w_ref[...], nt=True)


MOD_SH1, MOD_SC1, MOD_G1, MOD_SH2, MOD_SC2, MOD_G2 = range(6)


def _mod_spec(mod, li, k, tm, rows_per_group):
    if mod.ndim == 5:
        tiles = rows_per_group // tm
        return pl.BlockSpec((None, None, None, 1, mod.shape[-1]), lambda i, *_: (li, k, i // tiles, 0, 0))
    return pl.BlockSpec((None, None, tm, mod.shape[-1]), lambda i, *_: (li, k, 0, 0))


def in_proj(x, mod, li, g, w, tm, rows_per_group):
    m, d = x.shape
    n = w.shape[0]
    return pl.pallas_call(
        _in_proj_kernel,
        grid=(m // tm,),
        in_specs=[pl.BlockSpec((tm, d), lambda i: (i, 0)),
                  _mod_spec(mod, li, MOD_SC1, tm, rows_per_group), _mod_spec(mod, li, MOD_SH1, tm, rows_per_group),
                  pl.BlockSpec((1, d), lambda i: (0, 0)),
                  pl.BlockSpec((n, d), lambda i: (0, 0))],
        out_specs=pl.BlockSpec((tm, n), lambda i: (i, 0)),
        out_shape=jax.ShapeDtypeStruct((m, n), F32),
        compiler_params=_cp(("arbitrary",)),
        name="in_proj",
    )(x, mod, mod, g, w)


def _block_input(x_ref, mix_refs):
    x = x_ref[...]
    if mix_refs:
        *parts, w_ref, g1_ref = mix_refs
        x = x + g1_ref[...] * _dot(jnp.concatenate([p[...] for p in parts], axis=1), w_ref[...])
    return x


def _mix_operands(mix, mod, li, tm, rows_per_group):
    if mix is None:
        return [], []
    parts, w_out = mix
    specs = [pl.BlockSpec((tm, GW), lambda i, *_: (i, 0))] * 4 + [
        pl.BlockSpec(w_out.shape, lambda i, *_: (0, 0)), _mod_spec(mod, li, MOD_G1, tm, rows_per_group)]
    return specs, [*parts, w_out, mod]


def _out_proj_kernel(x_ref, *rest):
    rest[-1][...] = _block_input(x_ref, rest[:-1])


def out_proj(x, mix, mod, li, tm, rows_per_group):
    m, d = x.shape
    mix_specs, mix_args = _mix_operands(mix, mod, li, tm, rows_per_group)
    return pl.pallas_call(
        _out_proj_kernel,
        grid=(m // tm,),
        in_specs=[pl.BlockSpec((tm, d), lambda i: (i, 0))] + mix_specs,
        out_specs=pl.BlockSpec((tm, d), lambda i: (i, 0)),
        out_shape=jax.ShapeDtypeStruct((m, d), F32),
        compiler_params=_cp(("arbitrary",)),
        name="out_proj",
    )(x, *mix_args)


def _ffn_kernel(n_mix, x_ref, *rest):
    mix_refs, rest = rest[:n_mix], rest[n_mix:]
    sc_ref, sh_ref, g_ref, gate_ref, w1_ref, w3_ref, w2_ref, o_ref, h_scr, acc_scr, x1_scr = rest
    j = pl.program_id(1)

    @pl.when(j == 0)
    def _():
        x1 = _block_input(x_ref, mix_refs)
        x1_scr[...] = x1
        h_scr[...] = _modnorm(x1, g_ref[...], sc_ref[...], sh_ref[...]).astype(h_scr.dtype)
        acc_scr[...] = jnp.zeros_like(acc_scr)

    h = h_scr[...]
    acc_scr[...] += _wdot(_silu(_wdot(h, w1_ref[...])) * _wdot(h, w3_ref[...]), w2_ref[...])

    @pl.when(j == pl.num_programs(1) - 1)
    def _():
        o_ref[...] = x1_scr[...] + gate_ref[...] * acc_scr[...]


def ffn_dense(x, mix, mod, li, g, w1, w3, w2, tm, tf, rows_per_group):
    m, d = x.shape
    ff = w1.shape[1]
    mix_specs, mix_args = _mix_operands(mix, mod, li, tm, rows_per_group)
    spec = lambda k: _mod_spec(mod, li, k, tm, rows_per_group)
    return pl.pallas_call(
        functools.partial(_ffn_kernel, len(mix_args)),
        grid=(m // tm, ff // tf),
        in_specs=[pl.BlockSpec((tm, d), lambda i, j: (i, 0))] + mix_specs + [
            spec(MOD_SC2), spec(MOD_SH2),
            pl.BlockSpec((1, d), lambda i, j: (0, 0)),
            spec(MOD_G2),
            pl.BlockSpec((d, tf), lambda i, j: (0, j)),
            pl.BlockSpec((d, tf), lambda i, j: (0, j)),
            pl.BlockSpec((tf, d), lambda i, j: (j, 0))],
        out_specs=pl.BlockSpec((tm, d), lambda i, j: (i, 0)),
        out_shape=jax.ShapeDtypeStruct((m, d), F32),
        scratch_shapes=[pltpu.VMEM((tm, d), w1.dtype), pltpu.VMEM((tm, d), F32), pltpu.VMEM((tm, d), F32)],
        compiler_params=_cp(("arbitrary", "arbitrary")),
        name="ffn_dense",
    )(x, *mix_args, mod, mod, g, mod, w1, w3, w2)


def _route_kernel(n_valid, x_ref, sc_ref, sh_ref, g_ref, rt_ref, tri_ref, h_ref, rank_ref, gate_ref, cnt_ref):
    t = x_ref.shape[0]
    h = _modnorm(x_ref[...], g_ref[...], sc_ref[...], sh_ref[...])
    h_ref[...] = h.astype(BF16)
    logits = jnp.dot(h, rt_ref[...], preferred_element_type=F32, precision=HI).T[:N_EXPERTS]
    e_iota = lax.broadcasted_iota(jnp.int32, logits.shape, 0).astype(F32)
    m1 = jnp.max(logits, axis=0, keepdims=True)
    i1 = jnp.min(jnp.where(logits == m1, e_iota, float(N_EXPERTS)), axis=0, keepdims=True)
    sel1 = e_iota == i1
    rest = jnp.where(sel1, -jnp.inf, logits)
    m2 = jnp.max(rest, axis=0, keepdims=True)
    i2 = jnp.min(jnp.where(rest == m2, e_iota, float(N_EXPERTS)), axis=0, keepdims=True)
    sel2 = e_iota == i2
    e2 = jnp.exp(m2 - m1)
    den = 1.0 + e2
    gate = jnp.where(sel1, 1.0 / den, 0.0) + jnp.where(sel2, e2 / den, 0.0)
    tok = pl.program_id(0) * t + lax.broadcasted_iota(jnp.int32, logits.shape, 1)
    sel = jnp.logical_and(jnp.logical_or(sel1, sel2), tok < n_valid)
    incl = _dot(jnp.where(sel, 1.0, 0.0).astype(BF16), tri_ref[...])
    rank_ref[...] = jnp.where(sel, incl - 1.0, -1.0)
    gate_ref[...] = gate
    cnt_ref[...] = jnp.broadcast_to(incl[:, t - 1:t], cnt_ref.shape)


def _moe_kernel(cnt_ref, h_ref, rank_ref, gate_ref, x_ref, g2_ref, w1_ref, w3_ref, w2_ref, o_ref, xs_scr, y_scr):
    i, e, j = pl.program_id(0), pl.program_id(1), pl.program_id(2)
    nff = pl.num_programs(2)
    t = h_ref.shape[0]
    r = min(MOE_ROWS, t)
    cnt = cnt_ref[i * N_EXPERTS + e]
    nb = (cnt + (r - 1)) // r
    row = lax.broadcasted_iota(jnp.int32, (r, t), 0).astype(F32)

    @pl.when(jnp.logical_and(e == 0, j == 0))
    def _():
        o_ref[...] = jnp.zeros_like(o_ref)
        y_scr[...] = jnp.zeros_like(y_scr)

    @pl.when(j == 0)
    def _():
        def gather(s, c):
            r0 = pl.multiple_of(s * r, r)
            onehot = jnp.where(rank_ref[...] == row + (s * r).astype(F32), 1.0, 0.0).astype(BF16)
            xs_scr[pl.ds(r0, r), :] = _dot(onehot, h_ref[...]).astype(BF16)
            y_scr[pl.ds(r0, r), :] = jnp.zeros((r, y_scr.shape[1]), F32)
            return c
        lax.fori_loop(0, nb, gather, 0)

    def ffn(s, c, rows=r):
        r0 = pl.multiple_of(s * r, r)
        xs = xs_scr[pl.ds(r0, rows), :]
        act = (_silu(_dot(xs, w1_ref[...])) * _dot(xs, w3_ref[...])).astype(BF16)
        y_scr[pl.ds(r0, rows), :] += _dot(act, w2_ref[...])
        return c

    n_full = cnt // r
    rem = cnt - n_full * r
    lax.fori_loop(0, n_full, ffn, 0)
    @pl.when(rem > 0)
    def _():
        ffn(n_full, 0)

    @pl.when(j == nff - 1)
    def _():
        rp = min(max(r, 256), t)
        rowp = lax.broadcasted_iota(jnp.int32, (rp, t), 0).astype(F32)

        def scatter(s, c):
            r0 = pl.multiple_of(s * rp, rp)
            hit = rank_ref[...] == rowp + (s * rp).astype(F32)
            wgt = jnp.where(hit, gate_ref[...], 0.0).astype(BF16)
            o_ref[...] += _dot_tn(wgt, y_scr[pl.ds(r0, rp), :].astype(BF16))
            return c
        lax.fori_loop(0, (nb * r + rp - 1) // rp, scatter, 0)

    @pl.when(jnp.logical_and(e == N_EXPERTS - 1, j == nff - 1))
    def _():
        o_ref[...] = x_ref[...] + g2_ref[...] * o_ref[...]


def moe_ffn(x, mod, li, g, router_t, w1, w3, w2, t, tf, rows_per_group, n_valid):
    m, d = x.shape
    nt = m // t
    ne, _, ff = w1.shape
    assert ne == N_EXPERTS
    tri = jnp.asarray(np.triu(np.ones((t, t), np.float32)), BF16)
    h, rank, gate, cnt = pl.pallas_call(
        functools.partial(_route_kernel, n_valid),
        grid=(nt,),
        in_specs=[pl.BlockSpec((t, d), lambda i: (i, 0)),
                  _mod_spec(mod, li, MOD_SC2, t, rows_per_group), _mod_spec(mod, li, MOD_SH2, t, rows_per_group),
                  pl.BlockSpec((1, d), lambda i: (0, 0)),
                  pl.BlockSpec((d, 128), lambda i: (0, 0)),
                  pl.BlockSpec((t, t), lambda i: (0, 0))],
        out_specs=[pl.BlockSpec((t, d), lambda i: (i, 0)),
                   pl.BlockSpec((None, ne, t), lambda i: (i, 0, 0)),
                   pl.BlockSpec((None, ne, t), lambda i: (i, 0, 0)),
                   pl.BlockSpec((None, ne, 128), lambda i: (i, 0, 0))],
        out_shape=[jax.ShapeDtypeStruct((m, d), BF16),
                   jax.ShapeDtypeStruct((nt, ne, t), F32),
                   jax.ShapeDtypeStruct((nt, ne, t), F32),
                   jax.ShapeDtypeStruct((nt, ne, 128), F32)],
        compiler_params=_cp(("arbitrary",)),
        name="moe_route",
    )(x, mod, mod, g, router_t, tri)
    nblk = cnt[:, :, 0].astype(jnp.int32).reshape(nt * ne)
    rank = rank.reshape(nt, ne, 1, t)
    gate = gate.reshape(nt, ne, 1, t)
    out = pl.pallas_call(
        _moe_kernel,
        grid_spec=pltpu.PrefetchScalarGridSpec(
            num_scalar_prefetch=1,
            grid=(nt, ne, ff // tf),
            in_specs=[pl.BlockSpec((t, d), lambda i, e, j, nb: (i, 0)),
                      pl.BlockSpec((None, None, 1, t), lambda i, e, j, nb: (i, e, 0, 0)),
                      pl.BlockSpec((None, None, 1, t), lambda i, e, j, nb: (i, e, 0, 0)),
                      pl.BlockSpec((t, d), lambda i, e, j, nb: (i, 0)),
                      _mod_spec(mod, li, MOD_G2, t, rows_per_group),
                      pl.BlockSpec((None, d, tf), lambda i, e, j, nb: (e, 0, j)),
                      pl.BlockSpec((None, d, tf), lambda i, e, j, nb: (e, 0, j)),
                      pl.BlockSpec((None, tf, d), lambda i, e, j, nb: (e, j, 0))],
            out_specs=pl.BlockSpec((t, d), lambda i, e, j, nb: (i, 0)),
            scratch_shapes=[pltpu.VMEM((t, d), BF16), pltpu.VMEM((t, d), F32)]),
        out_shape=jax.ShapeDtypeStruct((m, d), F32),
        compiler_params=_cp(("arbitrary", "arbitrary", "arbitrary")),
        name="moe_ffn",
    )(nblk, h, rank, gate, x, mod, w1, w3, w2)
    return out


def _conv_kernel(a_ref, g_ref, w_ref, b_ref, lg_ref, lb_ref, o_ref, cn_ref, ext_scr, sh_scr):
    tc = a_ref.shape[0]
    halo = 32
    off = halo - (CONV_K - 1)

    @pl.when(pl.program_id(1) == 0)
    def _():
        ext_scr[0:halo, :] = jnp.zeros((halo, GW), F32)

    ext_scr[halo:halo + tc, :] = a_ref[...] * jax.nn.sigmoid(g_ref[...])
    for s in range(1, 8):
        sh_scr[s - 1] = ext_scr[s:s + tc + halo - 8, :]
    rc = 64
    for c in range(tc // rc):
        acc = jnp.zeros((rc, GW), F32)
        for k in range(CONV_K):
            s, r0 = (k + off) % 8, c * rc + (k + off) // 8 * 8
            rows = ext_scr[r0:r0 + rc, :] if s == 0 else sh_scr[s - 1, r0:r0 + rc, :]
            acc = acc + w_ref[k:k + 1, :] * rows
        y = acc + b_ref[...]
        mu = jnp.mean(y, axis=-1, keepdims=True)
        yc = y - mu
        var = jnp.mean(yc * yc, axis=-1, keepdims=True)
        o_ref[c * rc:(c + 1) * rc, :] = _silu(yc * lax.rsqrt(var + EPS) * lg_ref[...] + lb_ref[...]).astype(o_ref.dtype)
    cn_ref[...] = ext_scr[tc + off:tc + halo, :]
    ext_scr[0:halo, :] = ext_scr[tc:tc + halo, :]


def conv_prompt(z, nb, l, conv_w, conv_b, ln_g, ln_b, tc):
    m = z.shape[0]
    nl = l // tc
    row = lambda v: v.reshape(1, GW)
    return pl.pallas_call(
        _conv_kernel,
        grid=(nb, nl),
        in_specs=[pl.BlockSpec((tc, GW), lambda b, i: (b * nl + i, G_CA)),
                  pl.BlockSpec((tc, GW), lambda b, i: (b * nl + i, G_CG)),
                  pl.BlockSpec((CONV_K, GW), lambda b, i: (0, 0)),
                  pl.BlockSpec((1, GW), lambda b, i: (0, 0)),
                  pl.BlockSpec((1, GW), lambda b, i: (0, 0)),
                  pl.BlockSpec((1, GW), lambda b, i: (0, 0))],
        out_specs=[pl.BlockSpec((tc, GW), lambda b, i: (b * nl + i, 0)),
                   pl.BlockSpec((None, CONV_K - 1, GW), lambda b, i: (b, 0, 0))],
        out_shape=[jax.ShapeDtypeStruct((m, GW), BF16),
                   jax.ShapeDtypeStruct((nb, CONV_K - 1, GW), F32)],
        scratch_shapes=[pltpu.VMEM((tc + 32, GW), F32), pltpu.VMEM((7, tc + 24, GW), F32)],
        compiler_params=_cp(("arbitrary", "arbitrary")),
        name="conv_prompt",
    )(z, z, conv_w, row(conv_b), row(ln_g), row(ln_b))


def _fox_prep_kernel(q_ref, k_ref, v_ref, t_ref, gq_ref, gk_ref, fb_ref, hs_ref, tri_ref,
                     k_all, v_all, lf_all, qb_ref, kb_ref, vb_ref, kt_ref, vt_ref, lft_ref, fc_ref, carry_scr):
    del k_all, v_all, lf_all

    @pl.when(pl.program_id(1) == 0)
    def _():
        carry_scr[...] = jnp.zeros_like(carry_scr)

    hs = hs_ref[...]
    qn = _head_rms(q_ref[...], hs, gq_ref[...])
    kn = _head_rms(k_ref[...], hs, gk_ref[...])
    v = v_ref[...]
    vt = v.T
    qb_ref[...] = (qn * (HD ** -0.5 * LOG2E)).astype(BF16)
    kb_ref[...] = kn.astype(BF16)
    vb_ref[...] = vt.astype(BF16)
    kt_ref[...] = kn.T
    vt_ref[...] = vt
    lf = _log_sigmoid(t_ref[...] + fb_ref[...])
    lft_ref[...] = lf.T[:NH]
    cum = _dot_sel(tri_ref[...], lf, "b") + carry_scr[...]
    fc_ref[...] = cum * LOG2E
    carry_scr[...] = cum[cum.shape[0] - 1:, :]


def _fox_flash_kernel(q_ref, k_ref, vt_ref, fc_ref, o_ref, ot_scr):
    tq = q_ref.shape[0]
    tk = tq
    qi = pl.program_id(1)
    q = q_ref[...]
    head = _head_ids((1, GW), 1)
    kpos = lax.broadcasted_iota(jnp.int32, (tk, tq), 0)
    qpos = lax.broadcasted_iota(jnp.int32, (tk, tq), 1)
    qhs = [jnp.where(head == h, q, jnp.zeros_like(q)) for h in range(NH)]

    def step(jb, carry, diagonal):
        k0 = pl.multiple_of(jb * tk, tk)
        kb = k_ref[pl.ds(k0, tk), :]
        fcb = fc_ref[pl.ds(k0, tk), :]
        out = []
        for h in range(NH):
            m, l, acc = carry[h]
            s = _dot_nt(kb, qhs[h]) - fcb[:, h:h + 1]
            if diagonal:
                s = jnp.where(kpos <= qpos, s, NEG)
            m_new = jnp.maximum(m, jnp.max(s, axis=0, keepdims=True))
            alpha = jnp.exp2(m - m_new)
            p = jnp.exp2(s - m_new)
            l = alpha * l + jnp.sum(p, axis=0, keepdims=True)
            acc = alpha * acc + _dot(vt_ref[jb, h * HD:(h + 1) * HD, :], p.astype(BF16))
            out.append((m_new, l, acc))
        return tuple(out)

    init = tuple((jnp.full((1, tq), NEG, F32), jnp.zeros((1, tq), F32), jnp.zeros((HD, tq), F32)) for _ in range(NH))
    carry = lax.fori_loop(0, qi, functools.partial(step, diagonal=False), init)
    carry = step(qi, carry, True)
    for h in range(NH):
        _, l, acc = carry[h]
        ot_scr[h * HD:(h + 1) * HD, :] = acc * (1.0 / l)
    o_ref[...] = ot_scr[...].T.astype(o_ref.dtype)


def fox_prompt(z, nb, l, gq, gk, fb, tq, li, k_all, v_all, lf_all):
    m = z.shape[0]
    tp = tq
    nl = l // tp
    hs = jnp.asarray(_np_hsum())
    tri = jnp.asarray(np.tril(np.ones((tp, tp), np.float32)))
    row = lambda v: jnp.tile(v, NH).reshape(1, GW)
    fbp = jnp.zeros((1, TAIL_W), F32).at[0, :NH].set(fb)
    blk = lambda g: pl.BlockSpec((tp, GW), lambda b, i: (b * nl + i, g))
    oblk = lambda w: pl.BlockSpec((tp, w), lambda b, i: (b * nl + i, 0))
    anyspec = pl.BlockSpec(memory_space=pl.ANY)
    stacked = lambda r: pl.BlockSpec((None, None, r, tp), lambda b, i: (b, li, 0, i))
    qb, kb, vb, k_all, v_all, lf_all, fc = pl.pallas_call(
        _fox_prep_kernel,
        grid=(nb, nl),
        in_specs=[blk(G_FQ), blk(G_FK), blk(G_FV),
                  pl.BlockSpec((tp, TAIL_W), lambda b, i: (b * nl + i, TAIL_BLK)),
                  pl.BlockSpec((1, GW), lambda b, i: (0, 0)),
                  pl.BlockSpec((1, GW), lambda b, i: (0, 0)),
                  pl.BlockSpec((1, TAIL_W), lambda b, i: (0, 0)),
                  pl.BlockSpec((GW, GW), lambda b, i: (0, 0)),
                  pl.BlockSpec((tp, tp), lambda b, i: (0, 0)),
                  anyspec, anyspec, anyspec],
        out_specs=[oblk(GW)] * 2 + [pl.BlockSpec((None, None, GW, tp), lambda b, i: (b, i, 0, 0)),
                                    stacked(GW), stacked(GW), stacked(NH), oblk(TAIL_W)],
        out_shape=[jax.ShapeDtypeStruct((m, GW), BF16)] * 2 + [jax.ShapeDtypeStruct((nb, nl, GW, tp), BF16),
                   jax.ShapeDtypeStruct(k_all.shape, F32), jax.ShapeDtypeStruct(v_all.shape, F32),
                   jax.ShapeDtypeStruct(lf_all.shape, F32), jax.ShapeDtypeStruct((m, TAIL_W), F32)],
        input_output_aliases={9: 3, 10: 4, 11: 5},
        scratch_shapes=[pltpu.VMEM((1, TAIL_W), F32)],
        compiler_params=_cp(("arbitrary", "arbitrary")),
        name="fox_prep",
    )(z, z, z, z, row(gq), row(gk), fbp, hs, tri, k_all, v_all, lf_all)
    nq = l // tq
    o = pl.pallas_call(
        _fox_flash_kernel,
        grid=(nb, nq),
        in_specs=[pl.BlockSpec((tq, GW), lambda b, i: (b * nq + i, 0)),
                  pl.BlockSpec((l, GW), lambda b, i: (b, 0)),
                  pl.BlockSpec((None, nq, GW, tq), lambda b, i: (b, 0, 0, 0)),
                  pl.BlockSpec((l, TAIL_W), lambda b, i: (b, 0))],
        out_specs=pl.BlockSpec((tq, GW), lambda b, i: (b * nq + i, 0)),
        out_shape=jax.ShapeDtypeStruct((m, GW), BF16),
        scratch_shapes=[pltpu.VMEM((GW, tq), F32)],
        compiler_params=_cp(("arbitrary", "arbitrary")),
        name="fox_flash",
    )(qb, kb, vb, fc)
    return o, k_all, v_all, lf_all


GLA_SAFE_LOG = -80.0


def _gla_kernel(q_ref, k_ref, v_ref, t_ref, go_ref, w2_ref, bg_ref, gn_ref, hs_ref, cs_ref, bd_ref,
                o_ref, st_ref, st_scr, x_scr, o_scr):
    tg = q_ref.shape[0]
    c = GLA_CHUNK

    @pl.when(pl.program_id(1) == 0)
    def _():
        st_scr[...] = jnp.zeros_like(st_scr)

    hs = hs_ref[...]
    bd = bd_ref[...]
    glog = _log_sigmoid(_dot(t_ref[...].astype(BF16), w2_ref[...]) + bg_ref[...]) * (1.0 / GLA_TAU)
    q = q_ref[...] * (HD ** -0.5)
    k = k_ref[...]
    v = v_ref[...]
    vb = v.astype(BF16)
    lower = lax.broadcasted_iota(jnp.int32, (tg, tg), 0) >= lax.broadcasted_iota(jnp.int32, (tg, tg), 1)
    bfull = _dot_sel(jnp.where(lower, 1.0, 0.0), glog, "b")
    safe = jnp.min(bfull) >= GLA_SAFE_LOG

    @pl.when(safe)
    def _():
        head = _head_ids((1, GW), 1)
        btot = bfull[tg - 1:tg]
        qe = (q * jnp.exp(bfull)).astype(BF16)
        kinv = (k * jnp.exp(-bfull)).astype(BF16)
        kend = (k * jnp.exp(btot - bfull)).astype(BF16)
        o = _dot_nt(qe, st_scr[...].astype(BF16))
        for h in range(NH):
            s = _dot_nt(jnp.where(head == h, qe, jnp.zeros_like(qe)), kinv)
            o = o + jnp.where(head == h, _dot(jnp.where(lower, s, 0.0).astype(BF16), vb), 0.0)
        o_scr[...] = o
        st_scr[...] = st_scr[...] * jnp.exp(btot) + _dot_tn(vb, kend) * bd

    @pl.when(jnp.logical_not(safe))
    def _():
        hs_b = hs.astype(BF16)
        cs = cs_ref[...]
        bcum = _dot_sel(jnp.where(lower, cs, 0.0), glog, "b")
        blast = _dot_sel(cs, glog, "b")
        qe = (q * jnp.exp(bcum)).astype(BF16)
        ke = (k * jnp.exp(blast - bcum)).astype(BF16)
        ii = lax.broadcasted_iota(jnp.int32, (c, GW), 0)
        for n in range(tg // c):
            r0 = n * c
            bc = bcum[r0:r0 + c]
            qc = q[r0:r0 + c]
            kc = k[r0:r0 + c]
            vc = v[r0:r0 + c]
            for j in range(c):
                ex = jnp.exp(jnp.where(ii >= j, bc - bc[j:j + 1], -jnp.inf))
                x_scr[j * c:(j + 1) * c, :] = (qc * ex * kc[j:j + 1]).astype(BF16)
            att = _dot(x_scr[...], hs_b)
            o = _dot_nt(qe[r0:r0 + c], st_scr[...].astype(BF16))
            for j in range(c):
                o = o + att[j * c:(j + 1) * c] * vc[j:j + 1]
            o_scr[r0:r0 + c, :] = o
            kv = _dot_tn(vb[r0:r0 + c], ke[r0:r0 + c])
            st_scr[...] = st_scr[...] * jnp.exp(blast[r0:r0 + 1]) + kv * bd

    o_ref[...] = (_head_rms(o_scr[...], hs, gn_ref[...]) * _silu(go_ref[...])).astype(o_ref.dtype)
    st_ref[...] = st_scr[...]


def gla_prompt(z, nb, l, w_gate2, b_gate, norm_g, tg):
    m = z.shape[0]
    nl = l // tg
    hs = jnp.asarray(_np_hsum())
    ch = np.arange(tg) // GLA_CHUNK
    cs = jnp.asarray((ch[:, None] == ch[None, :]).astype(np.float32))
    w2p = jnp.zeros((TAIL_W, GW), F32).at[NH:NH + GLA_LR].set(w_gate2).astype(BF16)
    blk = lambda g: pl.BlockSpec((tg, GW), lambda b, i: (b * nl + i, g))
    full = lambda r, c: pl.BlockSpec((r, c), lambda b, i: (0, 0))
    return pl.pallas_call(
        _gla_kernel,
        grid=(nb, nl),
        in_specs=[blk(G_GQ), blk(G_GK), blk(G_GV),
                  pl.BlockSpec((tg, TAIL_W), lambda b, i: (b * nl + i, TAIL_BLK)),
                  blk(G_GO), full(TAIL_W, GW), full(1, GW), full(1, GW), full(GW, GW), full(tg, tg), full(GW, GW)],
        out_specs=[pl.BlockSpec((tg, GW), lambda b, i: (b * nl + i, 0)),
                   pl.BlockSpec((None, GW, GW), lambda b, i: (b, 0, 0))],
        out_shape=[jax.ShapeDtypeStruct((m, GW), BF16), jax.ShapeDtypeStruct((nb, GW, GW), F32)],
        scratch_shapes=[pltpu.VMEM((GW, GW), F32), pltpu.VMEM((GLA_CHUNK * GLA_CHUNK, GW), BF16),
                        pltpu.VMEM((tg, GW), F32)],
        compiler_params=_cp(("arbitrary", "arbitrary")),
        name="gla_prompt",
    )(z, z, z, z, z, w2p, b_gate.reshape(1, GW), norm_g.reshape(1, GW), hs, cs, hs)


def _ret_lg_row():
    lg = np.log(1.0 - np.exp2(-5.0 - np.arange(NH, dtype=np.float32))).astype(np.float32)
    return np.repeat(lg, HD).reshape(1, GW)


def _ret_decay_mask(c):
    lg = _ret_lg_row()[0, ::HD]
    i = np.arange(c, dtype=np.float32)
    rel = i[:, None] - i[None, :]
    return np.where(rel >= 0, np.exp(np.maximum(rel, 0.0)[None] * lg[:, None, None]), 0.0).astype(np.float32)


def _rope(x, cos, sin_signed):
    first = (lax.broadcasted_iota(jnp.int32, (1, GW), 1) & (HD - 1)) < (HD // 2)
    swapped = jnp.where(first, pltpu.roll(x, GW - HD // 2, axis=1), pltpu.roll(x, HD // 2, axis=1))
    return x * cos + swapped * sin_signed


def _ret_kernel(q_ref, k_ref, v_ref, g_ref, cos_ref, sin_ref, lg_ref, gn_ref, hs_ref, dm_ref, o_ref, st_ref, st_scr):
    tr = dm_ref.shape[1]

    @pl.when(pl.program_id(1) == 0)
    def _():
        st_scr[...] = jnp.zeros_like(st_scr)

    hs = hs_ref[...]
    lg = lg_ref[...]
    head = _head_ids((1, GW), 1)
    ri = lax.broadcasted_iota(jnp.int32, (tr, 1), 0).astype(F32)
    dq = jnp.exp((ri + 1.0) * lg)
    dk = jnp.exp((tr - 1.0 - ri) * lg)
    ds = jnp.exp(tr * lg)
    for n in range(q_ref.shape[0] // tr):
        rows = slice(n * tr, (n + 1) * tr)
        cos = cos_ref[rows, :]
        sin = sin_ref[rows, :]
        q = _rope(q_ref[rows, :], cos, sin)
        k = _rope(k_ref[rows, :], cos, sin) * (HD ** -0.5)
        qb = q.astype(BF16)
        kb = k.astype(BF16)
        vb = v_ref[rows, :].astype(BF16)
        o = _dot_nt((q * dq).astype(BF16), st_scr[...].astype(BF16))
        for h in range(NH):
            att = _dot_nt(jnp.where(head == h, qb, jnp.zeros_like(qb)), kb) * dm_ref[h]
            o = o + jnp.where(head == h, _dot(att.astype(BF16), vb), 0.0)
        st_scr[...] = st_scr[...] * ds + _dot_tn(vb, (k * dk).astype(BF16)) * hs
        o_ref[rows, :] = (_head_rms(o, hs, gn_ref[...]) * _silu(g_ref[rows, :])).astype(o_ref.dtype)
    st_ref[...] = st_scr[...]


def _rope_tables(pos):
    half = HD // 2
    inv = 10000.0 ** (-jnp.arange(half, dtype=F32) / half)
    ang = pos[:, None] * inv[None, :]
    cos = jnp.cos(ang)
    sin = jnp.sin(ang)
    cos_t = jnp.tile(jnp.concatenate([cos, cos], axis=1), (1, NH))
    sin_t = jnp.tile(jnp.concatenate([-sin, sin], axis=1), (1, NH))
    return cos_t, sin_t


def ret_prompt(z, nb, l, norm_g, tt):
    m = z.shape[0]
    nl = l // tt
    tr = RET_CHUNK
    hs = jnp.asarray(_np_hsum())
    cos_t, sin_t = _rope_tables(jnp.arange(l, dtype=F32))
    blk = lambda g: pl.BlockSpec((tt, GW), lambda b, i: (b * nl + i, g))
    full = lambda r, c: pl.BlockSpec((r, c), lambda b, i: (0, 0))
    tab = pl.BlockSpec((tt, GW), lambda b, i: (i, 0))
    return pl.pallas_call(
        _ret_kernel,
        grid=(nb, nl),
        in_specs=[blk(G_RQ), blk(G_RK), blk(G_RV), blk(G_RG), tab, tab, full(1, GW), full(1, GW), full(GW, GW),
                  pl.BlockSpec((NH, tr, tr), lambda b, i: (0, 0, 0))],
        out_specs=[pl.BlockSpec((tt, GW), lambda b, i: (b * nl + i, 0)),
                   pl.BlockSpec((None, GW, GW), lambda b, i: (b, 0, 0))],
        out_shape=[jax.ShapeDtypeStruct((m, GW), BF16), jax.ShapeDtypeStruct((nb, GW, GW), F32)],
        scratch_shapes=[pltpu.VMEM((GW, GW), F32)],
        compiler_params=_cp(("arbitrary", "arbitrary")),
        name="ret_prompt",
    )(z, z, z, z, cos_t, sin_t, jnp.asarray(_ret_lg_row()), norm_g.reshape(1, GW), hs, jnp.asarray(_ret_decay_mask(tr)))


def _state_from_blockdiag(st):
    nb = st.shape[0]
    s5 = st.reshape(nb, NH, HD, NH, HD)
    diag = jnp.stack([s5[:, h, :, h, :] for h in range(NH)], axis=1)
    return diag.transpose(0, 1, 3, 2)


def _sample_rows_kernel(z_ref, buf_ref, cw_ref, cb_ref, lg_ref, lb_ref, gq_ref, gk_ref, fb_ref, w2_ref, bg_ref,
                        cos_ref, sin_ref, rlg_ref, hs_ref,
                        oa_ref, cn_ref, lf_ref, col_ref):
    grp = lambda g: z_ref[:, g * GW:(g + 1) * GW]
    tail = z_ref[:, N_MAIN:N_MAIN + TAIL_W]
    hs = hs_ref[...]
    u = grp(G_CA) * jax.nn.sigmoid(grp(G_CG))
    y = cw_ref[CONV_K - 1:CONV_K, :] * u + cb_ref[...]
    for k in range(CONV_K - 1):
        y = y + cw_ref[k:k + 1, :] * buf_ref[k]
    mu = jnp.mean(y, axis=-1, keepdims=True)
    yc = y - mu
    var = jnp.mean(yc * yc, axis=-1, keepdims=True)
    oa_ref[...] = _silu(yc * lax.rsqrt(var + EPS) * lg_ref[...] + lb_ref[...]).astype(oa_ref.dtype)
    for k in range(CONV_K - 2):
        cn_ref[k] = buf_ref[k + 1]
    cn_ref[CONV_K - 2] = u
    col_ref[0] = _head_rms(grp(G_FQ), hs, gq_ref[...]) * (HD ** -0.5)
    col_ref[1] = _head_rms(grp(G_FK), hs, gk_ref[...])
    col_ref[2] = grp(G_FV)
    lf_ref[...] = _log_sigmoid(tail + fb_ref[...])
    glog = _log_sigmoid(_wdot(tail, w2_ref[...]) + bg_ref[...]) * (1.0 / GLA_TAU)
    col_ref[3] = grp(G_GQ) * (HD ** -0.5)
    col_ref[4] = grp(G_GK)
    col_ref[5] = jnp.exp(glog)
    col_ref[6] = _rope(grp(G_RQ), cos_ref[...], sin_ref[...])
    col_ref[7] = _rope(grp(G_RK), cos_ref[...], sin_ref[...]) * (HD ** -0.5)
    col_ref[8] = jnp.broadcast_to(jnp.exp(rlg_ref[...]), col_ref.shape[1:])


def _rec_step_kernel(c_ref, v_ref, s0_ref, s1_ref, o_ref, sn0_ref, sn1_ref):
    for r, (s_ref, sn_ref) in enumerate(((s0_ref, sn0_ref), (s1_ref, sn1_ref))):
        for b in range(v_ref.shape[1]):
            for h in range(NH):
                q = c_ref[3 * r, b, h]
                k = c_ref[3 * r + 1, b, h]
                a = c_ref[3 * r + 2, b, h]
                v = v_ref[r, b, h]
                s = s_ref[b, h]
                qk = jnp.sum(q * k, axis=0, keepdims=True)
                o_ref[r, b, h] = qk * v + jnp.sum((q * a) * s, axis=0, keepdims=True)
                sn_ref[b, h] = a * s + k * v


REC_SEQS = 8


def rec_step(cols, v, state_gla, state_ret, li):
    nb = cols.shape[1]
    bt = math.gcd(nb, REC_SEQS)
    sspec = pl.BlockSpec((None, bt, NH, HD, HD), lambda b: (li, b, 0, 0, 0))
    ospec = pl.BlockSpec((bt, NH, HD, HD), lambda b: (b, 0, 0, 0))
    st = jax.ShapeDtypeStruct((nb, NH, HD, HD), F32)
    o, sn0, sn1 = pl.pallas_call(
        _rec_step_kernel,
        grid=(nb // bt,),
        in_specs=[pl.BlockSpec((6, bt, NH, HD, 1), lambda b: (0, b, 0, 0, 0)),
                  pl.BlockSpec((2, bt, NH, 1, HD), lambda b: (0, b, 0, 0, 0)), sspec, sspec],
        out_specs=[pl.BlockSpec((2, bt, NH, 1, HD), lambda b: (0, b, 0, 0, 0)), ospec, ospec],
        out_shape=[jax.ShapeDtypeStruct((2, nb, NH, 1, HD), F32), st, st],
        compiler_params=_cp(("arbitrary",)),
        name="rec_step",
    )(cols.reshape(6, nb, NH, HD, 1), v.reshape(2, nb, NH, 1, HD), state_gla, state_ret)
    o = o.reshape(2, nb, GW)
    return o[0], sn0, o[1], sn1


FOX_PAGES = 64


def _fox_bias_kernel(pt_ref, lfn_ref, lf_ref, ts_ref, pre_ref, o_ref, lf_scr):
    b = pl.program_id(0)
    n_pages = o_ref.shape[1]
    pg = o_ref.shape[2]
    for p in range(n_pages):
        page = lf_ref[pt_ref[b * n_pages + p]]
        for h in range(NH):
            lf_scr[h * n_pages + p:h * n_pages + p + 1, :] = page[h:h + 1, :]
    both = _dot_sel(lf_scr[...], ts_ref[...], "a")
    suf = both[:, :pg]
    tot = both[:, pg:]
    later = _dot_sel(pre_ref[...], tot, "b")
    for h in range(NH):
        r0, r1 = h * n_pages, (h + 1) * n_pages
        o_ref[h] = lfn_ref[h:h + 1, :] + later[r0:r1] + suf[r0:r1]


def _fox_sample_kernel(pt_ref, q_ref, kn_ref, vn_ref, bias_ref, *rest):
    g_n = bias_ref.shape[1]
    k_refs = rest[:g_n]
    v_refs = rest[g_n:2 * g_n]
    o_ref, m_scr, l_scr, acc_scr, s_scr = rest[2 * g_n:]
    c = pl.program_id(1)
    pg = k_refs[0].shape[1]

    @pl.when(c == 0)
    def _():
        m_scr[...] = jnp.full(m_scr.shape, NEG, F32)
        l_scr[...] = jnp.zeros_like(l_scr)
        acc_scr[...] = jnp.zeros_like(acc_scr)

    qb = jnp.broadcast_to(q_ref[...], (GW, pg))
    for g in range(g_n):
        prod = k_refs[g][...] * qb
        for h in range(NH):
            s_scr[h * g_n + g:h * g_n + g + 1, :] = jnp.sum(prod[h * HD:(h + 1) * HD], axis=0, keepdims=True)
    for h in range(NH):
        s_h = s_scr[h * g_n:(h + 1) * g_n, :] + bias_ref[h]
        m_old = m_scr[h:h + 1, :]
        m_new = jnp.maximum(m_old, jnp.max(jnp.max(s_h, axis=0, keepdims=True), axis=1, keepdims=True))
        alpha = jnp.exp(m_old - m_new)
        p = jnp.exp(s_h - m_new)
        m_scr[h:h + 1, :] = m_new
        l_scr[h:h + 1, :] = l_scr[h:h + 1, :] * alpha + jnp.sum(p, axis=0, keepdims=True)
        acc = acc_scr[h * HD:(h + 1) * HD, :] * alpha
        for g in range(g_n):
            acc = acc + p[g:g + 1, :] * v_refs[g][h * HD:(h + 1) * HD, :]
        acc_scr[h * HD:(h + 1) * HD, :] = acc

    @pl.when(c == pl.num_programs(1) - 1)
    def _():
        prod = q_ref[...] * kn_ref[...]
        for h in range(NH):
            s_self = jnp.sum(prod[h * HD:(h + 1) * HD], axis=0, keepdims=True)
            m_h = m_scr[h:h + 1, 0:1]
            m_fin = jnp.maximum(m_h, s_self)
            a_h = jnp.exp(m_h - m_fin)
            p_self = jnp.exp(s_self - m_fin)
            l_tot = jnp.sum(l_scr[h:h + 1, :], axis=1, keepdims=True) * a_h + p_self
            num = jnp.sum(acc_scr[h * HD:(h + 1) * HD, :], axis=1, keepdims=True) * a_h \
                + p_self * vn_ref[h * HD:(h + 1) * HD, :]
            o_ref[h * HD:(h + 1) * HD, :] = num / l_tot


def fox_sample(qkv, lf_new, cache_kt, cache_vt, cache_lf, page_table, li):
    nb, n_pages = page_table.shape
    n_phys = cache_lf.shape[0]
    pg = cache_kt.shape[-1]
    g_n = math.gcd(FOX_PAGES, n_pages)
    nc = n_pages // g_n
    qkv = qkv.reshape(3, nb, GW, 1)
    pt = page_table.reshape(-1)
    lfn = jnp.broadcast_to(lf_new[:, :, None], (nb, NH, pg))
    t = np.arange(pg)
    ts = jnp.asarray(np.concatenate([(t[:, None] > t[None, :]).astype(np.float32), np.ones((pg, pg), np.float32)], axis=1))
    r = np.arange(NH * n_pages)
    later = jnp.asarray(((r[:, None] // n_pages == r[None, :] // n_pages) & (r[None, :] > r[:, None])).astype(np.float32))
    bias = pl.pallas_call(
        _fox_bias_kernel,
        grid_spec=pltpu.PrefetchScalarGridSpec(
            num_scalar_prefetch=1,
            grid=(nb,),
            in_specs=[pl.BlockSpec((None, NH, pg), lambda b, pt: (b, 0, 0)),
                      pl.BlockSpec((n_phys, None, NH, pg), lambda b, pt: (0, li, 0, 0)),
                      pl.BlockSpec((pg, 2 * pg), lambda b, pt: (0, 0)),
                      pl.BlockSpec((NH * n_pages, NH * n_pages), lambda b, pt: (0, 0))],
            out_specs=pl.BlockSpec((None, NH, n_pages, pg), lambda b, pt: (b, 0, 0, 0)),
            scratch_shapes=[pltpu.VMEM((NH * n_pages, pg), F32)]),
        out_shape=jax.ShapeDtypeStruct((nb, NH, n_pages, pg), F32),
        compiler_params=_cp(("arbitrary",)),
        name="fox_bias",
    )(pt, lfn, cache_lf, ts, later)

    def page(g):
        return lambda b, c, pt: (pt[b * n_pages + c * g_n + g], li, 0, 0)

    cspec = lambda r: pl.BlockSpec((None, None, GW, 1), lambda b, c, pt: (r, b, 0, 0))
    in_specs = [cspec(0), cspec(1), cspec(2), pl.BlockSpec((None, NH, g_n, pg), lambda b, c, pt: (b, 0, c, 0))]
    in_specs += [pl.BlockSpec((None, None, GW, pg), page(g)) for g in range(g_n)]
    in_specs += [pl.BlockSpec((None, None, GW, pg), page(g)) for g in range(g_n)]
    o = pl.pallas_call(
        _fox_sample_kernel,
        grid_spec=pltpu.PrefetchScalarGridSpec(
            num_scalar_prefetch=1,
            grid=(nb, nc),
            in_specs=in_specs,
            out_specs=pl.BlockSpec((None, GW, 1), lambda b, c, pt: (b, 0, 0)),
            scratch_shapes=[pltpu.VMEM((NH, pg), F32), pltpu.VMEM((NH, pg), F32),
                            pltpu.VMEM((GW, pg), F32), pltpu.VMEM((NH * g_n, pg), F32)]),
        out_shape=jax.ShapeDtypeStruct((nb, GW, 1), F32),
        compiler_params=_cp(("arbitrary", "arbitrary")),
        name="fox_sample",
    )(pt, qkv, qkv, qkv, bias, *([cache_kt] * g_n), *([cache_vt] * g_n))
    return o.reshape(nb, GW)


def _sample_out_kernel(oa_ref, ob_ref, oc_ref, od_ref, go_ref, rg_ref, gng_ref, rng_ref, hs_ref, w_ref, x_ref, g1_ref, o_ref):
    hs = hs_ref[...]
    oc = _head_rms(oc_ref[...], hs, gng_ref[...]) * _silu(go_ref[...])
    od = _head_rms(od_ref[...], hs, rng_ref[...]) * _silu(rg_ref[...])
    cat = jnp.concatenate([oa_ref[...], ob_ref[...], oc, od], axis=1)
    o_ref[...] = x_ref[...] + g1_ref[...] * _wdot(cat, w_ref[...])


def _prep_w_in(w_in, li):
    wt = w_in.transpose(2, 0, 1)[:, li, :]
    o_ff = 5 * GW
    o_lr = o_ff + NH + 3 * GW
    return jnp.concatenate([wt[:o_ff], wt[o_ff + NH:o_lr], wt[o_lr + GLA_LR:],
                            wt[o_ff:o_ff + NH], wt[o_lr:o_lr + GLA_LR],
                            jnp.zeros((TAIL_W - NH - GLA_LR, wt.shape[1]), wt.dtype)], axis=0)


def _ffn_apply(li, x, mix, mod, g, fw, tm, rows_per_group, n_valid):
    if li % 2 == 0:
        w1, w3, w2 = fw
        return ffn_dense(x, mix, mod, li, g, w1, w3, w2, tm, w1.shape[1] // 2, rows_per_group)
    router_t, w1, w3, w2 = fw
    if mix is not None:
        x = out_proj(x, mix, mod, li, 512, rows_per_group)
    return moe_ffn(x, mod, li, g, router_t, w1, w3, w2, tm, w1.shape[2] // 2, rows_per_group, n_valid)


def kernel(x_prompt, x_sample, c_prompt, c_sample, state_conv, cache_k, cache_v, cache_logf, state_gla, state_ret, page_table, w_in, w_out, conv_w, conv_b, conv_ln_g, conv_ln_b, fox_qn_g, fox_kn_g, fox_fb, gla_w_gate2, gla_b_gate, gla_norm_g, ret_norm_g, norm1_g, norm2_g, w_ada, b_ada, ffn_w1, ffn_w3, ffn_w2, moe_router, moe_w1, moe_w3, moe_w2):
    nbp, l, d = x_prompt.shape
    nbs = x_sample.shape[0]
    depth = w_in.shape[0]
    n_pages, pg = page_table.shape[1], cache_k.shape[2]
    p_len = n_pages * pg
    mp = nbp * l
    ms = -(-nbs // 8) * 8
    ms_moe = 128

    mod = ada_mod(jnp.concatenate([c_prompt, c_sample], axis=0), w_ada, b_ada)
    mod = mod.reshape(depth, nbp + nbs, 6, d).transpose(0, 2, 1, 3)
    mod_p = mod[:, :, :nbp].reshape(depth, 6, nbp, 1, d)
    mod_s = jnp.zeros((depth, 6, ms_moe, d), F32).at[:, :, :nbs].set(mod[:, :, nbp:])
    cache_kt = cache_k.transpose(0, 1, 3, 4, 2).reshape(cache_k.shape[0], depth, GW, pg)
    cache_vt = cache_v.transpose(0, 1, 3, 4, 2).reshape(cache_v.shape[0], depth, GW, pg)
    cache_lf = cache_logf.transpose(0, 1, 3, 2)
    conv_state = state_conv.transpose(0, 2, 1, 3)
    cos_s, sin_s = _rope_tables(jnp.full((1,), p_len, F32))
    hs = jnp.asarray(_np_hsum())
    rlg = jnp.asarray(_ret_lg_row())

    xp = x_prompt.reshape(mp, d)
    xs = jnp.zeros((ms, d), F32).at[:nbs].set(x_sample.reshape(nbs, d))
    outs_p, outs_s = [], []
    k_all = jnp.zeros((nbp, depth, GW, l), F32)
    v_all = jnp.zeros((nbp, depth, GW, l), F32)
    lf_all = jnp.zeros((nbp, depth, NH, l), F32)
    for li in range(depth):
        n1 =norm1_g[li].reshape(1, d)
        n2 = norm2_g[li].reshape(1, d)
        w_in_f = _prep_w_in(w_in, li)
        w_in_b = w_in_f.astype(BF16)
        w_out_f = w_out[li]
        w_out_b = w_out_f.astype(BF16)
        if li % 2 == 0:
            fw_s = (ffn_w1[li // 2], ffn_w3[li // 2], ffn_w2[li // 2])
            fw = tuple(w.astype(BF16) for w in fw_s)
        else:
            fw = (jnp.pad(moe_router[li // 2], ((0, 0), (0, 128 - N_EXPERTS))), moe_w1[li // 2].astype(BF16),
                  moe_w3[li // 2].astype(BF16), moe_w2[li // 2].astype(BF16))
            fw_s = fw

        z = in_proj(xp, mod_p, li, n1, w_in_b, 512, l)
        out_a, conv_new = conv_prompt(z, nbp, l, conv_w[li], conv_b[li], conv_ln_g[li], conv_ln_b[li], 512)
        out_b, k_all, v_all, lf_all = fox_prompt(z, nbp, l, fox_qn_g[li], fox_kn_g[li], fox_fb[li], 512, li,
                                                 k_all, v_all, lf_all)
        out_c, gla_st = gla_prompt(z, nbp, l, gla_w_gate2[li], gla_b_gate[li], gla_norm_g[li], 256)
        out_d, ret_st = ret_prompt(z, nbp, l, ret_norm_g[li], 512)
        xp = _ffn_apply(li, xp, ((out_a, out_b, out_c, out_d), w_out_b), mod_p, n2, fw,
                        512 if li % 2 == 0 else 1024, l, mp)
        outs_p.append((conv_new, None, None, None, _state_from_blockdiag(gla_st), _state_from_blockdiag(ret_st)))

        zs = in_proj(xs, mod_s, li, n1, w_in_f, ms, ms)[:nbs]
        row = lambda v_: v_.reshape(1, GW)
        w2p = jnp.zeros((TAIL_W, GW), F32).at[NH:NH + GLA_LR].set(gla_w_gate2[li])
        fbp = jnp.zeros((1, TAIL_W), F32).at[0, :NH].set(fox_fb[li])
        r32 = jax.ShapeDtypeStruct((nbs, GW), F32)
        oa, cn, lfs, cols = pl.pallas_call(
            _sample_rows_kernel,
            out_shape=[r32, jax.ShapeDtypeStruct((CONV_K - 1, nbs, GW), F32),
                       jax.ShapeDtypeStruct((nbs, TAIL_W), F32), jax.ShapeDtypeStruct((9, nbs, GW), F32)],
            compiler_params=pltpu.CompilerParams(vmem_limit_bytes=VMEM_LIMIT),
            name="sample_rows",
        )(zs, conv_state[li], conv_w[li], row(conv_b[li]), row(conv_ln_g[li]), row(conv_ln_b[li]),
          row(jnp.tile(fox_qn_g[li], NH)), row(jnp.tile(fox_kn_g[li], NH)), fbp, w2p, row(gla_b_gate[li]),
          cos_s, sin_s, rlg, hs)
        grp = lambda g_: zs[:, g_ * GW:(g_ + 1) * GW]
        fk, f_v = cols[1], cols[2]
        ob = fox_sample(cols[:3], lfs[:, :NH], cache_kt, cache_vt, cache_lf, page_table, li)
        oc, gla_new, od, ret_new = rec_step(cols[3:], jnp.stack([grp(G_GV), grp(G_RV)]), state_gla, state_ret, li)
        pad = lambda v_: jnp.zeros((ms, v_.shape[1]), F32).at[:nbs].set(v_)
        xs = pl.pallas_call(
            _sample_out_kernel,
            out_shape=jax.ShapeDtypeStruct((ms, d), F32),
            compiler_params=pltpu.CompilerParams(vmem_limit_bytes=VMEM_LIMIT),
            name="sample_out",
        )(pad(oa), pad(ob), pad(oc), pad(od), pad(grp(G_GO)), pad(grp(G_RG)), row(gla_norm_g[li]), row(ret_norm_g[li]),
          hs, w_out_f, xs, mod_s[li, MOD_G1, :ms])
        if li % 2 == 0:
            xs = _ffn_apply(li, xs, None, mod_s, n2, fw_s, ms, ms, nbs)
        else:
            wide = jnp.zeros((ms_moe, d), F32).at[:ms].set(xs)
            xs = _ffn_apply(li, wide, None, mod_s, n2, fw_s, ms_moe, ms_moe, nbs)[:ms]
        outs_s.append((cn.transpose(1, 0, 2), fk.reshape(nbs, 1, NH, HD), f_v.reshape(nbs, 1, NH, HD),
                       lfs[:, :NH].reshape(nbs, 1, NH), gla_new, ret_new))

    st = lambda lst, i, ax: jnp.stack([s[i] for s in lst], axis=ax)
    heads_last = lambda a: a.reshape(nbp, depth, NH, HD, l).transpose(0, 1, 4, 2, 3)
    return (xp.reshape(nbp, l, d), xs[:nbs].reshape(nbs, 1, d),
            st(outs_p, 0, 0), st(outs_s, 0, 0),
            heads_last(k_all), st(outs_s, 1, 1),
            heads_last(v_all), st(outs_s, 2, 1),
            lf_all.transpose(0, 1, 3, 2), st(outs_s, 3, 1),
            st(outs_p, 4, 0), st(outs_s, 4, 0),
            st(outs_p, 5, 0), st(outs_s, 5, 0))
```

```python
import functools
import math

import numpy as np
import jax
import jax.numpy as jnp
from jax import lax
from jax.experimental import pallas as pl
from jax.experimental.pallas import tpu as pltpu

F32 = jnp.float32
BF16 = jnp.bfloat16
HI = lax.Precision.HIGHEST

D_MODEL = 1024
NH = 4
HD = 64
GW = NH * HD
CONV_K = 31
GLA_LR = 16
GLA_TAU = 16.0
GLA_CHUNK = 16
RET_CHUNK = 256
N_EXPERTS = 8
EPS = 1e-6
N_MAIN = 13 * GW
TAIL_W = 128
IN_PAD = N_MAIN + TAIL_W
TAIL_BLK = N_MAIN // TAIL_W
(G_CA, G_CG, G_FQ, G_FK, G_FV, G_GQ, G_GK, G_GV, G_GO, G_RQ, G_RK, G_RV, G_RG) = range(13)
VMEM_LIMIT = 56 * 1024 * 1024
MOE_ROWS = 128
NEG = -1e30
LOG2E = math.log2(math.e)


def _cp(sem, vmem=VMEM_LIMIT):
    return pltpu.CompilerParams(dimension_semantics=sem, vmem_limit_bytes=vmem)


def _silu(x):
    return x * jax.nn.sigmoid(x)


def _log_sigmoid(x):
    return jnp.minimum(x, 0.0) - jnp.log1p(jnp.exp(-jnp.abs(x)))


def _dot(a, b):
    return jnp.dot(a, b, preferred_element_type=F32)


def _split3(x):
    hi = x.astype(BF16)
    r1 = x - hi.astype(F32)
    mid = r1.astype(BF16)
    lo = (r1 - mid.astype(F32)).astype(BF16)
    return hi, mid, lo


def _dot_sel(a, b, data):
    if data == "a":
        sel = b.astype(BF16)
        return sum(_dot(p, sel) for p in _split3(a))
    sel = a.astype(BF16)
    return sum(_dot(sel, p) for p in _split3(b))


def _head_ids(shape, dim):
    return lax.shift_right_logical(lax.broadcasted_iota(jnp.int32, shape, dim), int(math.log2(HD)))


def _dot_nt(a, b, **kw):
    return lax.dot_general(a, b, (((1,), (1,)), ((), ())), preferred_element_type=F32, **kw)


def _dot_tn(a, b, **kw):
    return lax.dot_general(a, b, (((0,), (0,)), ((), ())), preferred_element_type=F32, **kw)


def _wdot(a, w, nt=False):
    if w.dtype == BF16:
        a, kw = a.astype(BF16), {}
    else:
        kw = {"precision": HI}
    return _dot_nt(a, w, **kw) if nt else jnp.dot(a, w, preferred_element_type=F32, **kw)


def _modnorm(x, g, sc, sh):
    y = x * lax.rsqrt(jnp.mean(x * x, axis=-1, keepdims=True) + EPS)
    return (y * g) * (1.0 + sc) + sh


def _head_rms(x, hsum, g):
    ms = _dot_sel(x * x, hsum, "a") * (1.0 / HD)
    return x * lax.rsqrt(ms + EPS) * g


def _np_hsum():
    i = np.arange(GW) // HD
    return (i[:, None] == i[None, :]).astype(np.float32)


def _ada_kernel(c_ref, w_ref, b_ref, o_ref):
    c = c_ref[...]
    o_ref[...] = _wdot(_silu(c), w_ref[...]) + b_ref[...]


def ada_mod(c_all, w_ada, b_ada):
    depth, d, n = w_ada.shape
    r = c_all.shape[0]
    tn = 1536
    return pl.pallas_call(
        _ada_kernel,
        grid=(depth, n // tn),
        in_specs=[pl.BlockSpec((r, d), lambda l, j: (0, 0)),
                  pl.BlockSpec((None, d, tn), lambda l, j: (l, 0, j)),
                  pl.BlockSpec((None, 1, tn), lambda l, j: (l, 0, j))],
        out_specs=pl.BlockSpec((None, r, tn), lambda l, j: (l, 0, j)),
        out_shape=jax.ShapeDtypeStruct((depth, r, n), F32),
        compiler_params=_cp(("arbitrary", "arbitrary")),
        name="ada_mod",
    )(c_all, w_ada, b_ada.reshape(depth, 1, n))


def _in_proj_kernel(x_ref, sc_ref, sh_ref, g_ref, w_ref, z_ref):
    h = _modnorm(x_ref[...], g_ref[...], sc_ref[...], sh_ref[...])
    z_ref[...] = _wdot(h, w_ref[...], nt=True)


MOD_SH1, MOD_SC1, MOD_G1, MOD_SH2, MOD_SC2, MOD_G2 = range(6)


def _mod_spec(mod, li, k, tm, rows_per_group):
    if mod.ndim == 5:
        tiles = rows_per_group // tm
        return pl.BlockSpec((None, None, None, 1, mod.shape[-1]), lambda i, *_: (li, k, i // tiles, 0, 0))
    return pl.BlockSpec((None, None, tm, mod.shape[-1]), lambda i, *_: (li, k, 0, 0))


def in_proj(x, mod, li, g, w, tm, rows_per_group):
    m, d = x.shape
    n = w.shape[0]
    return pl.pallas_call(
        _in_proj_kernel,
        grid=(m // tm,),
        in_specs=[pl.BlockSpec((tm, d), lambda i: (i, 0)),
                  _mod_spec(mod, li, MOD_SC1, tm, rows_per_group), _mod_spec(mod, li, MOD_SH1, tm, rows_per_group),
                  pl.BlockSpec((1, d), lambda i: (0, 0)),
                  pl.BlockSpec((n, d), lambda i: (0, 0))],
        out_specs=pl.BlockSpec((tm, n), lambda i: (i, 0)),
        out_shape=jax.ShapeDtypeStruct((m, n), F32),
        compiler_params=_cp(("arbitrary",)),
        name="in_proj",
    )(x, mod, mod, g, w)


def _block_input(x_ref, mix_refs):
    x = x_ref[...]
    if mix_refs:
        *parts, w_ref, g1_ref = mix_refs
        x = x + g1_ref[...] * _dot(jnp.concatenate([p[...] for p in parts], axis=1), w_ref[...])
    return x


def _mix_operands(mix, mod, li, tm, rows_per_group):
    if mix is None:
        return [], []
    parts, w_out = mix
    specs = [pl.BlockSpec((tm, GW), lambda i, *_: (i, 0))] * 4 + [
        pl.BlockSpec(w_out.shape, lambda i, *_: (0, 0)), _mod_spec(mod, li, MOD_G1, tm, rows_per_group)]
    return specs, [*parts, w_out, mod]


def _out_proj_kernel(x_ref, *rest):
    rest[-1][...] = _block_input(x_ref, rest[:-1])


def out_proj(x, mix, mod, li, tm, rows_per_group):
    m, d = x.shape
    mix_specs, mix_args = _mix_operands(mix, mod, li, tm, rows_per_group)
    return pl.pallas_call(
        _out_proj_kernel,
        grid=(m // tm,),
        in_specs=[pl.BlockSpec((tm, d), lambda i: (i, 0))] + mix_specs,
        out_specs=pl.BlockSpec((tm, d), lambda i: (i, 0)),
        out_shape=jax.ShapeDtypeStruct((m, d), F32),
        compiler_params=_cp(("arbitrary",)),
        name="out_proj",
    )(x, *mix_args)


def _ffn_kernel(n_mix, x_ref, *rest):
    mix_refs, rest = rest[:n_mix], rest[n_mix:]
    sc_ref, sh_ref, g_ref, gate_ref, w1_ref, w3_ref, w2_ref, o_ref, h_scr, acc_scr, x1_scr = rest
    j = pl.program_id(1)

    @pl.when(j == 0)
    def _():
        x1 = _block_input(x_ref, mix_refs)
        x1_scr[...] = x1
        h_scr[...] = _modnorm(x1, g_ref[...], sc_ref[...], sh_ref[...]).astype(h_scr.dtype)
        acc_scr[...] = jnp.zeros_like(acc_scr)

    h = h_scr[...]
    acc_scr[...] += _wdot(_silu(_wdot(h, w1_ref[...])) * _wdot(h, w3_ref[...]), w2_ref[...])

    @pl.when(j == pl.num_programs(1) - 1)
    def _():
        o_ref[...] = x1_scr[...] + gate_ref[...] * acc_scr[...]


def ffn_dense(x, mix, mod, li, g, w1, w3, w2, tm, tf, rows_per_group):
    m, d = x.shape
    ff = w1.shape[1]
    mix_specs, mix_args = _mix_operands(mix, mod, li, tm, rows_per_group)
    spec = lambda k: _mod_spec(mod, li, k, tm, rows_per_group)
    return pl.pallas_call(
        functools.partial(_ffn_kernel, len(mix_args)),
        grid=(m // tm, ff // tf),
        in_specs=[pl.BlockSpec((tm, d), lambda i, j: (i, 0))] + mix_specs + [
            spec(MOD_SC2), spec(MOD_SH2),
            pl.BlockSpec((1, d), lambda i, j: (0, 0)),
            spec(MOD_G2),
            pl.BlockSpec((d, tf), lambda i, j: (0, j)),
            pl.BlockSpec((d, tf), lambda i, j: (0, j)),
            pl.BlockSpec((tf, d), lambda i, j: (j, 0))],
        out_specs=pl.BlockSpec((tm, d), lambda i, j: (i, 0)),
        out_shape=jax.ShapeDtypeStruct((m, d), F32),
        scratch_shapes=[pltpu.VMEM((tm, d), w1.dtype), pltpu.VMEM((tm, d), F32), pltpu.VMEM((tm, d), F32)],
        compiler_params=_cp(("arbitrary", "arbitrary")),
        name="ffn_dense",
    )(x, *mix_args, mod, mod, g, mod, w1, w3, w2)


def _route_kernel(n_valid, x_ref, sc_ref, sh_ref, g_ref, rt_ref, tri_ref, h_ref, rank_ref, gate_ref, cnt_ref):
    t = x_ref.shape[0]
    h = _modnorm(x_ref[...], g_ref[...], sc_ref[...], sh_ref[...])
    h_ref[...] = h.astype(BF16)
    logits = jnp.dot(h, rt_ref[...], preferred_element_type=F32, precision=HI).T[:N_EXPERTS]
    e_iota = lax.broadcasted_iota(jnp.int32, logits.shape, 0).astype(F32)
    m1 = jnp.max(logits, axis=0, keepdims=True)
    i1 = jnp.min(jnp.where(logits == m1, e_iota, float(N_EXPERTS)), axis=0, keepdims=True)
    sel1 = e_iota == i1
    rest = jnp.where(sel1, -jnp.inf, logits)
    m2 = jnp.max(rest, axis=0, keepdims=True)
    i2 = jnp.min(jnp.where(rest == m2, e_iota, float(N_EXPERTS)), axis=0, keepdims=True)
    sel2 = e_iota == i2
    e2 = jnp.exp(m2 - m1)
    den = 1.0 + e2
    gate = jnp.where(sel1, 1.0 / den, 0.0) + jnp.where(sel2, e2 / den, 0.0)
    tok = pl.program_id(0) * t + lax.broadcasted_iota(jnp.int32, logits.shape, 1)
    sel = jnp.logical_and(jnp.logical_or(sel1, sel2), tok < n_valid)
    incl = _dot(jnp.where(sel, 1.0, 0.0).astype(BF16), tri_ref[...])
    rank_ref[...] = jnp.where(sel, incl - 1.0, -1.0)
    gate_ref[...] = gate
    cnt_ref[...] = jnp.broadcast_to(incl[:, t - 1:t], cnt_ref.shape)


def _moe_kernel(cnt_ref, h_ref, rank_ref, gate_ref, x_ref, g2_ref, w1_ref, w3_ref, w2_ref, o_ref, xs_scr, y_scr):
    i, e, j = pl.program_id(0), pl.program_id(1), pl.program_id(2)
    nff = pl.num_programs(2)
    t = h_ref.shape[0]
    r = min(MOE_ROWS, t)
    cnt = cnt_ref[i * N_EXPERTS + e]
    nb = (cnt + (r - 1)) // r
    row = lax.broadcasted_iota(jnp.int32, (r, t), 0).astype(F32)

    @pl.when(jnp.logical_and(e == 0, j == 0))
    def _():
        o_ref[...] = jnp.zeros_like(o_ref)
        y_scr[...] = jnp.zeros_like(y_scr)

    @pl.when(j == 0)
    def _():
        def gather(s, c):
            r0 = pl.multiple_of(s * r, r)
            onehot = jnp.where(rank_ref[...] == row + (s * r).astype(F32), 1.0, 0.0).astype(BF16)
            xs_scr[pl.ds(r0, r), :] = _dot(onehot, h_ref[...]).astype(BF16)
            y_scr[pl.ds(r0, r), :] = jnp.zeros((r, y_scr.shape[1]), F32)
            return c
        lax.fori_loop(0, nb, gather, 0)

    def ffn(s, c, rows=r):
        r0 = pl.multiple_of(s * r, r)
        xs = xs_scr[pl.ds(r0, rows), :]
        act = (_silu(_dot(xs, w1_ref[...])) * _dot(xs, w3_ref[...])).astype(BF16)
        y_scr[pl.ds(r0, rows), :] += _dot(act, w2_ref[...])
        return c

    n_full = cnt // r
    rem = cnt - n_full * r
    lax.fori_loop(0, n_full, ffn, 0)
    @pl.when(rem > 0)
    def _():
        ffn(n_full, 0)

    @pl.when(j == nff - 1)
    def _():
        rp = min(max(r, 256), t)
        rowp = lax.broadcasted_iota(jnp.int32, (rp, t), 0).astype(F32)

        def scatter(s, c):
            r0 = pl.multiple_of(s * rp, rp)
            hit = rank_ref[...] == rowp + (s * rp).astype(F32)
            wgt = jnp.where(hit, gate_ref[...], 0.0).astype(BF16)
            o_ref[...] += _dot_tn(wgt, y_scr[pl.ds(r0, rp), :].astype(BF16))
            return c
        lax.fori_loop(0, (nb * r + rp - 1) // rp, scatter, 0)

    @pl.when(jnp.logical_and(e == N_EXPERTS - 1, j == nff - 1))
    def _():
        o_ref[...] = x_ref[...] + g2_ref[...] * o_ref[...]


def moe_ffn(x, mod, li, g, router_t, w1, w3, w2, t, tf, rows_per_group, n_valid):
    m, d = x.shape
    nt = m // t
    ne, _, ff = w1.shape
    assert ne == N_EXPERTS
    tri = jnp.asarray(np.triu(np.ones((t, t), np.float32)), BF16)
    h, rank, gate, cnt = pl.pallas_call(
        functools.partial(_route_kernel, n_valid),
        grid=(nt,),
        in_specs=[pl.BlockSpec((t, d), lambda i: (i, 0)),
                  _mod_spec(mod, li, MOD_SC2, t, rows_per_group), _mod_spec(mod, li, MOD_SH2, t, rows_per_group),
                  pl.BlockSpec((1, d), lambda i: (0, 0)),
                  pl.BlockSpec((d, 128), lambda i: (0, 0)),
                  pl.BlockSpec((t, t), lambda i: (0, 0))],
        out_specs=[pl.BlockSpec((t, d), lambda i: (i, 0)),
                   pl.BlockSpec((None, ne, t), lambda i: (i, 0, 0)),
                   pl.BlockSpec((None, ne, t), lambda i: (i, 0, 0)),
                   pl.BlockSpec((None, ne, 128), lambda i: (i, 0, 0))],
        out_shape=[jax.ShapeDtypeStruct((m, d), BF16),
                   jax.ShapeDtypeStruct((nt, ne, t), F32),
                   jax.ShapeDtypeStruct((nt, ne, t), F32),
                   jax.ShapeDtypeStruct((nt, ne, 128), F32)],
        compiler_params=_cp(("arbitrary",)),
        name="moe_route",
    )(x, mod, mod, g, router_t, tri)
    counts = cnt[:, :, 0].astype(jnp.int32).reshape(nt * ne)
    rank = rank.reshape(nt, ne, 1, t)
    gate = gate.reshape(nt, ne, 1, t)
    out = pl.pallas_call(
        _moe_kernel,
        grid_spec=pltpu.PrefetchScalarGridSpec(
            num_scalar_prefetch=1,
            grid=(nt, ne, ff // tf),
            in_specs=[pl.BlockSpec((t, d), lambda i, e, j, nb: (i, 0)),
                      pl.BlockSpec((None, None, 1, t), lambda i, e, j, nb: (i, e, 0, 0)),
                      pl.BlockSpec((None, None, 1, t), lambda i, e, j, nb: (i, e, 0, 0)),
                      pl.BlockSpec((t, d), lambda i, e, j, nb: (i, 0)),
                      _mod_spec(mod, li, MOD_G2, t, rows_per_group),
                      pl.BlockSpec((None, d, tf), lambda i, e, j, nb: (e, 0, j)),
                      pl.BlockSpec((None, d, tf), lambda i, e, j, nb: (e, 0, j)),
                      pl.BlockSpec((None, tf, d), lambda i, e, j, nb: (e, j, 0))],
            out_specs=pl.BlockSpec((t, d), lambda i, e, j, nb: (i, 0)),
            scratch_shapes=[pltpu.VMEM((t, d), BF16), pltpu.VMEM((t, d), F32)]),
        out_shape=jax.ShapeDtypeStruct((m, d), F32),
        compiler_params=_cp(("arbitrary", "arbitrary", "arbitrary")),
        name="moe_ffn",
    )(counts, h, rank, gate, x, mod, w1, w3, w2)
    return out


def _conv_kernel(a_ref, g_ref, w_ref, b_ref, lg_ref, lb_ref, o_ref, cn_ref, ext_scr, sh_scr):
    tc = a_ref.shape[0]
    halo = 32
    off = halo - (CONV_K - 1)

    @pl.when(pl.program_id(1) == 0)
    def _():
        ext_scr[0:halo, :] = jnp.zeros((halo, GW), F32)

    ext_scr[halo:halo + tc, :] = a_ref[...] * jax.nn.sigmoid(g_ref[...])
    for s in range(1, 8):
        sh_scr[s - 1] = ext_scr[s:s + tc + halo - 8, :]
    rc = 64
    for c in range(tc // rc):
        acc = jnp.zeros((rc, GW), F32)
        for k in range(CONV_K):
            s, r0 = (k + off) % 8, c * rc + (k + off) // 8 * 8
            rows = ext_scr[r0:r0 + rc, :] if s == 0 else sh_scr[s - 1, r0:r0 + rc, :]
            acc = acc + w_ref[k:k + 1, :] * rows
        y = acc + b_ref[...]
        mu = jnp.mean(y, axis=-1, keepdims=True)
        yc = y - mu
        var = jnp.mean(yc * yc, axis=-1, keepdims=True)
        o_ref[c * rc:(c + 1) * rc, :] = _silu(yc * lax.rsqrt(var + EPS) * lg_ref[...] + lb_ref[...]).astype(o_ref.dtype)
    cn_ref[...] = ext_scr[tc + off:tc + halo, :]
    ext_scr[0:halo, :] = ext_scr[tc:tc + halo, :]


def conv_prompt(z, nb, l, conv_w, conv_b, ln_g, ln_b, tc):
    m = z.shape[0]
    nl = l // tc
    row = lambda v: v.reshape(1, GW)
    return pl.pallas_call(
        _conv_kernel,
        grid=(nb, nl),
        in_specs=[pl.BlockSpec((tc, GW), lambda b, i: (b * nl + i, G_CA)),
                  pl.BlockSpec((tc, GW), lambda b, i: (b * nl + i, G_CG)),
                  pl.BlockSpec((CONV_K, GW), lambda b, i: (0, 0)),
                  pl.BlockSpec((1, GW), lambda b, i: (0, 0)),
                  pl.BlockSpec((1, GW), lambda b, i: (0, 0)),
                  pl.BlockSpec((1, GW), lambda b, i: (0, 0))],
        out_specs=[pl.BlockSpec((tc, GW), lambda b, i: (b * nl + i, 0)),
                   pl.BlockSpec((None, CONV_K - 1, GW), lambda b, i: (b, 0, 0))],
        out_shape=[jax.ShapeDtypeStruct((m, GW), BF16),
                   jax.ShapeDtypeStruct((nb, CONV_K - 1, GW), F32)],
        scratch_shapes=[pltpu.VMEM((tc + 32, GW), F32), pltpu.VMEM((7, tc + 24, GW), F32)],
        compiler_params=_cp(("arbitrary", "arbitrary")),
        name="conv_prompt",
    )(z, z, conv_w, row(conv_b), row(ln_g), row(ln_b))


def _fox_prep_kernel(q_ref, k_ref, v_ref, t_ref, gq_ref, gk_ref, fb_ref, hs_ref, tri_ref,
                     k_all, v_all, lf_all, qb_ref, kb_ref, vb_ref, kt_ref, vt_ref, lft_ref, fc_ref, carry_scr):
    del k_all, v_all, lf_all

    @pl.when(pl.program_id(1) == 0)
    def _():
        carry_scr[...] = jnp.zeros_like(carry_scr)

    hs = hs_ref[...]
    qn = _head_rms(q_ref[...], hs, gq_ref[...])
    kn = _head_rms(k_ref[...], hs, gk_ref[...])
    v = v_ref[...]
    vt = v.T
    qb_ref[...] = (qn * (HD ** -0.5 * LOG2E)).astype(BF16)
    kb_ref[...] = kn.astype(BF16)
    vb_ref[...] = vt.astype(BF16)
    kt_ref[...] = kn.T
    vt_ref[...] = vt
    lf = _log_sigmoid(t_ref[...] + fb_ref[...])
    lft_ref[...] = lf.T[:NH]
    cum = _dot_sel(tri_ref[...], lf, "b") + carry_scr[...]
    fc_ref[...] = cum * LOG2E
    carry_scr[...] = cum[cum.shape[0] - 1:, :]


def _fox_flash_kernel(q_ref, k_ref, vt_ref, fc_ref, o_ref, ot_scr):
    tq = q_ref.shape[0]
    tk = tq
    qi = pl.program_id(1)
    q = q_ref[...]
    head = _head_ids((1, GW), 1)
    kpos = lax.broadcasted_iota(jnp.int32, (tk, tq), 0)
    qpos = lax.broadcasted_iota(jnp.int32, (tk, tq), 1)
    qhs = [jnp.where(head == h, q, jnp.zeros_like(q)) for h in range(NH)]

    def step(jb, carry, diagonal):
        k0 = pl.multiple_of(jb * tk, tk)
        kb = k_ref[pl.ds(k0, tk), :]
        fcb = fc_ref[pl.ds(k0, tk), :]
        out = []
        for h in range(NH):
            m, l, acc = carry[h]
            s = _dot_nt(kb, qhs[h]) - fcb[:, h:h + 1]
            if diagonal:
                s = jnp.where(kpos <= qpos, s, NEG)
            m_new = jnp.maximum(m, jnp.max(s, axis=0, keepdims=True))
            alpha = jnp.exp2(m - m_new)
            p = jnp.exp2(s - m_new)
            l = alpha * l + jnp.sum(p, axis=0, keepdims=True)
            acc = alpha * acc + _dot(vt_ref[jb, h * HD:(h + 1) * HD, :], p.astype(BF16))
            out.append((m_new, l, acc))
        return tuple(out)

    init = tuple((jnp.full((1, tq), NEG, F32), jnp.zeros((1, tq), F32), jnp.zeros((HD, tq), F32)) for _ in range(NH))
    carry = lax.fori_loop(0, qi, functools.partial(step, diagonal=False), init)
    carry = step(qi, carry, True)
    for h in range(NH):
        _, l, acc = carry[h]
        ot_scr[h * HD:(h + 1) * HD, :] = acc * (1.0 / l)
    o_ref[...] = ot_scr[...].T.astype(o_ref.dtype)


def fox_prompt(z, nb, l, gq, gk, fb, tq, li, k_all, v_all, lf_all):
    m = z.shape[0]
    tp = tq
    nl = l // tp
    hs = jnp.asarray(_np_hsum())
    tri = jnp.asarray(np.tril(np.ones((tp, tp), np.float32)))
    row = lambda v: jnp.tile(v, NH).reshape(1, GW)
    fbp = jnp.zeros((1, TAIL_W), F32).at[0, :NH].set(fb)
    blk = lambda g: pl.BlockSpec((tp, GW), lambda b, i: (b * nl + i, g))
    oblk = lambda w: pl.BlockSpec((tp, w), lambda b, i: (b * nl + i, 0))
    anyspec = pl.BlockSpec(memory_space=pl.ANY)
    stacked = lambda r: pl.BlockSpec((None, None, r, tp), lambda b, i: (b, li, 0, i))
    qb, kb, vb, k_all, v_all, lf_all, fc = pl.pallas_call(
        _fox_prep_kernel,
        grid=(nb, nl),
        in_specs=[blk(G_FQ), blk(G_FK), blk(G_FV),
                  pl.BlockSpec((tp, TAIL_W), lambda b, i: (b * nl + i, TAIL_BLK)),
                  pl.BlockSpec((1, GW), lambda b, i: (0, 0)),
                  pl.BlockSpec((1, GW), lambda b, i: (0, 0)),
                  pl.BlockSpec((1, TAIL_W), lambda b, i: (0, 0)),
                  pl.BlockSpec((GW, GW), lambda b, i: (0, 0)),
                  pl.BlockSpec((tp, tp), lambda b, i: (0, 0)),
                  anyspec, anyspec, anyspec],
        out_specs=[oblk(GW)] * 2 + [pl.BlockSpec((None, None, GW, tp), lambda b, i: (b, i, 0, 0)),
                                    stacked(GW), stacked(GW), stacked(NH), oblk(TAIL_W)],
        out_shape=[jax.ShapeDtypeStruct((m, GW), BF16)] * 2 + [jax.ShapeDtypeStruct((nb, nl, GW, tp), BF16),
                   jax.ShapeDtypeStruct(k_all.shape, F32), jax.ShapeDtypeStruct(v_all.shape, F32),
                   jax.ShapeDtypeStruct(lf_all.shape, F32), jax.ShapeDtypeStruct((m, TAIL_W), F32)],
        input_output_aliases={9: 3, 10: 4, 11: 5},
        scratch_shapes=[pltpu.VMEM((1, TAIL_W), F32)],
        compiler_params=_cp(("arbitrary", "arbitrary")),
        name="fox_prep",
    )(z, z, z, z, row(gq), row(gk), fbp, hs, tri, k_all, v_all, lf_all)
    nq = l // tq
    o = pl.pallas_call(
        _fox_flash_kernel,
        grid=(nb, nq),
        in_specs=[pl.BlockSpec((tq, GW), lambda b, i: (b * nq + i, 0)),
                  pl.BlockSpec((l, GW), lambda b, i: (b, 0)),
                  pl.BlockSpec((None, nq, GW, tq), lambda b, i: (b, 0, 0, 0)),
                  pl.BlockSpec((l, TAIL_W), lambda b, i: (b, 0))],
        out_specs=pl.BlockSpec((tq, GW), lambda b, i: (b * nq + i, 0)),
        out_shape=jax.ShapeDtypeStruct((m, GW), BF16),
        scratch_shapes=[pltpu.VMEM((GW, tq), F32)],
        compiler_params=_cp(("arbitrary", "arbitrary")),
        name="fox_flash",
    )(qb, kb, vb, fc)
    return o, k_all, v_all, lf_all


GLA_SAFE_LOG = -80.0
GLA_SUB = 256


def _gla_kernel(q_ref, k_ref, v_ref, t_ref, go_ref, w2_ref, bg_ref, gn_ref, hs_ref, cs_ref, bd_ref,
                o_ref, st_ref, st_scr, x_scr, o_scr):
    tg = q_ref.shape[0]
    c = GLA_CHUNK

    @pl.when(pl.program_id(1) == 0)
    def _():
        st_scr[...] = jnp.zeros_like(st_scr)

    hs = hs_ref[...]
    bd = bd_ref[...]
    glog_all = _log_sigmoid(_dot(t_ref[...].astype(BF16), w2_ref[...]) + bg_ref[...]) * (1.0 / GLA_TAU)
    ts = min(GLA_SUB, tg)
    lower = lax.broadcasted_iota(jnp.int32, (ts, ts), 0) >= lax.broadcasted_iota(jnp.int32, (ts, ts), 1)

    def sub_tile(n):
        rows = slice(n * ts, (n + 1) * ts)
        glog = glog_all[rows]
        q = q_ref[rows, :] * (HD ** -0.5)
        k = k_ref[rows, :]
        v = v_ref[rows, :]
        vb = v.astype(BF16)
        bfull = _dot_sel(jnp.where(lower, 1.0, 0.0), glog, "b")
        safe = jnp.min(bfull) >= GLA_SAFE_LOG

        @pl.when(safe)
        def _():
            head = _head_ids((1, GW), 1)
            btot = bfull[ts - 1:ts]
            qe = (q * jnp.exp(bfull)).astype(BF16)
            kinv = (k * jnp.exp(-bfull)).astype(BF16)
            kend = (k * jnp.exp(btot - bfull)).astype(BF16)
            o = _dot_nt(qe, st_scr[...].astype(BF16))
            for h in range(NH):
                s = _dot_nt(jnp.where(head == h, qe, jnp.zeros_like(qe)), kinv)
                o = o + jnp.where(head == h, _dot(jnp.where(lower, s, 0.0).astype(BF16), vb), 0.0)
            o_scr[rows, :] = o
            st_scr[...] = st_scr[...] * jnp.exp(btot) + _dot_tn(vb, kend) * bd

        @pl.when(jnp.logical_not(safe))
        def _():
            hs_b = hs.astype(BF16)
            cs = cs_ref[0:ts, 0:ts]
            bcum = _dot_sel(jnp.where(lower, cs, 0.0), glog, "b")
            blast = _dot_sel(cs, glog, "b")
            qe = (q * jnp.exp(bcum)).astype(BF16)
            ke = (k * jnp.exp(blast - bcum)).astype(BF16)
            ii = lax.broadcasted_iota(jnp.int32, (c, GW), 0)
            for m in range(ts // c):
                r0 = m * c
                bc = bcum[r0:r0 + c]
                qc = q[r0:r0 + c]
                kc = k[r0:r0 + c]
                vc = v[r0:r0 + c]
                for j in range(c):
                    ex = jnp.exp(jnp.where(ii >= j, bc - bc[j:j + 1], -jnp.inf))
                    x_scr[j * c:(j + 1) * c, :] = (qc * ex * kc[j:j + 1]).astype(BF16)
                att = _dot(x_scr[...], hs_b)
                o = _dot_nt(qe[r0:r0 + c], st_scr[...].astype(BF16))
                for j in range(c):
                    o = o + att[j * c:(j + 1) * c] * vc[j:j + 1]
                o_scr[n * ts + r0:n * ts + r0 + c, :] = o
                kv = _dot_tn(vb[r0:r0 + c], ke[r0:r0 + c])
                st_scr[...] = st_scr[...] * jnp.exp(blast[r0:r0 + 1]) + kv * bd

    for n in range(tg // ts):
        sub_tile(n)
    o_ref[...] = (_head_rms(o_scr[...], hs, gn_ref[...]) * _silu(go_ref[...])).astype(o_ref.dtype)
    st_ref[...] = st_scr[...]


def gla_prompt(z, nb, l, w_gate2, b_gate, norm_g, tg):
    m = z.shape[0]
    nl = l // tg
    hs = jnp.asarray(_np_hsum())
    ch = np.arange(tg) // GLA_CHUNK
    cs = jnp.asarray((ch[:, None] == ch[None, :]).astype(np.float32))
    w2p = jnp.zeros((TAIL_W, GW), F32).at[NH:NH + GLA_LR].set(w_gate2).astype(BF16)
    blk = lambda g: pl.BlockSpec((tg, GW), lambda b, i: (b * nl + i, g))
    full = lambda r, c: pl.BlockSpec((r, c), lambda b, i: (0, 0))
    return pl.pallas_call(
        _gla_kernel,
        grid=(nb, nl),
        in_specs=[blk(G_GQ), blk(G_GK), blk(G_GV),
                  pl.BlockSpec((tg, TAIL_W), lambda b, i: (b * nl + i, TAIL_BLK)),
                  blk(G_GO), full(TAIL_W, GW), full(1, GW), full(1, GW), full(GW, GW), full(tg, tg), full(GW, GW)],
        out_specs=[pl.BlockSpec((tg, GW), lambda b, i: (b * nl + i, 0)),
                   pl.BlockSpec((None, GW, GW), lambda b, i: (b, 0, 0))],
        out_shape=[jax.ShapeDtypeStruct((m, GW), BF16), jax.ShapeDtypeStruct((nb, GW, GW), F32)],
        scratch_shapes=[pltpu.VMEM((GW, GW), F32), pltpu.VMEM((GLA_CHUNK * GLA_CHUNK, GW), BF16),
                        pltpu.VMEM((tg, GW), F32)],
        compiler_params=_cp(("arbitrary", "arbitrary")),
        name="gla_prompt",
    )(z, z, z, z, z, w2p, b_gate.reshape(1, GW), norm_g.reshape(1, GW), hs, cs, hs)


def _ret_lg_row():
    lg = np.log(1.0 - np.exp2(-5.0 - np.arange(NH, dtype=np.float32))).astype(np.float32)
    return np.repeat(lg, HD).reshape(1, GW)


def _ret_decay_mask(c):
    lg = _ret_lg_row()[0, ::HD]
    i = np.arange(c, dtype=np.float32)
    rel = i[:, None] - i[None, :]
    return np.where(rel >= 0, np.exp(np.maximum(rel, 0.0)[None] * lg[:, None, None]), 0.0).astype(np.float32)


def _rope(x, cos, sin_signed):
    first = (lax.broadcasted_iota(jnp.int32, (1, GW), 1) & (HD - 1)) < (HD // 2)
    swapped = jnp.where(first, pltpu.roll(x, GW - HD // 2, axis=1), pltpu.roll(x, HD // 2, axis=1))
    return x * cos + swapped * sin_signed


def _ret_kernel(q_ref, k_ref, v_ref, g_ref, cos_ref, sin_ref, lg_ref, gn_ref, hs_ref, dm_ref, o_ref, st_ref, st_scr):
    tr = dm_ref.shape[1]

    @pl.when(pl.program_id(1) == 0)
    def _():
        st_scr[...] = jnp.zeros_like(st_scr)

    hs = hs_ref[...]
    lg = lg_ref[...]
    head = _head_ids((1, GW), 1)
    ri = lax.broadcasted_iota(jnp.int32, (tr, 1), 0).astype(F32)
    dq = jnp.exp((ri + 1.0) * lg)
    dk = jnp.exp((tr - 1.0 - ri) * lg)
    ds = jnp.exp(tr * lg)
    for n in range(q_ref.shape[0] // tr):
        rows = slice(n * tr, (n + 1) * tr)
        cos = cos_ref[rows, :]
        sin = sin_ref[rows, :]
        q = _rope(q_ref[rows, :], cos, sin)
        k = _rope(k_ref[rows, :], cos, sin) * (HD ** -0.5)
        qb = q.astype(BF16)
        kb = k.astype(BF16)
        vb = v_ref[rows, :].astype(BF16)
        o = _dot_nt((q * dq).astype(BF16), st_scr[...].astype(BF16))
        for h in range(NH):
            att = _dot_nt(jnp.where(head == h, qb, jnp.zeros_like(qb)), kb) * dm_ref[h]
            o = o + jnp.where(head == h, _dot(att.astype(BF16), vb), 0.0)
        st_scr[...] = st_scr[...] * ds + _dot_tn(vb, (k * dk).astype(BF16)) * hs
        o_ref[rows, :] = (_head_rms(o, hs, gn_ref[...]) * _silu(g_ref[rows, :])).astype(o_ref.dtype)
    st_ref[...] = st_scr[...]


def _rope_tables(pos):
    half = HD // 2
    inv = 10000.0 ** (-jnp.arange(half, dtype=F32) / half)
    ang = pos[:, None] * inv[None, :]
    cos = jnp.cos(ang)
    sin = jnp.sin(ang)
    cos_t = jnp.tile(jnp.concatenate([cos, cos], axis=1), (1, NH))
    sin_t = jnp.tile(jnp.concatenate([-sin, sin], axis=1), (1, NH))
    return cos_t, sin_t


def ret_prompt(z, nb, l, norm_g, tt):
    m = z.shape[0]
    nl = l // tt
    tr = RET_CHUNK
    hs = jnp.asarray(_np_hsum())
    cos_t, sin_t = _rope_tables(jnp.arange(l, dtype=F32))
    blk = lambda g: pl.BlockSpec((tt, GW), lambda b, i: (b * nl + i, g))
    full = lambda r, c: pl.BlockSpec((r, c), lambda b, i: (0, 0))
    tab = pl.BlockSpec((tt, GW), lambda b, i: (i, 0))
    return pl.pallas_call(
        _ret_kernel,
        grid=(nb, nl),
        in_specs=[blk(G_RQ), blk(G_RK), blk(G_RV), blk(G_RG), tab, tab, full(1, GW), full(1, GW), full(GW, GW),
                  pl.BlockSpec((NH, tr, tr), lambda b, i: (0, 0, 0))],
        out_specs=[pl.BlockSpec((tt, GW), lambda b, i: (b * nl + i, 0)),
                   pl.BlockSpec((None, GW, GW), lambda b, i: (b, 0, 0))],
        out_shape=[jax.ShapeDtypeStruct((m, GW), BF16), jax.ShapeDtypeStruct((nb, GW, GW), F32)],
        scratch_shapes=[pltpu.VMEM((GW, GW), F32)],
        compiler_params=_cp(("arbitrary", "arbitrary")),
        name="ret_prompt",
    )(z, z, z, z, cos_t, sin_t, jnp.asarray(_ret_lg_row()), norm_g.reshape(1, GW), hs, jnp.asarray(_ret_decay_mask(tr)))


def _state_from_blockdiag(st):
    nb = st.shape[0]
    s5 = st.reshape(nb, NH, HD, NH, HD)
    diag = jnp.stack([s5[:, h, :, h, :] for h in range(NH)], axis=1)
    return diag.transpose(0, 1, 3, 2)


def _sample_rows_kernel(z_ref, buf_ref, cw_ref, cb_ref, lg_ref, lb_ref, gq_ref, gk_ref, fb_ref, w2_ref, bg_ref,
                        cos_ref, sin_ref, rlg_ref, hs_ref,
                        oa_ref, cn_ref, lf_ref, col_ref):
    grp = lambda g: z_ref[:, g * GW:(g + 1) * GW]
    tail = z_ref[:, N_MAIN:N_MAIN + TAIL_W]
    hs = hs_ref[...]
    u = grp(G_CA) * jax.nn.sigmoid(grp(G_CG))
    y = cw_ref[CONV_K - 1:CONV_K, :] * u + cb_ref[...]
    for k in range(CONV_K - 1):
        y = y + cw_ref[k:k + 1, :] * buf_ref[k]
    mu = jnp.mean(y, axis=-1, keepdims=True)
    yc = y - mu
    var = jnp.mean(yc * yc, axis=-1, keepdims=True)
    oa_ref[...] = _silu(yc * lax.rsqrt(var + EPS) * lg_ref[...] + lb_ref[...]).astype(oa_ref.dtype)
    for k in range(CONV_K - 2):
        cn_ref[k] = buf_ref[k + 1]
    cn_ref[CONV_K - 2] = u
    col_ref[0] = _head_rms(grp(G_FQ), hs, gq_ref[...]) * (HD ** -0.5)
    col_ref[1] = _head_rms(grp(G_FK), hs, gk_ref[...])
    col_ref[2] = grp(G_FV)
    lf_ref[...] = _log_sigmoid(tail + fb_ref[...])
    glog = _log_sigmoid(_wdot(tail, w2_ref[...]) + bg_ref[...]) * (1.0 / GLA_TAU)
    col_ref[3] = grp(G_GQ) * (HD ** -0.5)
    col_ref[4] = grp(G_GK)
    col_ref[5] = jnp.exp(glog)
    col_ref[6] = _rope(grp(G_RQ), cos_ref[...], sin_ref[...])
    col_ref[7] = _rope(grp(G_RK), cos_ref[...], sin_ref[...]) * (HD ** -0.5)
    col_ref[8] = jnp.broadcast_to(jnp.exp(rlg_ref[...]), col_ref.shape[1:])


def _rec_step_kernel(c_ref, v_ref, s0_ref, s1_ref, o_ref, sn0_ref, sn1_ref):
    for r, (s_ref, sn_ref) in enumerate(((s0_ref, sn0_ref), (s1_ref, sn1_ref))):
        for b in range(v_ref.shape[1]):
            for h in range(NH):
                q = c_ref[3 * r, b, h]
                k = c_ref[3 * r + 1, b, h]
                a = c_ref[3 * r + 2, b, h]
                v = v_ref[r, b, h]
                s = s_ref[b, h]
                qk = jnp.sum(q * k, axis=0, keepdims=True)
                o_ref[r, b, h] = qk * v + jnp.sum((q * a) * s, axis=0, keepdims=True)
                sn_ref[b, h] = a * s + k * v


REC_SEQS = 8


def rec_step(cols, v, state_gla, state_ret, li):
    nb = cols.shape[1]
    bt = math.gcd(nb, REC_SEQS)
    sspec = pl.BlockSpec((None, bt, NH, HD, HD), lambda b: (li, b, 0, 0, 0))
    ospec = pl.BlockSpec((bt, NH, HD, HD), lambda b: (b, 0, 0, 0))
    st = jax.ShapeDtypeStruct((nb, NH, HD, HD), F32)
    o, sn0, sn1 = pl.pallas_call(
        _rec_step_kernel,
        grid=(nb // bt,),
        in_specs=[pl.BlockSpec((6, bt, NH, HD, 1), lambda b: (0, b, 0, 0, 0)),
                  pl.BlockSpec((2, bt, NH, 1, HD), lambda b: (0, b, 0, 0, 0)), sspec, sspec],
        out_specs=[pl.BlockSpec((2, bt, NH, 1, HD), lambda b: (0, b, 0, 0, 0)), ospec, ospec],
        out_shape=[jax.ShapeDtypeStruct((2, nb, NH, 1, HD), F32), st, st],
        compiler_params=_cp(("arbitrary",)),
        name="rec_step",
    )(cols.reshape(6, nb, NH, HD, 1), v.reshape(2, nb, NH, 1, HD), state_gla, state_ret)
    o = o.reshape(2, nb, GW)
    return o[0], sn0, o[1], sn1


FOX_PAGES = 64


def _fox_bias_kernel(pt_ref, lfn_ref, lf_ref, ts_ref, pre_ref, o_ref, lf_scr):
    b = pl.program_id(0)
    n_pages = o_ref.shape[1]
    pg = o_ref.shape[2]
    for p in range(n_pages):
        page = lf_ref[pt_ref[b * n_pages + p]]
        for h in range(NH):
            lf_scr[h * n_pages + p:h * n_pages + p + 1, :] = page[h:h + 1, :]
    both = _dot_sel(lf_scr[...], ts_ref[...], "a")
    suf = both[:, :pg]
    tot = both[:, pg:]
    later = _dot_sel(pre_ref[...], tot, "b")
    for h in range(NH):
        r0, r1 = h * n_pages, (h + 1) * n_pages
        o_ref[h] = lfn_ref[h:h + 1, :] + later[r0:r1] + suf[r0:r1]


def _fox_sample_kernel(pt_ref, q_ref, kn_ref, vn_ref, bias_ref, *rest):
    g_n = bias_ref.shape[1]
    k_refs = rest[:g_n]
    v_refs = rest[g_n:2 * g_n]
    o_ref, m_scr, l_scr, acc_scr, s_scr = rest[2 * g_n:]
    c = pl.program_id(1)
    pg = k_refs[0].shape[1]

    @pl.when(c == 0)
    def _():
        m_scr[...] = jnp.full(m_scr.shape, NEG, F32)
        l_scr[...] = jnp.zeros_like(l_scr)
        acc_scr[...] = jnp.zeros_like(acc_scr)

    qb = jnp.broadcast_to(q_ref[...], (GW, pg))
    for g in range(g_n):
        prod = k_refs[g][...] * qb
        for h in range(NH):
            s_scr[h * g_n + g:h * g_n + g + 1, :] = jnp.sum(prod[h * HD:(h + 1) * HD], axis=0, keepdims=True)
    for h in range(NH):
        s_h = s_scr[h * g_n:(h + 1) * g_n, :] + bias_ref[h]
        m_old = m_scr[h:h + 1, :]
        m_new = jnp.maximum(m_old, jnp.max(jnp.max(s_h, axis=0, keepdims=True), axis=1, keepdims=True))
        alpha = jnp.exp(m_old - m_new)
        p = jnp.exp(s_h - m_new)
        m_scr[h:h + 1, :] = m_new
        l_scr[h:h + 1, :] = l_scr[h:h + 1, :] * alpha + jnp.sum(p, axis=0, keepdims=True)
        acc = acc_scr[h * HD:(h + 1) * HD, :] * alpha
        for g in range(g_n):
            acc = acc + p[g:g + 1, :] * v_refs[g][h * HD:(h + 1) * HD, :]
        acc_scr[h * HD:(h + 1) * HD, :] = acc

    @pl.when(c == pl.num_programs(1) - 1)
    def _():
        prod = q_ref[...] * kn_ref[...]
        for h in range(NH):
            s_self = jnp.sum(prod[h * HD:(h + 1) * HD], axis=0, keepdims=True)
            m_h = m_scr[h:h + 1, 0:1]
            m_fin = jnp.maximum(m_h, s_self)
            a_h = jnp.exp(m_h - m_fin)
            p_self = jnp.exp(s_self - m_fin)
            l_tot = jnp.sum(l_scr[h:h + 1, :], axis=1, keepdims=True) * a_h + p_self
            num = jnp.sum(acc_scr[h * HD:(h + 1) * HD, :], axis=1, keepdims=True) * a_h \
                + p_self * vn_ref[h * HD:(h + 1) * HD, :]
            o_ref[h * HD:(h + 1) * HD, :] = num / l_tot


def fox_sample(qkv, lf_new, cache_kt, cache_vt, cache_lf, page_table, li):
    nb, n_pages = page_table.shape
    n_phys = cache_lf.shape[0]
    pg = cache_kt.shape[-1]
    g_n = math.gcd(FOX_PAGES, n_pages)
    nc = n_pages // g_n
    qkv = qkv.reshape(3, nb, GW, 1)
    pt = page_table.reshape(-1)
    lfn = jnp.broadcast_to(lf_new[:, :, None], (nb, NH, pg))
    t = np.arange(pg)
    ts = jnp.asarray(np.concatenate([(t[:, None] > t[None, :]).astype(np.float32), np.ones((pg, pg), np.float32)], axis=1))
    r = np.arange(NH * n_pages)
    later = jnp.asarray(((r[:, None] // n_pages == r[None, :] // n_pages) & (r[None, :] > r[:, None])).astype(np.float32))
    bias = pl.pallas_call(
        _fox_bias_kernel,
        grid_spec=pltpu.PrefetchScalarGridSpec(
            num_scalar_prefetch=1,
            grid=(nb,),
            in_specs=[pl.BlockSpec((None, NH, pg), lambda b, pt: (b, 0, 0)),
                      pl.BlockSpec((n_phys, None, NH, pg), lambda b, pt: (0, li, 0, 0)),
                      pl.BlockSpec((pg, 2 * pg), lambda b, pt: (0, 0)),
                      pl.BlockSpec((NH * n_pages, NH * n_pages), lambda b, pt: (0, 0))],
            out_specs=pl.BlockSpec((None, NH, n_pages, pg), lambda b, pt: (b, 0, 0, 0)),
            scratch_shapes=[pltpu.VMEM((NH * n_pages, pg), F32)]),
        out_shape=jax.ShapeDtypeStruct((nb, NH, n_pages, pg), F32),
        compiler_params=_cp(("arbitrary",)),
        name="fox_bias",
    )(pt, lfn, cache_lf, ts, later)

    def page(g):
        return lambda b, c, pt: (pt[b * n_pages + c * g_n + g], li, 0, 0)

    cspec = lambda r: pl.BlockSpec((None, None, GW, 1), lambda b, c, pt: (r, b, 0, 0))
    in_specs = [cspec(0), cspec(1), cspec(2), pl.BlockSpec((None, NH, g_n, pg), lambda b, c, pt: (b, 0, c, 0))]
    in_specs += [pl.BlockSpec((None, None, GW, pg), page(g)) for g in range(g_n)]
    in_specs += [pl.BlockSpec((None, None, GW, pg), page(g)) for g in range(g_n)]
    o = pl.pallas_call(
        _fox_sample_kernel,
        grid_spec=pltpu.PrefetchScalarGridSpec(
            num_scalar_prefetch=1,
            grid=(nb, nc),
            in_specs=in_specs,
            out_specs=pl.BlockSpec((None, GW, 1), lambda b, c, pt: (b, 0, 0)),
            scratch_shapes=[pltpu.VMEM((NH, pg), F32), pltpu.VMEM((NH, pg), F32),
                            pltpu.VMEM((GW, pg), F32), pltpu.VMEM((NH * g_n, pg), F32)]),
        out_shape=jax.ShapeDtypeStruct((nb, GW, 1), F32),
        compiler_params=_cp(("arbitrary", "arbitrary")),
        name="fox_sample",
    )(pt, qkv, qkv, qkv, bias, *([cache_kt] * g_n), *([cache_vt] * g_n))
    return o.reshape(nb, GW)


def _sample_out_kernel(oa_ref, ob_ref, oc_ref, od_ref, go_ref, rg_ref, gng_ref, rng_ref, hs_ref, w_ref, x_ref, g1_ref, o_ref):
    hs = hs_ref[...]
    oc = _head_rms(oc_ref[...], hs, gng_ref[...]) * _silu(go_ref[...])
    od = _head_rms(od_ref[...], hs, rng_ref[...]) * _silu(rg_ref[...])
    cat = jnp.concatenate([oa_ref[...], ob_ref[...], oc, od], axis=1)
    o_ref[...] = x_ref[...] + g1_ref[...] * _wdot(cat, w_ref[...])


def _prep_w_in(w_in, li):
    wt = w_in.transpose(2, 0, 1)[:, li, :]
    o_ff = 5 * GW
    o_lr = o_ff + NH + 3 * GW
    return jnp.concatenate([wt[:o_ff], wt[o_ff + NH:o_lr], wt[o_lr + GLA_LR:],
                            wt[o_ff:o_ff + NH], wt[o_lr:o_lr + GLA_LR],
                            jnp.zeros((TAIL_W - NH - GLA_LR, wt.shape[1]), wt.dtype)], axis=0)


def _ffn_apply(li, x, mix, mod, g, fw, tm, rows_per_group, n_valid):
    if li % 2 == 0:
        w1, w3, w2 = fw
        return ffn_dense(x, mix, mod, li, g, w1, w3, w2, tm, w1.shape[1] // 2, rows_per_group)
    router_t, w1, w3, w2 = fw
    if mix is not None:
        x = out_proj(x, mix, mod, li, 512, rows_per_group)
    return moe_ffn(x, mod, li, g, router_t, w1, w3, w2, tm, w1.shape[2] // 2, rows_per_group, n_valid)


def kernel(x_prompt, x_sample, c_prompt, c_sample, state_conv, cache_k, cache_v, cache_logf, state_gla, state_ret, page_table, w_in, w_out, conv_w, conv_b, conv_ln_g, conv_ln_b, fox_qn_g, fox_kn_g, fox_fb, gla_w_gate2, gla_b_gate, gla_norm_g, ret_norm_g, norm1_g, norm2_g, w_ada, b_ada, ffn_w1, ffn_w3, ffn_w2, moe_router, moe_w1, moe_w3, moe_w2):
    nbp, l, d = x_prompt.shape
    nbs = x_sample.shape[0]
    depth = w_in.shape[0]
    n_pages, pg = page_table.shape[1], cache_k.shape[2]
    p_len = n_pages * pg
    mp = nbp * l
    ms = -(-nbs // 8) * 8
    ms_moe = 128

    mod = ada_mod(jnp.concatenate([c_prompt, c_sample], axis=0), w_ada, b_ada)
    mod = mod.reshape(depth, nbp + nbs, 6, d).transpose(0, 2, 1, 3)
    mod_p = mod[:, :, :nbp].reshape(depth, 6, nbp, 1, d)
    mod_s = jnp.zeros((depth, 6, ms_moe, d), F32).at[:, :, :nbs].set(mod[:, :, nbp:])
    cache_kt = cache_k.transpose(0, 1, 3, 4, 2).reshape(cache_k.shape[0], depth, GW, pg)
    cache_vt = cache_v.transpose(0, 1, 3, 4, 2).reshape(cache_v.shape[0], depth, GW, pg)
    cache_lf = cache_logf.transpose(0, 1, 3, 2)
    conv_state = state_conv.transpose(0, 2, 1, 3)
    cos_s, sin_s = _rope_tables(jnp.full((1,), p_len, F32))
    hs = jnp.asarray(_np_hsum())
    rlg = jnp.asarray(_ret_lg_row())

    xp = x_prompt.reshape(mp, d)
    xs = jnp.zeros((ms, d), F32).at[:nbs].set(x_sample.reshape(nbs, d))
    outs_p, outs_s = [], []
    k_all = jnp.zeros((nbp, depth, GW, l), F32)
    v_all = jnp.zeros((nbp, depth, GW, l), F32)
    lf_all = jnp.zeros((nbp, depth, NH, l), F32)
    for li in range(depth):
        n1 =norm1_g[li].reshape(1, d)
        n2 = norm2_g[li].reshape(1, d)
        w_in_f = _prep_w_in(w_in, li)
        w_in_b = w_in_f.astype(BF16)
        w_out_f = w_out[li]
        w_out_b = w_out_f.astype(BF16)
        if li % 2 == 0:
            fw_s = (ffn_w1[li // 2], ffn_w3[li // 2], ffn_w2[li // 2])
            fw = tuple(w.astype(BF16) for w in fw_s)
        else:
            fw = (jnp.pad(moe_router[li // 2], ((0, 0), (0, 128 - N_EXPERTS))), moe_w1[li // 2].astype(BF16),
                  moe_w3[li // 2].astype(BF16), moe_w2[li // 2].astype(BF16))
            fw_s = fw

        z = in_proj(xp, mod_p, li, n1, w_in_b, 512, l)
        out_a, conv_new = conv_prompt(z, nbp, l, conv_w[li], conv_b[li], conv_ln_g[li], conv_ln_b[li], 512)
        out_b, k_all, v_all, lf_all = fox_prompt(z, nbp, l, fox_qn_g[li], fox_kn_g[li], fox_fb[li], 512, li,
                                                 k_all, v_all, lf_all)
        out_c, gla_st = gla_prompt(z, nbp, l, gla_w_gate2[li], gla_b_gate[li], gla_norm_g[li], 256)
        out_d, ret_st = ret_prompt(z, nbp, l, ret_norm_g[li], 512)
        xp = _ffn_apply(li, xp, ((out_a, out_b, out_c, out_d), w_out_b), mod_p, n2, fw,
                        512 if li % 2 == 0 else 1024, l, mp)
        outs_p.append((conv_new, None, None, None, _state_from_blockdiag(gla_st), _state_from_blockdiag(ret_st)))

        zs = in_proj(xs, mod_s, li, n1, w_in_f, ms, ms)[:nbs]
        row = lambda v_: v_.reshape(1, GW)
        w2p = jnp.zeros((TAIL_W, GW), F32).at[NH:NH + GLA_LR].set(gla_w_gate2[li])
        fbp = jnp.zeros((1, TAIL_W), F32).at[0, :NH].set(fox_fb[li])
        r32 = jax.ShapeDtypeStruct((nbs, GW), F32)
        oa, cn, lfs, cols = pl.pallas_call(
            _sample_rows_kernel,
            out_shape=[r32, jax.ShapeDtypeStruct((CONV_K - 1, nbs, GW), F32),
                       jax.ShapeDtypeStruct((nbs, TAIL_W), F32), jax.ShapeDtypeStruct((9, nbs, GW), F32)],
            compiler_params=pltpu.CompilerParams(vmem_limit_bytes=VMEM_LIMIT),
            name="sample_rows",
        )(zs, conv_state[li], conv_w[li], row(conv_b[li]), row(conv_ln_g[li]), row(conv_ln_b[li]),
          row(jnp.tile(fox_qn_g[li], NH)), row(jnp.tile(fox_kn_g[li], NH)), fbp, w2p, row(gla_b_gate[li]),
          cos_s, sin_s, rlg, hs)
        grp = lambda g_: zs[:, g_ * GW:(g_ + 1) * GW]
        fk, f_v = cols[1], cols[2]
        ob = fox_sample(cols[:3], lfs[:, :NH], cache_kt, cache_vt, cache_lf, page_table, li)
        oc, gla_new, od, ret_new = rec_step(cols[3:], jnp.stack([grp(G_GV), grp(G_RV)]), state_gla, state_ret, li)
        pad = lambda v_: jnp.zeros((ms, v_.shape[1]), F32).at[:nbs].set(v_)
        xs = pl.pallas_call(
            _sample_out_kernel,
            out_shape=jax.ShapeDtypeStruct((ms, d), F32),
            compiler_params=pltpu.CompilerParams(vmem_limit_bytes=VMEM_LIMIT),
            name="sample_out",
        )(pad(oa), pad(ob), pad(oc), pad(od), pad(grp(G_GO)), pad(grp(G_RG)), row(gla_norm_g[li]), row(ret_norm_g[li]),
          hs, w_out_f, xs, mod_s[li, MOD_G1, :ms])
        if li % 2 == 0:
            xs = _ffn_apply(li, xs, None, mod_s, n2, fw_s, ms, ms, nbs)
        else:
            wide = jnp.zeros((ms_moe, d), F32).at[:ms].set(xs)
            xs = _ffn_apply(li, wide, None, mod_s, n2, fw_s, ms_moe, ms_moe, nbs)[:ms]
        outs_s.append((cn.transpose(1, 0, 2), fk.reshape(nbs, 1, NH, HD), f_v.reshape(nbs, 1, NH, HD),
                       lfs[:, :NH].reshape(nbs, 1, NH), gla_new, ret_new))

    st = lambda lst, i, ax: jnp.stack([s[i] for s in lst], axis=ax)
    heads_last = lambda a: a.reshape(nbp, depth, NH, HD, l).transpose(0, 1, 4, 2, 3)
    return (xp.reshape(nbp, l, d), xs[:nbs].reshape(nbs, 1, d),
            st(outs_p, 0, 0), st(outs_s, 0, 0),
            heads_last(k_all), st(outs_s, 1, 1),
            heads_last(v_all), st(outs_s, 2, 1),
            lf_all.transpose(0, 1, 3, 2), st(outs_s, 3, 1),
            st(outs_p, 4, 0), st(outs_s, 4, 0),
            st(outs_p, 5, 0), st(outs_s, 5, 0))
```

```python
import functools
import math

import numpy as np
import jax
import jax.numpy as jnp
from jax import lax
from jax.experimental import pallas as pl
from jax.experimental.pallas import tpu as pltpu

F32 = jnp.float32
BF16 = jnp.bfloat16
HI = lax.Precision.HIGHEST

D_MODEL = 1024
NH = 4
HD = 64
GW = NH * HD
CONV_K = 31
GLA_LR = 16
GLA_TAU = 16.0
GLA_CHUNK = 16
RET_CHUNK = 256
N_EXPERTS = 8
EPS = 1e-6
N_MAIN = 13 * GW
TAIL_W = 128
IN_PAD = N_MAIN + TAIL_W
TAIL_BLK = N_MAIN // TAIL_W
(G_CA, G_CG, G_FQ, G_FK, G_FV, G_GQ, G_GK, G_GV, G_GO, G_RQ, G_RK, G_RV, G_RG) = range(13)
VMEM_LIMIT = 56 * 1024 * 1024
MOE_ROWS = 128
NEG = -1e30
LOG2E = math.log2(math.e)


def _cp(sem, vmem=VMEM_LIMIT):
    return pltpu.CompilerParams(dimension_semantics=sem, vmem_limit_bytes=vmem)


def _silu(x):
    return x * jax.nn.sigmoid(x)


def _log_sigmoid(x):
    return jnp.minimum(x, 0.0) - jnp.log1p(jnp.exp(-jnp.abs(x)))


def _dot(a, b):
    return jnp.dot(a, b, preferred_element_type=F32)


def _split3(x):
    hi = x.astype(BF16)
    r1 = x - hi.astype(F32)
    mid = r1.astype(BF16)
    lo = (r1 - mid.astype(F32)).astype(BF16)
    return hi, mid, lo


def _dot_sel(a, b, data):
    if data == "a":
        sel = b.astype(BF16)
        return sum(_dot(p, sel) for p in _split3(a))
    sel = a.astype(BF16)
    return sum(_dot(sel, p) for p in _split3(b))


def _head_ids(shape, dim):
    return lax.shift_right_logical(lax.broadcasted_iota(jnp.int32, shape, dim), int(math.log2(HD)))


def _dot_nt(a, b, **kw):
    return lax.dot_general(a, b, (((1,), (1,)), ((), ())), preferred_element_type=F32, **kw)


def _dot_tn(a, b, **kw):
    return lax.dot_general(a, b, (((0,), (0,)), ((), ())), preferred_element_type=F32, **kw)


def _wdot(a, w, nt=False):
    if w.dtype == BF16:
        a, kw = a.astype(BF16), {}
    else:
        kw = {"precision": HI}
    return _dot_nt(a, w, **kw) if nt else jnp.dot(a, w, preferred_element_type=F32, **kw)


def _modnorm(x, g, sc, sh):
    y = x * lax.rsqrt(jnp.mean(x * x, axis=-1, keepdims=True) + EPS)
    return (y * g) * (1.0 + sc) + sh


def _head_rms(x, hsum, g):
    ms = _dot_sel(x * x, hsum, "a") * (1.0 / HD)
    return x * lax.rsqrt(ms + EPS) * g


def _np_hsum():
    i = np.arange(GW) // HD
    return (i[:, None] == i[None, :]).astype(np.float32)


def _ada_kernel(c_ref, w_ref, b_ref, o_ref):
    c = c_ref[...]
    o_ref[...] = _wdot(_silu(c), w_ref[...]) + b_ref[...]


def ada_mod(c_all, w_ada, b_ada):
    depth, d, n = w_ada.shape
    r = c_all.shape[0]
    tn = 1536
    return pl.pallas_call(
        _ada_kernel,
        grid=(depth, n // tn),
        in_specs=[pl.BlockSpec((r, d), lambda l, j: (0, 0)),
                  pl.BlockSpec((None, d, tn), lambda l, j: (l, 0, j)),
                  pl.BlockSpec((None, 1, tn), lambda l, j: (l, 0, j))],
        out_specs=pl.BlockSpec((None, r, tn), lambda l, j: (l, 0, j)),
        out_shape=jax.ShapeDtypeStruct((depth, r, n), F32),
        compiler_params=_cp(("arbitrary", "arbitrary")),
        name="ada_mod",
    )(c_all, w_ada, b_ada.reshape(depth, 1, n))


def _in_proj_kernel(x_ref, sc_ref, sh_ref, g_ref, w_ref, z_ref):
    h = _modnorm(x_ref[...], g_ref[...], sc_ref[...], sh_ref[...])
    z_ref[...] = _wdot(h, w_ref[...], nt=True)


MOD_SH1, MOD_SC1, MOD_G1, MOD_SH2, MOD_SC2, MOD_G2 = range(6)


def _mod_spec(mod, li, k, tm, rows_per_group):
    if mod.ndim == 5:
        tiles = rows_per_group // tm
        return pl.BlockSpec((None, None, None, 1, mod.shape[-1]), lambda i, *_: (li, k, i // tiles, 0, 0))
    return pl.BlockSpec((None, None, tm, mod.shape[-1]), lambda i, *_: (li, k, 0, 0))


def in_proj(x, mod, li, g, w, tm, rows_per_group):
    m, d = x.shape
    n = w.shape[0]
    return pl.pallas_call(
        _in_proj_kernel,
        grid=(m // tm,),
        in_specs=[pl.BlockSpec((tm, d), lambda i: (i, 0)),
                  _mod_spec(mod, li, MOD_SC1, tm, rows_per_group), _mod_spec(mod, li, MOD_SH1, tm, rows_per_group),
                  pl.BlockSpec((1, d), lambda i: (0, 0)),
                  pl.BlockSpec((n, d), lambda i: (0, 0))],
        out_specs=pl.BlockSpec((tm, n), lambda i: (i, 0)),
        out_shape=jax.ShapeDtypeStruct((m, n), F32),
        compiler_params=_cp(("arbitrary",)),
        name="in_proj",
    )(x, mod, mod, g, w)


def _block_input(x_ref, mix_refs):
    x = x_ref[...]
    if mix_refs:
        *parts, w_ref, g1_ref = mix_refs
        x = x + g1_ref[...] * _dot(jnp.concatenate([p[...] for p in parts], axis=1), w_ref[...])
    return x


def _mix_operands(mix, mod, li, tm, rows_per_group):
    if mix is None:
        return [], []
    parts, w_out = mix
    specs = [pl.BlockSpec((tm, GW), lambda i, *_: (i, 0))] * 4 + [
        pl.BlockSpec(w_out.shape, lambda i, *_: (0, 0)), _mod_spec(mod, li, MOD_G1, tm, rows_per_group)]
    return specs, [*parts, w_out, mod]


def _out_proj_kernel(x_ref, *rest):
    rest[-1][...] = _block_input(x_ref, rest[:-1])


def out_proj(x, mix, mod, li, tm, rows_per_group):
    m, d = x.shape
    mix_specs, mix_args = _mix_operands(mix, mod, li, tm, rows_per_group)
    return pl.pallas_call(
        _out_proj_kernel,
        grid=(m // tm,),
        in_specs=[pl.BlockSpec((tm, d), lambda i: (i, 0))] + mix_specs,
        out_specs=pl.BlockSpec((tm, d), lambda i: (i, 0)),
        out_shape=jax.ShapeDtypeStruct((m, d), F32),
        compiler_params=_cp(("arbitrary",)),
        name="out_proj",
    )(x, *mix_args)


def _ffn_kernel(n_mix, x_ref, *rest):
    mix_refs, rest = rest[:n_mix], rest[n_mix:]
    sc_ref, sh_ref, g_ref, gate_ref, w1_ref, w3_ref, w2_ref, o_ref, h_scr, acc_scr, x1_scr = rest
    j = pl.program_id(1)

    @pl.when(j == 0)
    def _():
        x1 = _block_input(x_ref, mix_refs)
        x1_scr[...] = x1
        h_scr[...] = _modnorm(x1, g_ref[...], sc_ref[...], sh_ref[...]).astype(h_scr.dtype)
        acc_scr[...] = jnp.zeros_like(acc_scr)

    h = h_scr[...]
    acc_scr[...] += _wdot(_silu(_wdot(h, w1_ref[...])) * _wdot(h, w3_ref[...]), w2_ref[...])

    @pl.when(j == pl.num_programs(1) - 1)
    def _():
        o_ref[...] = x1_scr[...] + gate_ref[...] * acc_scr[...]


def ffn_dense(x, mix, mod, li, g, w1, w3, w2, tm, tf, rows_per_group):
    m, d = x.shape
    ff = w1.shape[1]
    mix_specs, mix_args = _mix_operands(mix, mod, li, tm, rows_per_group)
    spec = lambda k: _mod_spec(mod, li, k, tm, rows_per_group)
    return pl.pallas_call(
        functools.partial(_ffn_kernel, len(mix_args)),
        grid=(m // tm, ff // tf),
        in_specs=[pl.BlockSpec((tm, d), lambda i, j: (i, 0))] + mix_specs + [
            spec(MOD_SC2), spec(MOD_SH2),
            pl.BlockSpec((1, d), lambda i, j: (0, 0)),
            spec(MOD_G2),
            pl.BlockSpec((d, tf), lambda i, j: (0, j)),
            pl.BlockSpec((d, tf), lambda i, j: (0, j)),
            pl.BlockSpec((tf, d), lambda i, j: (j, 0))],
        out_specs=pl.BlockSpec((tm, d), lambda i, j: (i, 0)),
        out_shape=jax.ShapeDtypeStruct((m, d), F32),
        scratch_shapes=[pltpu.VMEM((tm, d), w1.dtype), pltpu.VMEM((tm, d), F32), pltpu.VMEM((tm, d), F32)],
        compiler_params=_cp(("arbitrary", "arbitrary")),
        name="ffn_dense",
    )(x, *mix_args, mod, mod, g, mod, w1, w3, w2)


def _route_kernel(n_valid, x_ref, sc_ref, sh_ref, g_ref, rt_ref, tri_ref, h_ref, rank_ref, gate_ref, cnt_ref):
    t = x_ref.shape[0]
    h = _modnorm(x_ref[...], g_ref[...], sc_ref[...], sh_ref[...])
    h_ref[...] = h.astype(BF16)
    logits = jnp.dot(h, rt_ref[...], preferred_element_type=F32, precision=HI).T[:N_EXPERTS]
    e_iota = lax.broadcasted_iota(jnp.int32, logits.shape, 0).astype(F32)
    m1 = jnp.max(logits, axis=0, keepdims=True)
    i1 = jnp.min(jnp.where(logits == m1, e_iota, float(N_EXPERTS)), axis=0, keepdims=True)
    sel1 = e_iota == i1
    rest = jnp.where(sel1, -jnp.inf, logits)
    m2 = jnp.max(rest, axis=0, keepdims=True)
    i2 = jnp.min(jnp.where(rest == m2, e_iota, float(N_EXPERTS)), axis=0, keepdims=True)
    sel2 = e_iota == i2
    e2 = jnp.exp(m2 - m1)
    den = 1.0 + e2
    gate = jnp.where(sel1, 1.0 / den, 0.0) + jnp.where(sel2, e2 / den, 0.0)
    tok = pl.program_id(0) * t + lax.broadcasted_iota(jnp.int32, logits.shape, 1)
    sel = jnp.logical_and(jnp.logical_or(sel1, sel2), tok < n_valid)
    incl = _dot(jnp.where(sel, 1.0, 0.0).astype(BF16), tri_ref[...])
    rank_ref[...] = jnp.where(sel, incl - 1.0, -1.0)
    gate_ref[...] = gate
    cnt_ref[...] = jnp.broadcast_to(incl[:, t - 1:t], cnt_ref.shape)


def _moe_kernel(cnt_ref, h_ref, rank_ref, gate_ref, x_ref, g2_ref, w1_ref, w3_ref, w2_ref, o_ref, xs_scr, y_scr):
    i, e, j = pl.program_id(0), pl.program_id(1), pl.program_id(2)
    nff = pl.num_programs(2)
    t = h_ref.shape[0]
    r = min(MOE_ROWS, t)
    cnt = cnt_ref[i * N_EXPERTS + e]
    nb = (cnt + (r - 1)) // r
    row = lax.broadcasted_iota(jnp.int32, (r, t), 0).astype(F32)

    @pl.when(jnp.logical_and(e == 0, j == 0))
    def _():
        o_ref[...] = jnp.zeros_like(o_ref)
        y_scr[...] = jnp.zeros_like(y_scr)

    @pl.when(j == 0)
    def _():
        def gather(s, c):
            r0 = pl.multiple_of(s * r, r)
            onehot = jnp.where(rank_ref[...] == row + (s * r).astype(F32), 1.0, 0.0).astype(BF16)
            xs_scr[pl.ds(r0, r), :] = _dot(onehot, h_ref[...]).astype(BF16)
            y_scr[pl.ds(r0, r), :] = jnp.zeros((r, y_scr.shape[1]), F32)
            return c
        lax.fori_loop(0, nb, gather, 0)

    def ffn(s, c, rows=r):
        r0 = pl.multiple_of(s * r, r)
        xs = xs_scr[pl.ds(r0, rows), :]
        act = (_silu(_dot(xs, w1_ref[...])) * _dot(xs, w3_ref[...])).astype(BF16)
        y_scr[pl.ds(r0, rows), :] += _dot(act, w2_ref[...])
        return c

    n_full = cnt // r
    rem = cnt - n_full * r
    lax.fori_loop(0, n_full, ffn, 0)
    @pl.when(rem > 0)
    def _():
        ffn(n_full, 0)

    @pl.when(j == nff - 1)
    def _():
        rp = min(max(r, 256), t)
        rowp = lax.broadcasted_iota(jnp.int32, (rp, t), 0).astype(F32)

        def scatter(s, c):
            r0 = pl.multiple_of(s * rp, rp)
            hit = rank_ref[...] == rowp + (s * rp).astype(F32)
            wgt = jnp.where(hit, gate_ref[...], 0.0).astype(BF16)
            o_ref[...] += _dot_tn(wgt, y_scr[pl.ds(r0, rp), :].astype(BF16))
            return c
        lax.fori_loop(0, (nb * r + rp - 1) // rp, scatter, 0)

    @pl.when(jnp.logical_and(e == N_EXPERTS - 1, j == nff - 1))
    def _():
        o_ref[...] = x_ref[...] + g2_ref[...] * o_ref[...]


def moe_ffn(x, mod, li, g, router_t, w1, w3, w2, t, tf, rows_per_group, n_valid):
    m, d = x.shape
    nt = m // t
    ne, _, ff = w1.shape
    assert ne == N_EXPERTS
    tri = jnp.asarray(np.triu(np.ones((t, t), np.float32)), BF16)
    h, rank, gate, cnt = pl.pallas_call(
        functools.partial(_route_kernel, n_valid),
        grid=(nt,),
        in_specs=[pl.BlockSpec((t, d), lambda i: (i, 0)),
                  _mod_spec(mod, li, MOD_SC2, t, rows_per_group), _mod_spec(mod, li, MOD_SH2, t, rows_per_group),
                  pl.BlockSpec((1, d), lambda i: (0, 0)),
                  pl.BlockSpec((d, 128), lambda i: (0, 0)),
                  pl.BlockSpec((t, t), lambda i: (0, 0))],
        out_specs=[pl.BlockSpec((t, d), lambda i: (i, 0)),
                   pl.BlockSpec((None, ne, t), lambda i: (i, 0, 0)),
                   pl.BlockSpec((None, ne, t), lambda i: (i, 0, 0)),
                   pl.BlockSpec((None, ne, 128), lambda i: (i, 0, 0))],
        out_shape=[jax.ShapeDtypeStruct((m, d), BF16),
                   jax.ShapeDtypeStruct((nt, ne, t), F32),
                   jax.ShapeDtypeStruct((nt, ne, t), F32),
                   jax.ShapeDtypeStruct((nt, ne, 128), F32)],
        compiler_params=_cp(("arbitrary",)),
        name="moe_route",
    )(x, mod, mod, g, router_t, tri)
    counts = cnt[:, :, 0].astype(jnp.int32).reshape(nt * ne)
    rank = rank.reshape(nt, ne, 1, t)
    gate = gate.reshape(nt, ne, 1, t)
    out = pl.pallas_call(
        _moe_kernel,
        grid_spec=pltpu.PrefetchScalarGridSpec(
            num_scalar_prefetch=1,
            grid=(nt, ne, ff // tf),
            in_specs=[pl.BlockSpec((t, d), lambda i, e, j, nb: (i, 0)),
                      pl.BlockSpec((None, None, 1, t), lambda i, e, j, nb: (i, e, 0, 0)),
                      pl.BlockSpec((None, None, 1, t), lambda i, e, j, nb: (i, e, 0, 0)),
                      pl.BlockSpec((t, d), lambda i, e, j, nb: (i, 0)),
                      _mod_spec(mod, li, MOD_G2, t, rows_per_group),
                      pl.BlockSpec((None, d, tf), lambda i, e, j, nb: (e, 0, j)),
                      pl.BlockSpec((None, d, tf), lambda i, e, j, nb: (e, 0, j)),
                      pl.BlockSpec((None, tf, d), lambda i, e, j, nb: (e, j, 0))],
            out_specs=pl.BlockSpec((t, d), lambda i, e, j, nb: (i, 0)),
            scratch_shapes=[pltpu.VMEM((t, d), BF16), pltpu.VMEM((t, d), F32)]),
        out_shape=jax.ShapeDtypeStruct((m, d), F32),
        compiler_params=_cp(("arbitrary", "arbitrary", "arbitrary")),
        name="moe_ffn",
    )(counts, h, rank, gate, x, mod, w1, w3, w2)
    return out


def _conv_kernel(a_ref, g_ref, w_ref, b_ref, lg_ref, lb_ref, o_ref, cn_ref, ext_scr, sh_scr):
    tc = a_ref.shape[0]
    halo = 32
    off = halo - (CONV_K - 1)

    @pl.when(pl.program_id(1) == 0)
    def _():
        ext_scr[0:halo, :] = jnp.zeros((halo, GW), F32)

    ext_scr[halo:halo + tc, :] = a_ref[...] * jax.nn.sigmoid(g_ref[...])
    for s in range(1, 8):
        sh_scr[s - 1] = ext_scr[s:s + tc + halo - 8, :]
    rc = 64
    for c in range(tc // rc):
        acc = jnp.zeros((rc, GW), F32)
        for k in range(CONV_K):
            s, r0 = (k + off) % 8, c * rc + (k + off) // 8 * 8
            rows = ext_scr[r0:r0 + rc, :] if s == 0 else sh_scr[s - 1, r0:r0 + rc, :]
            acc = acc + w_ref[k:k + 1, :] * rows
        y = acc + b_ref[...]
        mu = jnp.mean(y, axis=-1, keepdims=True)
        yc = y - mu
        var = jnp.mean(yc * yc, axis=-1, keepdims=True)
        o_ref[c * rc:(c + 1) * rc, :] = _silu(yc * lax.rsqrt(var + EPS) * lg_ref[...] + lb_ref[...]).astype(o_ref.dtype)
    cn_ref[...] = ext_scr[tc + off:tc + halo, :]
    ext_scr[0:halo, :] = ext_scr[tc:tc + halo, :]


def conv_prompt(z, nb, l, conv_w, conv_b, ln_g, ln_b, tc):
    m = z.shape[0]
    nl = l // tc
    row = lambda v: v.reshape(1, GW)
    return pl.pallas_call(
        _conv_kernel,
        grid=(nb, nl),
        in_specs=[pl.BlockSpec((tc, GW), lambda b, i: (b * nl + i, G_CA)),
                  pl.BlockSpec((tc, GW), lambda b, i: (b * nl + i, G_CG)),
                  pl.BlockSpec((CONV_K, GW), lambda b, i: (0, 0)),
                  pl.BlockSpec((1, GW), lambda b, i: (0, 0)),
                  pl.BlockSpec((1, GW), lambda b, i: (0, 0)),
                  pl.BlockSpec((1, GW), lambda b, i: (0, 0))],
        out_specs=[pl.BlockSpec((tc, GW), lambda b, i: (b * nl + i, 0)),
                   pl.BlockSpec((None, CONV_K - 1, GW), lambda b, i: (b, 0, 0))],
        out_shape=[jax.ShapeDtypeStruct((m, GW), BF16),
                   jax.ShapeDtypeStruct((nb, CONV_K - 1, GW), F32)],
        scratch_shapes=[pltpu.VMEM((tc + 32, GW), F32), pltpu.VMEM((7, tc + 24, GW), F32)],
        compiler_params=_cp(("arbitrary", "arbitrary")),
        name="conv_prompt",
    )(z, z, conv_w, row(conv_b), row(ln_g), row(ln_b))


def _fox_prep_kernel(q_ref, k_ref, v_ref, t_ref, gq_ref, gk_ref, fb_ref, hs_ref, tri_ref,
                     k_all, v_all, lf_all, qb_ref, kb_ref, vb_ref, kt_ref, vt_ref, lft_ref, fc_ref, carry_scr):
    del k_all, v_all, lf_all

    @pl.when(pl.program_id(1) == 0)
    def _():
        carry_scr[...] = jnp.zeros_like(carry_scr)

    hs = hs_ref[...]
    qn = _head_rms(q_ref[...], hs, gq_ref[...])
    kn = _head_rms(k_ref[...], hs, gk_ref[...])
    v = v_ref[...]
    vt = v.T
    qb_ref[...] = (qn * (HD ** -0.5 * LOG2E)).astype(BF16)
    kb_ref[...] = kn.astype(BF16)
    vb_ref[...] = vt.astype(BF16)
    kt_ref[...] = kn.T
    vt_ref[...] = vt
    lf = _log_sigmoid(t_ref[...] + fb_ref[...])
    lft_ref[...] = lf.T[:NH]
    cum = _dot_sel(tri_ref[...], lf, "b") + carry_scr[...]
    fc_ref[...] = cum * LOG2E
    carry_scr[...] = cum[cum.shape[0] - 1:, :]


def _fox_flash_kernel(q_ref, k_ref, vt_ref, fc_ref, o_ref, ot_scr):
    tq = q_ref.shape[0]
    tk = tq
    qi = pl.program_id(1)
    q = q_ref[...]
    head = _head_ids((1, GW), 1)
    kpos = lax.broadcasted_iota(jnp.int32, (tk, tq), 0)
    qpos = lax.broadcasted_iota(jnp.int32, (tk, tq), 1)
    qhs = [jnp.where(head == h, q, jnp.zeros_like(q)) for h in range(NH)]

    def step(jb, carry, diagonal):
        k0 = pl.multiple_of(jb * tk, tk)
        kb = k_ref[pl.ds(k0, tk), :]
        fcb = fc_ref[pl.ds(k0, tk), :]
        out = []
        for h in range(NH):
            m, l, acc = carry[h]
            s = _dot_nt(kb, qhs[h]) - fcb[:, h:h + 1]
            if diagonal:
                s = jnp.where(kpos <= qpos, s, NEG)
            m_new = jnp.maximum(m, jnp.max(s, axis=0, keepdims=True))
            alpha = jnp.exp2(m - m_new)
            p = jnp.exp2(s - m_new)
            l = alpha * l + jnp.sum(p, axis=0, keepdims=True)
            acc = alpha * acc + _dot(vt_ref[jb, h * HD:(h + 1) * HD, :], p.astype(BF16))
            out.append((m_new, l, acc))
        return tuple(out)

    init = tuple((jnp.full((1, tq), NEG, F32), jnp.zeros((1, tq), F32), jnp.zeros((HD, tq), F32)) for _ in range(NH))
    carry = lax.fori_loop(0, qi, functools.partial(step, diagonal=False), init)
    carry = step(qi, carry, True)
    for h in range(NH):
        _, l, acc = carry[h]
        ot_scr[h * HD:(h + 1) * HD, :] = acc * (1.0 / l)
    o_ref[...] = ot_scr[...].T.astype(o_ref.dtype)


def fox_prompt(z, nb, l, gq, gk, fb, tq, li, k_all, v_all, lf_all):
    m = z.shape[0]
    tp = tq
    nl = l // tp
    hs = jnp.asarray(_np_hsum())
    tri = jnp.asarray(np.tril(np.ones((tp, tp), np.float32)))
    row = lambda v: jnp.tile(v, NH).reshape(1, GW)
    fbp = jnp.zeros((1, TAIL_W), F32).at[0, :NH].set(fb)
    blk = lambda g: pl.BlockSpec((tp, GW), lambda b, i: (b * nl + i, g))
    oblk = lambda w: pl.BlockSpec((tp, w), lambda b, i: (b * nl + i, 0))
    anyspec = pl.BlockSpec(memory_space=pl.ANY)
    stacked = lambda r: pl.BlockSpec((None, None, r, tp), lambda b, i: (b, li, 0, i))
    qb, kb, vb, k_all, v_all, lf_all, fc = pl.pallas_call(
        _fox_prep_kernel,
        grid=(nb, nl),
        in_specs=[blk(G_FQ), blk(G_FK), blk(G_FV),
                  pl.BlockSpec((tp, TAIL_W), lambda b, i: (b * nl + i, TAIL_BLK)),
                  pl.BlockSpec((1, GW), lambda b, i: (0, 0)),
                  pl.BlockSpec((1, GW), lambda b, i: (0, 0)),
                  pl.BlockSpec((1, TAIL_W), lambda b, i: (0, 0)),
                  pl.BlockSpec((GW, GW), lambda b, i: (0, 0)),
                  pl.BlockSpec((tp, tp), lambda b, i: (0, 0)),
                  anyspec, anyspec, anyspec],
        out_specs=[oblk(GW)] * 2 + [pl.BlockSpec((None, None, GW, tp), lambda b, i: (b, i, 0, 0)),
                                    stacked(GW), stacked(GW), stacked(NH), oblk(TAIL_W)],
        out_shape=[jax.ShapeDtypeStruct((m, GW), BF16)] * 2 + [jax.ShapeDtypeStruct((nb, nl, GW, tp), BF16),
                   jax.ShapeDtypeStruct(k_all.shape, F32), jax.ShapeDtypeStruct(v_all.shape, F32),
                   jax.ShapeDtypeStruct(lf_all.shape, F32), jax.ShapeDtypeStruct((m, TAIL_W), F32)],
        input_output_aliases={9: 3, 10: 4, 11: 5},
        scratch_shapes=[pltpu.VMEM((1, TAIL_W), F32)],
        compiler_params=_cp(("arbitrary", "arbitrary")),
        name="fox_prep",
    )(z, z, z, z, row(gq), row(gk), fbp, hs, tri, k_all, v_all, lf_all)
    nq = l // tq
    o = pl.pallas_call(
        _fox_flash_kernel,
        grid=(nb, nq),
        in_specs=[pl.BlockSpec((tq, GW), lambda b, i: (b * nq + i, 0)),
                  pl.BlockSpec((l, GW), lambda b, i: (b, 0)),
                  pl.BlockSpec((None, nq, GW, tq), lambda b, i: (b, 0, 0, 0)),
                  pl.BlockSpec((l, TAIL_W), lambda b, i: (b, 0))],
        out_specs=pl.BlockSpec((tq, GW), lambda b, i: (b * nq + i, 0)),
        out_shape=jax.ShapeDtypeStruct((m, GW), BF16),
        scratch_shapes=[pltpu.VMEM((GW, tq), F32)],
        compiler_params=_cp(("arbitrary", "arbitrary")),
        name="fox_flash",
    )(qb, kb, vb, fc)
    return o, k_all, v_all, lf_all


GLA_SAFE_LOG = -80.0
GLA_SUB = 256


def _gla_kernel(q_ref, k_ref, v_ref, t_ref, go_ref, w2_ref, bg_ref, gn_ref, hs_ref, cs_ref, bd_ref,
                o_ref, st_ref, st_scr, x_scr, o_scr):
    tg = q_ref.shape[0]
    c = GLA_CHUNK

    @pl.when(pl.program_id(1) == 0)
    def _():
        st_scr[...] = jnp.zeros_like(st_scr)

    hs = hs_ref[...]
    bd = bd_ref[...]
    glog_all = _log_sigmoid(_dot(t_ref[...].astype(BF16), w2_ref[...]) + bg_ref[...]) * (1.0 / GLA_TAU)
    ts = min(GLA_SUB, tg)
    lower = lax.broadcasted_iota(jnp.int32, (ts, ts), 0) >= lax.broadcasted_iota(jnp.int32, (ts, ts), 1)

    def sub_tile(n):
        rows = slice(n * ts, (n + 1) * ts)
        glog = glog_all[rows]
        q = q_ref[rows, :] * (HD ** -0.5)
        k = k_ref[rows, :]
        v = v_ref[rows, :]
        vb = v.astype(BF16)
        bfull = _dot_sel(jnp.where(lower, 1.0, 0.0), glog, "b")
        safe = jnp.min(bfull) >= GLA_SAFE_LOG

        @pl.when(safe)
        def _():
            head = _head_ids((1, GW), 1)
            btot = bfull[ts - 1:ts]
            qe = (q * jnp.exp(bfull)).astype(BF16)
            kinv = (k * jnp.exp(-bfull)).astype(BF16)
            kend = (k * jnp.exp(btot - bfull)).astype(BF16)
            o = _dot_nt(qe, st_scr[...].astype(BF16))
            for h in range(NH):
                s = _dot_nt(jnp.where(head == h, qe, jnp.zeros_like(qe)), kinv)
                o = o + jnp.where(head == h, _dot(jnp.where(lower, s, 0.0).astype(BF16), vb), 0.0)
            o_scr[rows, :] = o
            st_scr[...] = st_scr[...] * jnp.exp(btot) + _dot_tn(vb, kend) * bd

        @pl.when(jnp.logical_not(safe))
        def _():
            hs_b = hs.astype(BF16)
            cs = cs_ref[0:ts, 0:ts]
            bcum = _dot_sel(jnp.where(lower, cs, 0.0), glog, "b")
            blast = _dot_sel(cs, glog, "b")
            qe = (q * jnp.exp(bcum)).astype(BF16)
            ke = (k * jnp.exp(blast - bcum)).astype(BF16)
            ii = lax.broadcasted_iota(jnp.int32, (c, GW), 0)
            for m in range(ts // c):
                r0 = m * c
                bc = bcum[r0:r0 + c]
                qc = q[r0:r0 + c]
                kc = k[r0:r0 + c]
                vc = v[r0:r0 + c]
                for j in range(c):
                    ex = jnp.exp(jnp.where(ii >= j, bc - bc[j:j + 1], -jnp.inf))
                    x_scr[j * c:(j + 1) * c, :] = (qc * ex * kc[j:j + 1]).astype(BF16)
                att = _dot(x_scr[...], hs_b)
                o = _dot_nt(qe[r0:r0 + c], st_scr[...].astype(BF16))
                for j in range(c):
                    o = o + att[j * c:(j + 1) * c] * vc[j:j + 1]
                o_scr[n * ts + r0:n * ts + r0 + c, :] = o
                kv = _dot_tn(vb[r0:r0 + c], ke[r0:r0 + c])
                st_scr[...] = st_scr[...] * jnp.exp(blast[r0:r0 + 1]) + kv * bd

    for n in range(tg // ts):
        sub_tile(n)
    o_ref[...] = (_head_rms(o_scr[...], hs, gn_ref[...]) * _silu(go_ref[...])).astype(o_ref.dtype)
    st_ref[...] = st_scr[...]


def gla_prompt(z, nb, l, w_gate2, b_gate, norm_g, tg):
    m = z.shape[0]
    nl = l // tg
    hs = jnp.asarray(_np_hsum())
    ch = np.arange(tg) // GLA_CHUNK
    cs = jnp.asarray((ch[:, None] == ch[None, :]).astype(np.float32))
    w2p = jnp.zeros((TAIL_W, GW), F32).at[NH:NH + GLA_LR].set(w_gate2).astype(BF16)
    blk = lambda g: pl.BlockSpec((tg, GW), lambda b, i: (b * nl + i, g))
    full = lambda r, c: pl.BlockSpec((r, c), lambda b, i: (0, 0))
    return pl.pallas_call(
        _gla_kernel,
        grid=(nb, nl),
        in_specs=[blk(G_GQ), blk(G_GK), blk(G_GV),
                  pl.BlockSpec((tg, TAIL_W), lambda b, i: (b * nl + i, TAIL_BLK)),
                  blk(G_GO), full(TAIL_W, GW), full(1, GW), full(1, GW), full(GW, GW), full(tg, tg), full(GW, GW)],
        out_specs=[pl.BlockSpec((tg, GW), lambda b, i: (b * nl + i, 0)),
                   pl.BlockSpec((None, GW, GW), lambda b, i: (b, 0, 0))],
        out_shape=[jax.ShapeDtypeStruct((m, GW), BF16), jax.ShapeDtypeStruct((nb, GW, GW), F32)],
        scratch_shapes=[pltpu.VMEM((GW, GW), F32), pltpu.VMEM((GLA_CHUNK * GLA_CHUNK, GW), BF16),
                        pltpu.VMEM((tg, GW), F32)],
        compiler_params=_cp(("arbitrary", "arbitrary")),
        name="gla_prompt",
    )(z, z, z, z, z, w2p, b_gate.reshape(1, GW), norm_g.reshape(1, GW), hs, cs, hs)


def _ret_lg_row():
    lg = np.log(1.0 - np.exp2(-5.0 - np.arange(NH, dtype=np.float32))).astype(np.float32)
    return np.repeat(lg, HD).reshape(1, GW)


def _ret_decay_mask(c):
    lg = _ret_lg_row()[0, ::HD]
    i = np.arange(c, dtype=np.float32)
    rel = i[:, None] - i[None, :]
    return np.where(rel >= 0, np.exp(np.maximum(rel, 0.0)[None] * lg[:, None, None]), 0.0).astype(np.float32)


def _rope(x, cos, sin_signed):
    first = (lax.broadcasted_iota(jnp.int32, (1, GW), 1) & (HD - 1)) < (HD // 2)
    swapped = jnp.where(first, pltpu.roll(x, GW - HD // 2, axis=1), pltpu.roll(x, HD // 2, axis=1))
    return x * cos + swapped * sin_signed


def _ret_kernel(q_ref, k_ref, v_ref, g_ref, cos_ref, sin_ref, lg_ref, gn_ref, hs_ref, dm_ref, o_ref, st_ref, st_scr):
    tr = dm_ref.shape[1]

    @pl.when(pl.program_id(1) == 0)
    def _():
        st_scr[...] = jnp.zeros_like(st_scr)

    hs = hs_ref[...]
    lg = lg_ref[...]
    head = _head_ids((1, GW), 1)
    ri = lax.broadcasted_iota(jnp.int32, (tr, 1), 0).astype(F32)
    dq = jnp.exp((ri + 1.0) * lg)
    dk = jnp.exp((tr - 1.0 - ri) * lg)
    ds = jnp.exp(tr * lg)
    for n in range(q_ref.shape[0] // tr):
        rows = slice(n * tr, (n + 1) * tr)
        cos = cos_ref[rows, :]
        sin = sin_ref[rows, :]
        q = _rope(q_ref[rows, :], cos, sin)
        k = _rope(k_ref[rows, :], cos, sin) * (HD ** -0.5)
        qb = q.astype(BF16)
        kb = k.astype(BF16)
        vb = v_ref[rows, :].astype(BF16)
        o = _dot_nt((q * dq).astype(BF16), st_scr[...].astype(BF16))
        for h in range(NH):
            att = _dot_nt(jnp.where(head == h, qb, jnp.zeros_like(qb)), kb) * dm_ref[h]
            o = o + jnp.where(head == h, _dot(att.astype(BF16), vb), 0.0)
        st_scr[...] = st_scr[...] * ds + _dot_tn(vb, (k * dk).astype(BF16)) * hs
        o_ref[rows, :] = (_head_rms(o, hs, gn_ref[...]) * _silu(g_ref[rows, :])).astype(o_ref.dtype)
    st_ref[...] = st_scr[...]


def _rope_tables(pos):
    half = HD // 2
    inv = 10000.0 ** (-jnp.arange(half, dtype=F32) / half)
    ang = pos[:, None] * inv[None, :]
    cos = jnp.cos(ang)
    sin = jnp.sin(ang)
    cos_t = jnp.tile(jnp.concatenate([cos, cos], axis=1), (1, NH))
    sin_t = jnp.tile(jnp.concatenate([-sin, sin], axis=1), (1, NH))
    return cos_t, sin_t


def ret_prompt(z, nb, l, norm_g, tt):
    m = z.shape[0]
    nl = l // tt
    tr = RET_CHUNK
    hs = jnp.asarray(_np_hsum())
    cos_t, sin_t = _rope_tables(jnp.arange(l, dtype=F32))
    blk = lambda g: pl.BlockSpec((tt, GW), lambda b, i: (b * nl + i, g))
    full = lambda r, c: pl.BlockSpec((r, c), lambda b, i: (0, 0))
    tab = pl.BlockSpec((tt, GW), lambda b, i: (i, 0))
    return pl.pallas_call(
        _ret_kernel,
        grid=(nb, nl),
        in_specs=[blk(G_RQ), blk(G_RK), blk(G_RV), blk(G_RG), tab, tab, full(1, GW), full(1, GW), full(GW, GW),
                  pl.BlockSpec((NH, tr, tr), lambda b, i: (0, 0, 0))],
        out_specs=[pl.BlockSpec((tt, GW), lambda b, i: (b * nl + i, 0)),
                   pl.BlockSpec((None, GW, GW), lambda b, i: (b, 0, 0))],
        out_shape=[jax.ShapeDtypeStruct((m, GW), BF16), jax.ShapeDtypeStruct((nb, GW, GW), F32)],
        scratch_shapes=[pltpu.VMEM((GW, GW), F32)],
        compiler_params=_cp(("arbitrary", "arbitrary")),
        name="ret_prompt",
    )(z, z, z, z, cos_t, sin_t, jnp.asarray(_ret_lg_row()), norm_g.reshape(1, GW), hs, jnp.asarray(_ret_decay_mask(tr)))


def _state_from_blockdiag(st):
    nb = st.shape[0]
    s5 = st.reshape(nb, NH, HD, NH, HD)
    diag = jnp.stack([s5[:, h, :, h, :] for h in range(NH)], axis=1)
    return diag.transpose(0, 1, 3, 2)


def _sample_rows_kernel(z_ref, buf_ref, cw_ref, cb_ref, lg_ref, lb_ref, gq_ref, gk_ref, fb_ref, w2_ref, bg_ref,
                        cos_ref, sin_ref, rlg_ref, hs_ref,
                        oa_ref, cn_ref, lf_ref, col_ref):
    grp = lambda g: z_ref[:, g * GW:(g + 1) * GW]
    tail = z_ref[:, N_MAIN:N_MAIN + TAIL_W]
    hs = hs_ref[...]
    u = grp(G_CA) * jax.nn.sigmoid(grp(G_CG))
    y = cw_ref[CONV_K - 1:CONV_K, :] * u + cb_ref[...]
    for k in range(CONV_K - 1):
        y = y + cw_ref[k:k + 1, :] * buf_ref[k]
    mu = jnp.mean(y, axis=-1, keepdims=True)
    yc = y - mu
    var = jnp.mean(yc * yc, axis=-1, keepdims=True)
    oa_ref[...] = _silu(yc * lax.rsqrt(var + EPS) * lg_ref[...] + lb_ref[...]).astype(oa_ref.dtype)
    for k in range(CONV_K - 2):
        cn_ref[k] = buf_ref[k + 1]
    cn_ref[CONV_K - 2] = u
    col_ref[0] = _head_rms(grp(G_FQ), hs, gq_ref[...]) * (HD ** -0.5)
    col_ref[1] = _head_rms(grp(G_FK), hs, gk_ref[...])
    col_ref[2] = grp(G_FV)
    lf_ref[...] = _log_sigmoid(tail + fb_ref[...])
    glog = _log_sigmoid(_wdot(tail, w2_ref[...]) + bg_ref[...]) * (1.0 / GLA_TAU)
    col_ref[3] = grp(G_GQ) * (HD ** -0.5)
    col_ref[4] = grp(G_GK)
    col_ref[5] = jnp.exp(glog)
    col_ref[6] = _rope(grp(G_RQ), cos_ref[...], sin_ref[...])
    col_ref[7] = _rope(grp(G_RK), cos_ref[...], sin_ref[...]) * (HD ** -0.5)
    col_ref[8] = jnp.broadcast_to(jnp.exp(rlg_ref[...]), col_ref.shape[1:])


def _rec_step_kernel(c_ref, v_ref, s0_ref, s1_ref, o_ref, sn0_ref, sn1_ref):
    for r, (s_ref, sn_ref) in enumerate(((s0_ref, sn0_ref), (s1_ref, sn1_ref))):
        for b in range(v_ref.shape[1]):
            for h in range(NH):
                q = c_ref[3 * r, b, h]
                k = c_ref[3 * r + 1, b, h]
                a = c_ref[3 * r + 2, b, h]
                v = v_ref[r, b, h]
                s = s_ref[b, h]
                qk = jnp.sum(q * k, axis=0, keepdims=True)
                o_ref[r, b, h] = qk * v + jnp.sum((q * a) * s, axis=0, keepdims=True)
                sn_ref[b, h] = a * s + k * v


REC_SEQS = 8


def rec_step(cols, v, state_gla, state_ret, li):
    nb = cols.shape[1]
    bt = math.gcd(nb, REC_SEQS)
    sspec = pl.BlockSpec((None, bt, NH, HD, HD), lambda b: (li, b, 0, 0, 0))
    ospec = pl.BlockSpec((bt, NH, HD, HD), lambda b: (b, 0, 0, 0))
    st = jax.ShapeDtypeStruct((nb, NH, HD, HD), F32)
    o, sn0, sn1 = pl.pallas_call(
        _rec_step_kernel,
        grid=(nb // bt,),
        in_specs=[pl.BlockSpec((6, bt, NH, HD, 1), lambda b: (0, b, 0, 0, 0)),
                  pl.BlockSpec((2, bt, NH, 1, HD), lambda b: (0, b, 0, 0, 0)), sspec, sspec],
        out_specs=[pl.BlockSpec((2, bt, NH, 1, HD), lambda b: (0, b, 0, 0, 0)), ospec, ospec],
        out_shape=[jax.ShapeDtypeStruct((2, nb, NH, 1, HD), F32), st, st],
        compiler_params=_cp(("arbitrary",)),
        name="rec_step",
    )(cols.reshape(6, nb, NH, HD, 1), v.reshape(2, nb, NH, 1, HD), state_gla, state_ret)
    o = o.reshape(2, nb, GW)
    return o[0], sn0, o[1], sn1


FOX_PAGES = 64


def _fox_bias_kernel(pt_ref, lfn_ref, lf_ref, ts_ref, pre_ref, o_ref, lf_scr):
    b = pl.program_id(0)
    n_pages = o_ref.shape[1]
    pg = o_ref.shape[2]
    for p in range(n_pages):
        page = lf_ref[pt_ref[b * n_pages + p]]
        for h in range(NH):
            lf_scr[h * n_pages + p:h * n_pages + p + 1, :] = page[h:h + 1, :]
    both = _dot_sel(lf_scr[...], ts_ref[...], "a")
    suf = both[:, :pg]
    tot = both[:, pg:]
    later = _dot_sel(pre_ref[...], tot, "b")
    for h in range(NH):
        r0, r1 = h * n_pages, (h + 1) * n_pages
        o_ref[h] = lfn_ref[h:h + 1, :] + later[r0:r1] + suf[r0:r1]


def _fox_sample_kernel(pt_ref, q_ref, kn_ref, vn_ref, bias_ref, *rest):
    g_n = bias_ref.shape[1]
    k_refs = rest[:g_n]
    v_refs = rest[g_n:2 * g_n]
    o_ref, m_scr, l_scr, acc_scr, s_scr = rest[2 * g_n:]
    c = pl.program_id(1)
    pg = k_refs[0].shape[1]

    @pl.when(c == 0)
    def _():
        m_scr[...] = jnp.full(m_scr.shape, NEG, F32)
        l_scr[...] = jnp.zeros_like(l_scr)
        acc_scr[...] = jnp.zeros_like(acc_scr)

    qb = jnp.broadcast_to(q_ref[...], (GW, pg))
    for g in range(g_n):
        prod = k_refs[g][...] * qb
        for h in range(NH):
            s_scr[h * g_n + g:h * g_n + g + 1, :] = jnp.sum(prod[h * HD:(h + 1) * HD], axis=0, keepdims=True)
    for h in range(NH):
        s_h = s_scr[h * g_n:(h + 1) * g_n, :] + bias_ref[h]
        m_old = m_scr[h:h + 1, :]
        m_new = jnp.maximum(m_old, jnp.max(jnp.max(s_h, axis=0, keepdims=True), axis=1, keepdims=True))
        alpha = jnp.exp(m_old - m_new)
        p = jnp.exp(s_h - m_new)
        m_scr[h:h + 1, :] = m_new
        l_scr[h:h + 1, :] = l_scr[h:h + 1, :] * alpha + jnp.sum(p, axis=0, keepdims=True)
        acc = acc_scr[h * HD:(h + 1) * HD, :] * alpha
        for g in range(g_n):
            acc = acc + p[g:g + 1, :] * v_refs[g][h * HD:(h + 1) * HD, :]
        acc_scr[h * HD:(h + 1) * HD, :] = acc

    @pl.when(c == pl.num_programs(1) - 1)
    def _():
        prod = q_ref[...] * kn_ref[...]
        for h in range(NH):
            s_self = jnp.sum(prod[h * HD:(h + 1) * HD], axis=0, keepdims=True)
            m_h = m_scr[h:h + 1, 0:1]
            m_fin = jnp.maximum(m_h, s_self)
            a_h = jnp.exp(m_h - m_fin)
            p_self = jnp.exp(s_self - m_fin)
            l_tot = jnp.sum(l_scr[h:h + 1, :], axis=1, keepdims=True) * a_h + p_self
            num = jnp.sum(acc_scr[h * HD:(h + 1) * HD, :], axis=1, keepdims=True) * a_h \
                + p_self * vn_ref[h * HD:(h + 1) * HD, :]
            o_ref[h * HD:(h + 1) * HD, :] = num / l_tot


def fox_sample(qkv, lf_new, cache_kt, cache_vt, cache_lf, page_table, li):
    nb, n_pages = page_table.shape
    n_phys = cache_lf.shape[0]
    pg = cache_kt.shape[-1]
    g_n = math.gcd(FOX_PAGES, n_pages)
    nc = n_pages // g_n
    qkv = qkv.reshape(3, nb, GW, 1)
    pt = page_table.reshape(-1)
    lfn = jnp.broadcast_to(lf_new[:, :, None], (nb, NH, pg))
    t = np.arange(pg)
    ts = jnp.asarray(np.concatenate([(t[:, None] > t[None, :]).astype(np.float32), np.ones((pg, pg), np.float32)], axis=1))
    r = np.arange(NH * n_pages)
    later = jnp.asarray(((r[:, None] // n_pages == r[None, :] // n_pages) & (r[None, :] > r[:, None])).astype(np.float32))
    bias = pl.pallas_call(
        _fox_bias_kernel,
        grid_spec=pltpu.PrefetchScalarGridSpec(
            num_scalar_prefetch=1,
            grid=(nb,),
            in_specs=[pl.BlockSpec((None, NH, pg), lambda b, pt: (b, 0, 0)),
                      pl.BlockSpec((n_phys, None, NH, pg), lambda b, pt: (0, li, 0, 0)),
                      pl.BlockSpec((pg, 2 * pg), lambda b, pt: (0, 0)),
                      pl.BlockSpec((NH * n_pages, NH * n_pages), lambda b, pt: (0, 0))],
            out_specs=pl.BlockSpec((None, NH, n_pages, pg), lambda b, pt: (b, 0, 0, 0)),
            scratch_shapes=[pltpu.VMEM((NH * n_pages, pg), F32)]),
        out_shape=jax.ShapeDtypeStruct((nb, NH, n_pages, pg), F32),
        compiler_params=_cp(("arbitrary",)),
        name="fox_bias",
    )(pt, lfn, cache_lf, ts, later)

    def page(g):
        return lambda b, c, pt: (pt[b * n_pages + c * g_n + g], li, 0, 0)

    cspec = lambda r: pl.BlockSpec((None, None, GW, 1), lambda b, c, pt: (r, b, 0, 0))
    in_specs = [cspec(0), cspec(1), cspec(2), pl.BlockSpec((None, NH, g_n, pg), lambda b, c, pt: (b, 0, c, 0))]
    in_specs += [pl.BlockSpec((None, None, GW, pg), page(g)) for g in range(g_n)]
    in_specs += [pl.BlockSpec((None, None, GW, pg), page(g)) for g in range(g_n)]
    o = pl.pallas_call(
        _fox_sample_kernel,
        grid_spec=pltpu.PrefetchScalarGridSpec(
            num_scalar_prefetch=1,
            grid=(nb, nc),
            in_specs=in_specs,
            out_specs=pl.BlockSpec((None, GW, 1), lambda b, c, pt: (b, 0, 0)),
            scratch_shapes=[pltpu.VMEM((NH, pg), F32), pltpu.VMEM((NH, pg), F32),
                            pltpu.VMEM((GW, pg), F32), pltpu.VMEM((NH * g_n, pg), F32)]),
        out_shape=jax.ShapeDtypeStruct((nb, GW, 1), F32),
        compiler_params=_cp(("arbitrary", "arbitrary")),
        name="fox_sample",
    )(pt, qkv, qkv, qkv, bias, *([cache_kt] * g_n), *([cache_vt] * g_n))
    return o.reshape(nb, GW)


def _sample_out_kernel(oa_ref, ob_ref, oc_ref, od_ref, go_ref, rg_ref, gng_ref, rng_ref, hs_ref, w_ref, x_ref, g1_ref, o_ref):
    hs = hs_ref[...]
    oc = _head_rms(oc_ref[...], hs, gng_ref[...]) * _silu(go_ref[...])
    od = _head_rms(od_ref[...], hs, rng_ref[...]) * _silu(rg_ref[...])
    cat = jnp.concatenate([oa_ref[...], ob_ref[...], oc, od], axis=1)
    o_ref[...] = x_ref[...] + g1_ref[...] * _wdot(cat, w_ref[...])


def _prep_w_in(w_in, li):
    wt = w_in.transpose(2, 0, 1)[:, li, :]
    o_ff = 5 * GW
    o_lr = o_ff + NH + 3 * GW
    return jnp.concatenate([wt[:o_ff], wt[o_ff + NH:o_lr], wt[o_lr + GLA_LR:],
                            wt[o_ff:o_ff + NH], wt[o_lr:o_lr + GLA_LR],
                            jnp.zeros((TAIL_W - NH - GLA_LR, wt.shape[1]), wt.dtype)], axis=0)


def _ffn_apply(li, x, mix, mod, g, fw, tm, rows_per_group, n_valid):
    if li % 2 == 0:
        w1, w3, w2 = fw
        tf = w1.shape[1] if w1.dtype == BF16 else w1.shape[1] // 2
        return ffn_dense(x, mix, mod, li, g, w1, w3, w2, tm, tf, rows_per_group)
    router_t, w1, w3, w2 = fw
    if mix is not None:
        x = out_proj(x, mix, mod, li, 512, rows_per_group)
    return moe_ffn(x, mod, li, g, router_t, w1, w3, w2, tm, w1.shape[2] // 2, rows_per_group, n_valid)


def kernel(x_prompt, x_sample, c_prompt, c_sample, state_conv, cache_k, cache_v, cache_logf, state_gla, state_ret, page_table, w_in, w_out, conv_w, conv_b, conv_ln_g, conv_ln_b, fox_qn_g, fox_kn_g, fox_fb, gla_w_gate2, gla_b_gate, gla_norm_g, ret_norm_g, norm1_g, norm2_g, w_ada, b_ada, ffn_w1, ffn_w3, ffn_w2, moe_router, moe_w1, moe_w3, moe_w2):
    nbp, l, d = x_prompt.shape
    nbs = x_sample.shape[0]
    depth = w_in.shape[0]
    n_pages, pg = page_table.shape[1], cache_k.shape[2]
    p_len = n_pages * pg
    mp = nbp * l
    ms = -(-nbs // 8) * 8
    ms_moe = 128

    mod = ada_mod(jnp.concatenate([c_prompt, c_sample], axis=0), w_ada, b_ada)
    mod = mod.reshape(depth, nbp + nbs, 6, d).transpose(0, 2, 1, 3)
    mod_p = mod[:, :, :nbp].reshape(depth, 6, nbp, 1, d)
    mod_s = jnp.zeros((depth, 6, ms_moe, d), F32).at[:, :, :nbs].set(mod[:, :, nbp:])
    cache_kt = cache_k.transpose(0, 1, 3, 4, 2).reshape(cache_k.shape[0], depth, GW, pg)
    cache_vt = cache_v.transpose(0, 1, 3, 4, 2).reshape(cache_v.shape[0], depth, GW, pg)
    cache_lf = cache_logf.transpose(0, 1, 3, 2)
    conv_state = state_conv.transpose(0, 2, 1, 3)
    cos_s, sin_s = _rope_tables(jnp.full((1,), p_len, F32))
    hs = jnp.asarray(_np_hsum())
    rlg = jnp.asarray(_ret_lg_row())

    xp = x_prompt.reshape(mp, d)
    xs = jnp.zeros((ms, d), F32).at[:nbs].set(x_sample.reshape(nbs, d))
    outs_p, outs_s = [], []
    k_all = jnp.zeros((nbp, depth, GW, l), F32)
    v_all = jnp.zeros((nbp, depth, GW, l), F32)
    lf_all = jnp.zeros((nbp, depth, NH, l), F32)
    for li in range(depth):
        n1 =norm1_g[li].reshape(1, d)
        n2 = norm2_g[li].reshape(1, d)
        w_in_f = _prep_w_in(w_in, li)
        w_in_b = w_in_f.astype(BF16)
        w_out_f = w_out[li]
        w_out_b = w_out_f.astype(BF16)
        if li % 2 == 0:
            fw_s = (ffn_w1[li // 2], ffn_w3[li // 2], ffn_w2[li // 2])
            fw = tuple(w.astype(BF16) for w in fw_s)
        else:
            fw = (jnp.pad(moe_router[li // 2], ((0, 0), (0, 128 - N_EXPERTS))), moe_w1[li // 2].astype(BF16),
                  moe_w3[li // 2].astype(BF16), moe_w2[li // 2].astype(BF16))
            fw_s = fw

        z = in_proj(xp, mod_p, li, n1, w_in_b, 512, l)
        out_a, conv_new = conv_prompt(z, nbp, l, conv_w[li], conv_b[li], conv_ln_g[li], conv_ln_b[li], 512)
        out_b, k_all, v_all, lf_all = fox_prompt(z, nbp, l, fox_qn_g[li], fox_kn_g[li], fox_fb[li], 512, li,
                                                 k_all, v_all, lf_all)
        out_c, gla_st = gla_prompt(z, nbp, l, gla_w_gate2[li], gla_b_gate[li], gla_norm_g[li], 256)
        out_d, ret_st = ret_prompt(z, nbp, l, ret_norm_g[li], 512)
        xp = _ffn_apply(li, xp, ((out_a, out_b, out_c, out_d), w_out_b), mod_p, n2, fw,
                        256 if li % 2 == 0 else 1024, l, mp)
        outs_p.append((conv_new, None, None, None, _state_from_blockdiag(gla_st), _state_from_blockdiag(ret_st)))

        zs = in_proj(xs, mod_s, li, n1, w_in_f, ms, ms)[:nbs]
        row = lambda v_: v_.reshape(1, GW)
        w2p = jnp.zeros((TAIL_W, GW), F32).at[NH:NH + GLA_LR].set(gla_w_gate2[li])
        fbp = jnp.zeros((1, TAIL_W), F32).at[0, :NH].set(fox_fb[li])
        r32 = jax.ShapeDtypeStruct((nbs, GW), F32)
        oa, cn, lfs, cols = pl.pallas_call(
            _sample_rows_kernel,
            out_shape=[r32, jax.ShapeDtypeStruct((CONV_K - 1, nbs, GW), F32),
                       jax.ShapeDtypeStruct((nbs, TAIL_W), F32), jax.ShapeDtypeStruct((9, nbs, GW), F32)],
            compiler_params=pltpu.CompilerParams(vmem_limit_bytes=VMEM_LIMIT),
            name="sample_rows",
        )(zs, conv_state[li], conv_w[li], row(conv_b[li]), row(conv_ln_g[li]), row(conv_ln_b[li]),
          row(jnp.tile(fox_qn_g[li], NH)), row(jnp.tile(fox_kn_g[li], NH)), fbp, w2p, row(gla_b_gate[li]),
          cos_s, sin_s, rlg, hs)
        grp = lambda g_: zs[:, g_ * GW:(g_ + 1) * GW]
        fk, f_v = cols[1], cols[2]
        ob = fox_sample(cols[:3], lfs[:, :NH], cache_kt, cache_vt, cache_lf, page_table, li)
        oc, gla_new, od, ret_new = rec_step(cols[3:], jnp.stack([grp(G_GV), grp(G_RV)]), state_gla, state_ret, li)
        pad = lambda v_: jnp.zeros((ms, v_.shape[1]), F32).at[:nbs].set(v_)
        xs = pl.pallas_call(
            _sample_out_kernel,
            out_shape=jax.ShapeDtypeStruct((ms, d), F32),
            compiler_params=pltpu.CompilerParams(vmem_limit_bytes=VMEM_LIMIT),
            name="sample_out",
        )(pad(oa), pad(ob), pad(oc), pad(od), pad(grp(G_GO)), pad(grp(G_RG)), row(gla_norm_g[li]), row(ret_norm_g[li]),
          hs, w_out_f, xs, mod_s[li, MOD_G1, :ms])
        if li % 2 == 0:
            xs = _ffn_apply(li, xs, None, mod_s, n2, fw_s, ms, ms, nbs)
        else:
            wide = jnp.zeros((ms_moe, d), F32).at[:ms].set(xs)
            xs = _ffn_apply(li, wide, None, mod_s, n2, fw_s, ms_moe, ms_moe, nbs)[:ms]
        outs_s.append((cn.transpose(1, 0, 2), fk.reshape(nbs, 1, NH, HD), f_v.reshape(nbs, 1, NH, HD),
                       lfs[:, :NH].reshape(nbs, 1, NH), gla_new, ret_new))

    st = lambda lst, i, ax: jnp.stack([s[i] for s in lst], axis=ax)
    heads_last = lambda a: a.reshape(nbp, depth, NH, HD, l).transpose(0, 1, 4, 2, 3)
    return (xp.reshape(nbp, l, d), xs[:nbs].reshape(nbs, 1, d),
            st(outs_p, 0, 0), st(outs_s, 0, 0),
            heads_last(k_all), st(outs_s, 1, 1),
            heads_last(v_all), st(outs_s, 2, 1),
            lf_all.transpose(0, 1, 3, 2), st(outs_s, 3, 1),
            st(outs_p, 4, 0), st(outs_s, 4, 0),
            st(outs_p, 5, 0), st(outs_s, 5, 0))
```
